```python
import jax
import jax.numpy as jnp
from jax import lax
import numpy as np

D_MODEL = 1024
BATCH = 8
SEQ = 4096
DEPTH = 2

N_Q_HEADS = 8
N_KV_HEADS = 2
Q_PER_KV = N_Q_HEADS // N_KV_HEADS
HEAD_DIM = 64
WINDOW = 128
ATTN_BLOCK = 128
ATTN_Q_WIDTH = N_Q_HEADS * HEAD_DIM
ATTN_KV_WIDTH = N_KV_HEADS * HEAD_DIM
N_RET_HEADS = 8
RET_KEY_DIM = 32
RET_VAL_DIM = 64
RET_CHUNK = 128
RET_QK_WIDTH = N_RET_HEADS * RET_KEY_DIM
RET_V_WIDTH = N_RET_HEADS * RET_VAL_DIM
D_FF_DENSE = 2816
N_EXPERTS = 8
TOP_K = 2
D_FF_EXPERT = 1408
N_DENSE_LAYERS = (DEPTH + 1) // 2
N_MOE_LAYERS = DEPTH // 2
IN_SIZES = (ATTN_Q_WIDTH, ATTN_KV_WIDTH, ATTN_KV_WIDTH,
            RET_QK_WIDTH, RET_QK_WIDTH, RET_V_WIDTH, RET_V_WIDTH, RET_V_WIDTH,
            2 * D_MODEL)
IN_WIDTH = sum(IN_SIZES)
DEEPNORM_ALPHA = (2 * DEPTH) ** 0.25
DEEPNORM_BETA = (8 * DEPTH) ** -0.25
LN_EPS = 1e-5
GN_EPS = 1e-5
NEG_INF = -1e30

kernel_name = 'hybrid_swa_retention_moe_deepnorm'


def layer_norm(x, g, b):
    xf = x.astype(jnp.float32)
    mu = jnp.mean(xf, axis=-1, keepdims=True)
    var = jnp.mean(jnp.square(xf - mu), axis=-1, keepdims=True)
    return ((xf - mu) * lax.rsqrt(var + LN_EPS)).astype(x.dtype) * g + b


def split_columns(proj, sizes):
    parts, start = [], 0
    for size in sizes:
        parts.append(proj[..., start:start + size])
        start += size
    return parts


def alibi_slopes(n_heads):
    return jnp.exp2(-8.0 * jnp.arange(1, n_heads + 1, dtype=jnp.float32) / n_heads)


def windowed_gqa(q, k, v, sink_logits):
    B, S, _ = q.shape
    C = ATTN_BLOCK
    nb = S // C
    qb = q.reshape(B, nb, C, N_KV_HEADS, Q_PER_KV, HEAD_DIM)
    pad = ((0, 0), (C, C), (0, 0))
    kp = jnp.pad(k, pad).reshape(B, nb + 2, C, N_KV_HEADS, HEAD_DIM)
    vp = jnp.pad(v, pad).reshape(B, nb + 2, C, N_KV_HEADS, HEAD_DIM)
    kw = jnp.concatenate([kp[:, :-2], kp[:, 1:-1], kp[:, 2:]], axis=2)
    vw = jnp.concatenate([vp[:, :-2], vp[:, 1:-1], vp[:, 2:]], axis=2)
    scores = jnp.einsum('bnqgrd,bnkgd->bngrqk', qb, kw,
                        preferred_element_type=jnp.float32) * (HEAD_DIM ** -0.5)
    qi = jnp.arange(C)
    kj = jnp.arange(3 * C)
    dist = jnp.abs(qi[:, None] - kj[None, :] + C)
    key_pos = (jnp.arange(nb)[:, None] - 1) * C + kj[None, :]
    valid = (dist <= WINDOW)[None] & ((key_pos >= 0) & (key_pos < S))[:, None, :]
    slopes = alibi_slopes(N_Q_HEADS).reshape(N_KV_HEADS, Q_PER_KV)
    scores = scores - slopes[:, :, None, None] * dist.astype(jnp.float32)
    scores = jnp.where(valid[None, :, None, None], scores, NEG_INF)
    sink = jnp.broadcast_to(
        sink_logits.astype(jnp.float32).reshape(N_KV_HEADS, Q_PER_KV)[:, :, None, None],
        scores.shape[:-1] + (1,))
    probs = jax.nn.softmax(jnp.concatenate([scores, sink], axis=-1), axis=-1)[..., :-1]
    out = jnp.einsum('bngrqk,bnkgd->bnqgrd', probs.astype(v.dtype), vw)
    return out.reshape(B, S, ATTN_Q_WIDTH)


def head_group_norm(y):
    mu = jnp.mean(y, axis=-1, keepdims=True)
    var = jnp.mean(jnp.square(y - mu), axis=-1, keepdims=True)
    return (y - mu) * lax.rsqrt(var + GN_EPS)


def retention_one_direction(q, k, v, log_gamma, include_diag):
    B, S, H, dk = q.shape
    dv = v.shape[-1]
    C = RET_CHUNK
    nc = S // C
    qc = q.reshape(B, nc, C, H, dk)
    kc = k.reshape(B, nc, C, H, dk)
    vc = v.reshape(B, nc, C, H, dv)
    pos_i = jnp.arange(C)
    diff_i = pos_i[:, None] - pos_i[None, :]
    mask = (diff_i >= 0) if include_diag else (diff_i > 0)
    pos = pos_i.astype(jnp.float32)
    diff = jnp.maximum(diff_i, 0).astype(jnp.float32)
    intra_decay = jnp.where(mask[None], jnp.exp(diff[None] * log_gamma[:, None, None]), 0.0)
    q_decay = jnp.exp((pos + 1.0)[None] * log_gamma[:, None])
    k_decay = jnp.exp((C - 1.0 - pos)[None] * log_gamma[:, None])
    chunk_decay = jnp.exp(C * log_gamma)
    qk = jnp.einsum('bnqhd,bnkhd->bnhqk', qc, kc) * intra_decay
    intra = jnp.einsum('bnhqk,bnkhe->bnqhe', qk, vc)
    chunk_kv = jnp.einsum('bnkhd,hk,bnkhe->bnhde', kc, k_decay, vc)

    def step(state, kv):
        return chunk_decay[None, :, None, None] * state + kv, state

    init = jnp.zeros((B, H, dk, dv), jnp.float32)
    _, prev = lax.scan(step, init, jnp.moveaxis(chunk_kv, 1, 0))
    prev = jnp.moveaxis(prev, 0, 1)
    cross = jnp.einsum('bnqhd,bnhde,hq->bnqhe', qc, prev, q_decay)
    return (intra + cross).reshape(B, S, H, dv)


def bidirectional_retention(q, k, v, g_f, g_b, decay_fwd, decay_bwd):
    B, S, _ = q.shape
    qh = q.reshape(B, S, N_RET_HEADS, RET_KEY_DIM).astype(jnp.float32)
    kh = k.reshape(B, S, N_RET_HEADS, RET_KEY_DIM).astype(jnp.float32) * (RET_KEY_DIM ** -0.5)
    vh = v.reshape(B, S, N_RET_HEADS, RET_VAL_DIM).astype(jnp.float32)
    lg_f = jax.nn.log_sigmoid(decay_fwd.astype(jnp.float32))
    lg_b = jax.nn.log_sigmoid(decay_bwd.astype(jnp.float32))
    y_f = retention_one_direction(qh, kh, vh, lg_f, True)
    y_b = jnp.flip(retention_one_direction(jnp.flip(qh, 1), jnp.flip(kh, 1), jnp.flip(vh, 1),
                                           lg_b, False), axis=1)
    n_f = head_group_norm(y_f).reshape(B, S, RET_V_WIDTH).astype(g_f.dtype)
    n_b = head_group_norm(y_b).reshape(B, S, RET_V_WIDTH).astype(g_b.dtype)
    return jax.nn.silu(g_f) * n_f + jax.nn.silu(g_b) * n_b


def hybrid_mixer(h, w_in, b_gate, sink_logits, decay_fwd, decay_bwd, w_o_attn, w_o_ret, w_out):
    q_a, k_a, v_a, q_r, k_r, v_r, g_f, g_b, gate_logits = split_columns(h @ w_in, IN_SIZES)
    y_attn = windowed_gqa(q_a, k_a, v_a, sink_logits) @ w_o_attn
    y_ret = bidirectional_retention(q_r, k_r, v_r, g_f, g_b, decay_fwd, decay_bwd) @ w_o_ret
    gates = jax.nn.sigmoid(gate_logits + b_gate)
    merged = gates[..., :D_MODEL] * y_attn + gates[..., D_MODEL:] * y_ret
    return merged @ w_out


def swiglu(x, w1, w3, w2):
    return (jax.nn.silu(x @ w1) * (x @ w3)) @ w2


def moe_swiglu(x, router_w, router_b, w1, w3, w2):
    logits = (x @ router_w).astype(jnp.float32) + router_b.astype(jnp.float32)
    top_vals, top_idx = lax.top_k(logits, TOP_K)
    top_w = jax.nn.softmax(top_vals, axis=-1)
    gate = jnp.sum(jax.nn.one_hot(top_idx, N_EXPERTS, dtype=jnp.float32) * top_w[..., None], axis=-2)
    gate = gate.astype(x.dtype)
    out = jnp.zeros_like(x)
    for e in range(N_EXPERTS):
        out = out + gate[..., e:e + 1] * swiglu(x, w1[e], w3[e], w2[e])
    return out


def setup_inputs(seed: int = 0) -> dict:
    key = jax.random.key(seed)
    ks = jax.random.split(key, 24)
    f32 = jnp.float32
    beta = DEEPNORM_BETA

    def normal(k, shape, scale):
        return jax.random.normal(k, shape, f32) * scale

    col_mult = (1.0, 1.0, beta, 1.0, 1.0, beta, 1.0, 1.0, 1.0)
    col_scale = jnp.concatenate([jnp.full((s,), c, f32) for s, c in zip(IN_SIZES, col_mult)])
    base_gamma = 1.0 - jnp.exp2(-5.0 - jnp.arange(N_RET_HEADS, dtype=f32))
    base_logit = jnp.log(base_gamma) - jnp.log1p(-base_gamma)
    return {
        'x': normal(ks[0], (BATCH, SEQ, D_MODEL), 1.0),
        'ln_emb_g': 1.0 + normal(ks[1], (D_MODEL,), 0.02),
        'ln_emb_b': normal(ks[2], (D_MODEL,), 0.02),
        'w_in': normal(ks[3], (DEPTH, D_MODEL, IN_WIDTH), D_MODEL ** -0.5) * col_scale,
        'b_gate': normal(ks[4], (DEPTH, 2 * D_MODEL), 0.02),
        'sink_logits': normal(ks[5], (DEPTH, N_Q_HEADS), 0.5),
        'decay_fwd': base_logit + normal(ks[6], (DEPTH, N_RET_HEADS), 0.1),
        'decay_bwd': base_logit + normal(ks[7], (DEPTH, N_RET_HEADS), 0.1),
        'w_o_attn': normal(ks[8], (DEPTH, ATTN_Q_WIDTH, D_MODEL), beta * ATTN_Q_WIDTH ** -0.5),
        'w_o_ret': normal(ks[9], (DEPTH, RET_V_WIDTH, D_MODEL), beta * RET_V_WIDTH ** -0.5),
        'w_out': normal(ks[10], (DEPTH, D_MODEL, D_MODEL), beta * D_MODEL ** -0.5),
        'ln1_g': 1.0 + normal(ks[11], (DEPTH, D_MODEL), 0.02),
        'ln1_b': normal(ks[12], (DEPTH, D_MODEL), 0.02),
        'ffn_w1': normal(ks[13], (N_DENSE_LAYERS, D_MODEL, D_FF_DENSE), beta * D_MODEL ** -0.5),
        'ffn_w3': normal(ks[14], (N_DENSE_LAYERS, D_MODEL, D_FF_DENSE), beta * D_MODEL ** -0.5),
        'ffn_w2': normal(ks[15], (N_DENSE_LAYERS, D_FF_DENSE, D_MODEL), beta * D_FF_DENSE ** -0.5),
        'router_w': normal(ks[16], (N_MOE_LAYERS, D_MODEL, N_EXPERTS), D_MODEL ** -0.5),
        'router_b': normal(ks[17], (N_MOE_LAYERS, N_EXPERTS), 0.01),
        'moe_w1': normal(ks[18], (N_MOE_LAYERS, N_EXPERTS, D_MODEL, D_FF_EXPERT), beta * D_MODEL ** -0.5),
        'moe_w3': normal(ks[19], (N_MOE_LAYERS, N_EXPERTS, D_MODEL, D_FF_EXPERT), beta * D_MODEL ** -0.5),
        'moe_w2': normal(ks[20], (N_MOE_LAYERS, N_EXPERTS, D_FF_EXPERT, D_MODEL), beta * D_FF_EXPERT ** -0.5),
        'ln2_g': 1.0 + normal(ks[21], (DEPTH, D_MODEL), 0.02),
        'ln2_b': normal(ks[22], (DEPTH, D_MODEL), 0.02),
    }


def reference(x, ln_emb_g, ln_emb_b, w_in, b_gate, sink_logits, decay_fwd, decay_bwd,
              w_o_attn, w_o_ret, w_out, ln1_g, ln1_b, ffn_w1, ffn_w3, ffn_w2,
              router_w, router_b, moe_w1, moe_w3, moe_w2, ln2_g, ln2_b):
    x = layer_norm(x, ln_emb_g, ln_emb_b)
    for layer in range(DEPTH):
        mix = hybrid_mixer(x, w_in[layer], b_gate[layer], sink_logits[layer],
                           decay_fwd[layer], decay_bwd[layer],
                           w_o_attn[layer], w_o_ret[layer], w_out[layer])
        x = layer_norm(DEEPNORM_ALPHA * x + mix, ln1_g[layer], ln1_b[layer])
        i = layer // 2
        if layer % 2 == 0:
            ffn = swiglu(x, ffn_w1[i], ffn_w3[i], ffn_w2[i])
        else:
            ffn = moe_swiglu(x, router_w[i], router_b[i], moe_w1[i], moe_w3[i], moe_w2[i])
        x = layer_norm(DEEPNORM_ALPHA * x + ffn, ln2_g[layer], ln2_b[layer])
    return x
```

```python
import functools

import jax
import jax.numpy as jnp
from jax import lax
from jax.experimental import pallas as pl
from jax.experimental.pallas import tpu as pltpu

F32 = jnp.float32
BF16 = jnp.bfloat16

D_MODEL = 1024
DEPTH = 2
N_Q_HEADS = 8
N_KV_HEADS = 2
HEAD_DIM = 64
WINDOW = 128
BLK = 128
ATTN_Q_WIDTH = N_Q_HEADS * HEAD_DIM
ATTN_KV_WIDTH = N_KV_HEADS * HEAD_DIM
N_RET_HEADS = 8
RET_KEY_DIM = 32
RET_VAL_DIM = 64
RET_QK_WIDTH = N_RET_HEADS * RET_KEY_DIM
RET_V_WIDTH = N_RET_HEADS * RET_VAL_DIM
RET_WIDTH = 2 * RET_QK_WIDTH + 3 * RET_V_WIDTH
D_FF_DENSE = 2816
N_EXPERTS = 8
D_FF_EXPERT = 1408
IN_WIDTH = ATTN_Q_WIDTH + 2 * ATTN_KV_WIDTH + RET_WIDTH + 2 * D_MODEL
DEEPNORM_ALPHA = (2 * DEPTH) ** 0.25
LN_EPS = 1e-5
GN_EPS = 1e-5
NEG_INF = -1e30

TOKEN_TILE = 512
MOE_TOKEN_TILE = 1024
FF_CHUNK = 256
VMEM_LIMIT_BYTES = 56 * 1024 * 1024


def _params(*sem):
    return pltpu.CompilerParams(dimension_semantics=sem, vmem_limit_bytes=VMEM_LIMIT_BYTES)


def _layer_norm(x, g, b):
    mu = jnp.mean(x, axis=-1, keepdims=True)
    d = x - mu
    var = jnp.mean(d * d, axis=-1, keepdims=True)
    return d * lax.rsqrt(var + LN_EPS) * g + b


def _sigmoid(x):
    return 1.0 / (1.0 + jnp.exp(-x))


def _dot(a, b):
    return jnp.dot(a, b, preferred_element_type=F32)


def _dot_nt(a, b):
    return lax.dot_general(a, b, (((1,), (1,)), ((), ())), preferred_element_type=F32)


def _in_proj_kernel(apply_ln, *refs):
    if apply_ln:
        x_ref, g_ref, b_ref, w_ref, h_ref, qa_ref, kva_ref, ret_ref, gl_ref = refs
        x = _layer_norm(x_ref[...], g_ref[...], b_ref[...])
        h_ref[...] = x
    else:
        x_ref, w_ref, qa_ref, kva_ref, ret_ref, gl_ref = refs
        x = x_ref[...]
    xb = x.astype(BF16)

    def proj(lo, hi):
        return _dot(xb, w_ref[:, lo:hi])

    c0 = ATTN_Q_WIDTH
    c1 = c0 + 2 * ATTN_KV_WIDTH
    qa_ref[...] = (proj(0, c0) * (HEAD_DIM ** -0.5)).astype(BF16)
    kva_ref[...] = proj(c0, c1).astype(BF16)
    ret_ref[:, :RET_QK_WIDTH] = proj(c1, c1 + RET_QK_WIDTH).astype(BF16)
    ret_ref[:, RET_QK_WIDTH:2 * RET_QK_WIDTH] = (
        proj(c1 + RET_QK_WIDTH, c1 + 2 * RET_QK_WIDTH) * (RET_KEY_DIM ** -0.5)).astype(BF16)
    ret_ref[:, 2 * RET_QK_WIDTH:] = proj(c1 + 2 * RET_QK_WIDTH, c1 + RET_WIDTH).astype(BF16)
    gl_ref[...] = proj(c1 + RET_WIDTH, IN_WIDTH).astype(BF16)


def _in_proj(x, w, ln=None):
    t = x.shape[0]
    tm = TOKEN_TILE
    row = lambda i: (i, 0)
    const = lambda i: (0, 0)
    in_specs = [pl.BlockSpec((tm, D_MODEL), row)]
    args = [x]
    out_shape = []
    out_specs = []
    if ln is not None:
        in_specs += [pl.BlockSpec((1, D_MODEL), const), pl.BlockSpec((1, D_MODEL), const)]
        args += [ln[0].reshape(1, D_MODEL), ln[1].reshape(1, D_MODEL)]
        out_shape.append(jax.ShapeDtypeStruct((t, D_MODEL), F32))
        out_specs.append(pl.BlockSpec((tm, D_MODEL), row))
    in_specs.append(pl.BlockSpec((D_MODEL, IN_WIDTH), const))
    args.append(w)
    for width in (ATTN_Q_WIDTH, 2 * ATTN_KV_WIDTH, RET_WIDTH, 2 * D_MODEL):
        out_shape.append(jax.ShapeDtypeStruct((t, width), BF16))
        out_specs.append(pl.BlockSpec((tm, width), row))
    return pl.pallas_call(
        functools.partial(_in_proj_kernel, ln is not None),
        grid=(t // tm,),
        in_specs=in_specs,
        out_specs=out_specs,
        out_shape=out_shape,
        compiler_params=_params("parallel"),
        name="in_proj",
    )(*args)


def _head_block_mask(rows, cols, row_div, col_div):
    r = lax.broadcasted_iota(jnp.int32, (rows, cols), 0) // row_div
    c = lax.broadcasted_iota(jnp.int32, (rows, cols), 1) // col_div
    return r == c


def _ret_state_kernel(kf_ref, vf_ref, kb_ref, vb_ref, kdf_ref, kdb_ref, cdf_ref, cdb_ref,
                      pf_ref, pb_ref, sf_ref, sb_ref):
    @pl.when(pl.program_id(1) == 0)
    def _():
        sf_ref[...] = jnp.zeros_like(sf_ref)
        sb_ref[...] = jnp.zeros_like(sb_ref)

    pf_ref[...] = sf_ref[...]
    pb_ref[...] = sb_ref[...]
    mask = _head_block_mask(RET_QK_WIDTH, RET_V_WIDTH, RET_KEY_DIM, RET_VAL_DIM)

    def update(k_ref, v_ref, kd_ref, cd_ref, s_ref):
        kz = (k_ref[...].astype(F32) * kd_ref[...]).T.astype(BF16)
        full = jnp.where(mask, _dot(kz, v_ref[...]), 0.0)
        comp = full[0:RET_KEY_DIM]
        for h in range(1, N_RET_HEADS):
            comp = comp + full[h * RET_KEY_DIM:(h + 1) * RET_KEY_DIM]
        s_ref[...] = s_ref[...] * cd_ref[...] + comp

    update(kf_ref, vf_ref, kdf_ref, cdf_ref, sf_ref)
    update(kb_ref, vb_ref, kdb_ref, cdb_ref, sb_ref)


def _ret_state(ret, tabs, batch, nb):
    fwd = lambda col: (lambda b, i: (b * nb + i, col))
    bwd = lambda col: (lambda b, i: (b * nb + nb - 1 - i, col))
    const = lambda b, i: (0, 0)
    state_shape = jax.ShapeDtypeStruct((batch, nb, RET_KEY_DIM, RET_V_WIDTH), F32)
    return pl.pallas_call(
        _ret_state_kernel,
        grid=(batch, nb),
        in_specs=[
            pl.BlockSpec((BLK, RET_QK_WIDTH), fwd(1)),
            pl.BlockSpec((BLK, RET_V_WIDTH), fwd(1)),
            pl.BlockSpec((BLK, RET_QK_WIDTH), bwd(1)),
            pl.BlockSpec((BLK, RET_V_WIDTH), bwd(1)),
            pl.BlockSpec((BLK, RET_QK_WIDTH), const),
            pl.BlockSpec((BLK, RET_QK_WIDTH), const),
            pl.BlockSpec((1, RET_V_WIDTH), const),
            pl.BlockSpec((1, RET_V_WIDTH), const),
        ],
        out_specs=[
            pl.BlockSpec((None, None, RET_KEY_DIM, RET_V_WIDTH), lambda b, i: (b, i, 0, 0)),
            pl.BlockSpec((None, None, RET_KEY_DIM, RET_V_WIDTH), lambda b, i: (b, nb - 1 - i, 0, 0)),
        ],
        out_shape=[state_shape, state_shape],
        scratch_shapes=[pltpu.VMEM((RET_KEY_DIM, RET_V_WIDTH), F32),
                        pltpu.VMEM((RET_KEY_DIM, RET_V_WIDTH), F32)],
        compiler_params=_params("arbitrary", "arbitrary"),
        name="ret_state",
    )(ret, ret, ret, ret, tabs["kdec_f"], tabs["kdec_b"], tabs["cdec_f"], tabs["cdec_b"])


def _split_bf16(y):
    hi = y.astype(BF16)
    lo = (y - hi.astype(F32)).astype(BF16)
    return hi, lo


def _mixer_kernel(nb, sink_ref, qa_ref, kvp_ref, kvc_ref, kvn_ref, ret_ref, pf_ref, pb_ref,
                  bias_ref, df_ref, db_ref, qdf_ref, qdb_ref, gavg_ref, ya_ref, r_ref):
    n = pl.program_id(1)
    var = jnp.where(n == 0, 0, jnp.where(n == nb - 1, 2, 1))

    q = qa_ref[...]
    kv = jnp.concatenate([kvp_ref[...], kvc_ref[...], kvn_ref[...]], axis=0).astype(F32)
    lo_half = lax.broadcasted_iota(jnp.int32, (3 * BLK, 2 * HEAD_DIM), 1) < HEAD_DIM

    def lane_variants(a):
        swapped = pltpu.roll(a, HEAD_DIM, 1)
        z = jnp.zeros_like(a)
        return (((jnp.where(lo_half, a, z)).astype(BF16), (jnp.where(lo_half, z, swapped)).astype(BF16)),
                ((jnp.where(lo_half, swapped, z)).astype(BF16), (jnp.where(lo_half, z, a)).astype(BF16)))

    k_var = lane_variants(kv[:, :ATTN_KV_WIDTH])
    v_var = lane_variants(kv[:, ATTN_KV_WIDTH:])
    for p in range(N_Q_HEADS // 2):
        g = p // (N_Q_HEADS // (2 * N_KV_HEADS))
        qp = q[:, 2 * HEAD_DIM * p:2 * HEAD_DIM * (p + 1)]
        out = None
        for r in range(2):
            h = 2 * p + r
            sink = sink_ref[h]
            s = _dot_nt(qp, k_var[g][r]) + bias_ref[var, h]
            m = jnp.maximum(jnp.max(s, axis=1, keepdims=True), sink)
            e = jnp.exp(s - m)
            denom = jnp.sum(e, axis=1, keepdims=True) + jnp.exp(sink - m)
            pv = _dot(e.astype(BF16), v_var[g][r]) / denom
            out = pv if out is None else out + pv
        ya_ref[:, 2 * HEAD_DIM * p:2 * HEAD_DIM * (p + 1)] = out.astype(BF16)

    rq = ret_ref[:, :RET_QK_WIDTH]
    rk = ret_ref[:, RET_QK_WIDTH:2 * RET_QK_WIDTH]
    rv = ret_ref[:, 2 * RET_QK_WIDTH:2 * RET_QK_WIDTH + RET_V_WIDTH]
    k_exp = jnp.where(_head_block_mask(N_RET_HEADS * BLK, RET_QK_WIDTH, BLK, RET_KEY_DIM),
                      jnp.concatenate([rk] * N_RET_HEADS, axis=0), jnp.zeros((), BF16))
    qk = _dot_nt(rq, k_exp)
    p_f = (qk * df_ref[...]).astype(BF16)
    p_b = (qk * db_ref[...]).astype(BF16)
    half_heads = N_RET_HEADS // 2
    v_mask = _head_block_mask(half_heads * BLK, half_heads * RET_VAL_DIM, BLK, RET_VAL_DIM)
    y_f, y_b = [], []
    for g in range(2):
        cols = slice(g * half_heads * RET_VAL_DIM, (g + 1) * half_heads * RET_VAL_DIM)
        v_bd = jnp.where(v_mask, jnp.concatenate([rv[:, cols]] * half_heads, axis=0), jnp.zeros((), BF16))
        rows = slice(g * half_heads * BLK, (g + 1) * half_heads * BLK)
        y_f.append(_dot(p_f[:, rows], v_bd))
        y_b.append(_dot(p_b[:, rows], v_bd))
    s_mask = _head_block_mask(RET_QK_WIDTH, RET_V_WIDTH, RET_KEY_DIM, RET_VAL_DIM)

    def cross(state_ref, qdec_ref):
        s_bd = jnp.where(s_mask, jnp.concatenate([state_ref[...]] * N_RET_HEADS, axis=0), 0.0)
        return _dot(rq, s_bd.astype(BF16)) * qdec_ref[...]

    y_f = jnp.concatenate(y_f, axis=1) + cross(pf_ref, qdf_ref)
    y_b = jnp.concatenate(y_b, axis=1) + cross(pb_ref, qdb_ref)

    y = jnp.concatenate([y_f, y_b], axis=0)

    def group_mean(a):
        hi, lo = _split_bf16(a)
        return _dot(hi, gavg_ref[...]) + _dot(lo, gavg_ref[...])

    d = y - group_mean(y)
    normed = d * lax.rsqrt(group_mean(d * d) + GN_EPS)
    g_f = ret_ref[:, 2 * RET_QK_WIDTH + RET_V_WIDTH:2 * RET_QK_WIDTH + 2 * RET_V_WIDTH].astype(F32)
    g_b = ret_ref[:, 2 * RET_QK_WIDTH + 2 * RET_V_WIDTH:].astype(F32)
    r = g_f * _sigmoid(g_f) * normed[:BLK] + g_b * _sigmoid(g_b) * normed[BLK:]
    r_ref[...] = r.astype(BF16)


def _mixer(qa, kva, ret, prev_f, prev_b, sink, tabs, batch, nb):
    t = qa.shape[0]
    cur = lambda b, n: (b * nb + n, 0)
    prv = lambda b, n: (b * nb + jnp.maximum(n - 1, 0), 0)
    nxt = lambda b, n: (b * nb + jnp.minimum(n + 1, nb - 1), 0)
    c2 = lambda b, n: (0, 0)
    state_spec = pl.BlockSpec((None, None, RET_KEY_DIM, RET_V_WIDTH), lambda b, n: (b, n, 0, 0))
    return pl.pallas_call(
        functools.partial(_mixer_kernel, nb),
        grid=(batch, nb),
        in_specs=[
            pl.BlockSpec(memory_space=pltpu.SMEM),
            pl.BlockSpec((BLK, ATTN_Q_WIDTH), cur),
            pl.BlockSpec((BLK, 2 * ATTN_KV_WIDTH), prv),
            pl.BlockSpec((BLK, 2 * ATTN_KV_WIDTH), cur),
            pl.BlockSpec((BLK, 2 * ATTN_KV_WIDTH), nxt),
            pl.BlockSpec((BLK, RET_WIDTH), cur),
            state_spec,
            state_spec,
            pl.BlockSpec((3, N_Q_HEADS, BLK, 3 * BLK), lambda b, n: (0, 0, 0, 0)),
            pl.BlockSpec((BLK, N_RET_HEADS * BLK), c2),
            pl.BlockSpec((BLK, N_RET_HEADS * BLK), c2),
            pl.BlockSpec((BLK, RET_V_WIDTH), c2),
            pl.BlockSpec((BLK, RET_V_WIDTH), c2),
            pl.BlockSpec((RET_V_WIDTH, RET_V_WIDTH), c2),
        ],
        out_specs=[pl.BlockSpec((BLK, ATTN_Q_WIDTH), cur), pl.BlockSpec((BLK, RET_V_WIDTH), cur)],
        out_shape=[jax.ShapeDtypeStruct((t, ATTN_Q_WIDTH), BF16),
                   jax.ShapeDtypeStruct((t, RET_V_WIDTH), BF16)],
        compiler_params=_params("parallel", "parallel"),
        name="mixer",
    )(sink, qa, kva, kva, kva, ret, prev_f, prev_b, tabs["attn_bias"], tabs["intra_f"],
      tabs["intra_b"], tabs["qdec_f"], tabs["qdec_b"], tabs["group_avg"])


def _out_proj_kernel(h_ref, ya_ref, r_ref, gl_ref, bg_ref, woa_ref, wor_ref, wout_ref,
                     g_ref, b_ref, x_ref):
    gates = _sigmoid(gl_ref[...].astype(F32) + bg_ref[...])
    merged = (gates[:, :D_MODEL] * _dot(ya_ref[...], woa_ref[...])
              + gates[:, D_MODEL:] * _dot(r_ref[...], wor_ref[...]))
    mix = _dot(merged.astype(BF16), wout_ref[...])
    x_ref[...] = _layer_norm(DEEPNORM_ALPHA * h_ref[...] + mix, g_ref[...], b_ref[...])


def _out_proj(h, ya, r, gl, b_gate, woa, wor, wout, g, b):
    t = h.shape[0]
    tm = TOKEN_TILE
    row = lambda i: (i, 0)
    const = lambda i: (0, 0)
    return pl.pallas_call(
        _out_proj_kernel,
        grid=(t // tm,),
        in_specs=[
            pl.BlockSpec((tm, D_MODEL), row),
            pl.BlockSpec((tm, ATTN_Q_WIDTH), row),
            pl.BlockSpec((tm, RET_V_WIDTH), row),
            pl.BlockSpec((tm, 2 * D_MODEL), row),
            pl.BlockSpec((1, 2 * D_MODEL), const),
            pl.BlockSpec((ATTN_Q_WIDTH, D_MODEL), const),
            pl.BlockSpec((RET_V_WIDTH, D_MODEL), const),
            pl.BlockSpec((D_MODEL, D_MODEL), const),
            pl.BlockSpec((1, D_MODEL), const),
            pl.BlockSpec((1, D_MODEL), const),
        ],
        out_specs=pl.BlockSpec((tm, D_MODEL), row),
        out_shape=jax.ShapeDtypeStruct((t, D_MODEL), F32),
        compiler_params=_params("parallel"),
        name="out_proj",
    )(h, ya, r, gl, b_gate.reshape(1, -1), woa, wor, wout, g.reshape(1, -1), b.reshape(1, -1))


def _ffn_kernel(x_ref, w1_ref, w3_ref, w2_ref, g_ref, b_ref, o_ref):
    x = x_ref[...]
    xb = x.astype(BF16)
    acc = None
    for c in range(D_FF_DENSE // FF_CHUNK):
        cols = slice(c * FF_CHUNK, (c + 1) * FF_CHUNK)
        a = _dot(xb, w1_ref[:, cols])
        hid = a * _sigmoid(a) * _dot(xb, w3_ref[:, cols])
        part = _dot(hid.astype(BF16), w2_ref[cols, :])
        acc = part if acc is None else acc + part
    o_ref[...] = _layer_norm(DEEPNORM_ALPHA * x + acc, g_ref[...], b_ref[...])


def _ffn(x, w1, w3, w2, g, b):
    t = x.shape[0]
    tm = TOKEN_TILE
    row = lambda i: (i, 0)
    const = lambda i: (0, 0)
    return pl.pallas_call(
        _ffn_kernel,
        grid=(t // tm,),
        in_specs=[
            pl.BlockSpec((tm, D_MODEL), row),
            pl.BlockSpec((D_MODEL, D_FF_DENSE), const),
            pl.BlockSpec((D_MODEL, D_FF_DENSE), const),
            pl.BlockSpec((D_FF_DENSE, D_MODEL), const),
            pl.BlockSpec((1, D_MODEL), const),
            pl.BlockSpec((1, D_MODEL), const),
        ],
        out_specs=pl.BlockSpec((tm, D_MODEL), row),
        out_shape=jax.ShapeDtypeStruct((t, D_MODEL), F32),
        compiler_params=_params("parallel"),
        name="ffn",
    )(x, w1, w3, w2, g.reshape(1, -1), b.reshape(1, -1))


def _moe_kernel(x_ref, rw_ref, rb_ref, w1_ref, w3_ref, w2_ref, g_ref, b_ref, o_ref,
                gate_ref, acc_ref):
    e = pl.program_id(1)
    x = x_ref[...]
    lane = lax.broadcasted_iota(jnp.int32, gate_ref.shape, 1)

    @pl.when(e == 0)
    def _():
        xh, xl = _split_bf16(x)
        wh, wl = _split_bf16(rw_ref[...])
        logits = _dot(xh, wh) + _dot(xh, wl) + _dot(xl, wh) + rb_ref[...]
        logits = jnp.where(lane < N_EXPERTS, logits, NEG_INF)
        m1 = jnp.max(logits, axis=1, keepdims=True)
        i1 = jnp.min(jnp.where(logits == m1, lane, gate_ref.shape[1]), axis=1, keepdims=True)
        rest = jnp.where(lane == i1, NEG_INF, logits)
        m2 = jnp.max(rest, axis=1, keepdims=True)
        i2 = jnp.min(jnp.where(rest == m2, lane, gate_ref.shape[1]), axis=1, keepdims=True)
        e2 = jnp.exp(m2 - m1)
        w_top = 1.0 / (1.0 + e2)
        gate_ref[...] = jnp.where(lane == i1, w_top, 0.0) + jnp.where(lane == i2, e2 * w_top, 0.0)
        acc_ref[...] = jnp.zeros_like(acc_ref)

    xb = x.astype(BF16)
    a = _dot(xb, w1_ref[...])
    hid = a * _sigmoid(a) * _dot(xb, w3_ref[...])
    y = _dot(hid.astype(BF16), w2_ref[...])
    g_e = jnp.sum(jnp.where(lane == e, gate_ref[...], 0.0), axis=1, keepdims=True)
    acc_ref[...] += g_e * y

    @pl.when(e == N_EXPERTS - 1)
    def _():
        o_ref[...] = _layer_norm(DEEPNORM_ALPHA * x + acc_ref[...], g_ref[...], b_ref[...])


def _moe(x, router_w, router_b, w1, w3, w2, g, b):
    t = x.shape[0]
    tm = MOE_TOKEN_TILE
    lanes = 128
    rw = jnp.zeros((D_MODEL, lanes), F32).at[:, :N_EXPERTS].set(router_w)
    rb = jnp.zeros((1, lanes), F32).at[0, :N_EXPERTS].set(router_b)
    row = lambda i, e: (i, 0)
    const = lambda i, e: (0, 0)
    return pl.pallas_call(
        _moe_kernel,
        grid=(t // tm, N_EXPERTS),
        in_specs=[
            pl.BlockSpec((tm, D_MODEL), row),
            pl.BlockSpec((D_MODEL, lanes), const),
            pl.BlockSpec((1, lanes), const),
            pl.BlockSpec((None, D_MODEL, D_FF_EXPERT), lambda i, e: (e, 0, 0)),
            pl.BlockSpec((None, D_MODEL, D_FF_EXPERT), lambda i, e: (e, 0, 0)),
            pl.BlockSpec((None, D_FF_EXPERT, D_MODEL), lambda i, e: (e, 0, 0)),
            pl.BlockSpec((1, D_MODEL), const),
            pl.BlockSpec((1, D_MODEL), const),
        ],
        out_specs=pl.BlockSpec((tm, D_MODEL), row),
        out_shape=jax.ShapeDtypeStruct((t, D_MODEL), F32),
        scratch_shapes=[pltpu.VMEM((tm, lanes), F32), pltpu.VMEM((tm, D_MODEL), F32)],
        compiler_params=_params("parallel", "arbitrary"),
        name="moe",
    )(x, rw, rb, w1, w3, w2, g.reshape(1, -1), b.reshape(1, -1))


def _attn_bias_table():
    qi = jnp.arange(BLK)[:, None]
    kj = jnp.arange(3 * BLK)[None, :]
    dist = jnp.abs(qi - kj + BLK)
    slopes = jnp.exp2(-8.0 * jnp.arange(1, N_Q_HEADS + 1, dtype=F32) / N_Q_HEADS)
    bias = -slopes[:, None, None] * dist.astype(F32)[None]
    in_window = (dist <= WINDOW)[None]
    variants = []
    for ok in (kj >= BLK, kj >= 0, kj < 2 * BLK):
        variants.append(jnp.where(in_window & ok[None], bias, NEG_INF))
    return jnp.stack(variants)


def _retention_tables(decay_fwd, decay_bwd):
    pos = jnp.arange(BLK, dtype=F32)
    diff = jnp.arange(BLK)[:, None] - jnp.arange(BLK)[None, :]
    lg_f = jax.nn.log_sigmoid(decay_fwd.astype(F32))
    lg_b = jax.nn.log_sigmoid(decay_bwd.astype(F32))

    def intra(lg, d, mask):
        dec = jnp.where(mask[None], jnp.exp(jnp.maximum(d, 0).astype(F32)[None] * lg[:, None, None]), 0.0)
        return jnp.transpose(dec, (1, 0, 2)).reshape(BLK, N_RET_HEADS * BLK)

    def per_pos(lg, expo, width):
        return jnp.repeat(jnp.exp(expo[:, None] * lg[None, :]), width, axis=1)

    return {
        "intra_f": intra(lg_f, diff, diff >= 0),
        "intra_b": intra(lg_b, -diff, diff < 0),
        "qdec_f": per_pos(lg_f, pos + 1.0, RET_VAL_DIM),
        "qdec_b": per_pos(lg_b, BLK - pos, RET_VAL_DIM),
        "kdec_f": per_pos(lg_f, BLK - 1.0 - pos, RET_KEY_DIM),
        "kdec_b": per_pos(lg_b, pos, RET_KEY_DIM),
        "cdec_f": jnp.repeat(jnp.exp(BLK * lg_f), RET_VAL_DIM)[None, :],
        "cdec_b": jnp.repeat(jnp.exp(BLK * lg_b), RET_VAL_DIM)[None, :],
    }


def _group_avg_matrix():
    head = jnp.arange(RET_V_WIDTH) // RET_VAL_DIM
    return jnp.where(head[:, None] == head[None, :], 1.0 / RET_VAL_DIM, 0.0).astype(BF16)


def kernel(x, ln_emb_g, ln_emb_b, w_in, b_gate, sink_logits, decay_fwd, decay_bwd,
           w_o_attn, w_o_ret, w_out, ln1_g, ln1_b, ffn_w1, ffn_w3, ffn_w2,
           router_w, router_b, moe_w1, moe_w3, moe_w2, ln2_g, ln2_b):
    batch, seq, _ = x.shape
    nb = seq // BLK
    xt = x.reshape(batch * seq, D_MODEL)
    attn_bias = _attn_bias_table()
    group_avg = _group_avg_matrix()
    for layer in range(DEPTH):
        tabs = _retention_tables(decay_fwd[layer], decay_bwd[layer])
        tabs["attn_bias"] = attn_bias
        tabs["group_avg"] = group_avg
        w = w_in[layer].astype(BF16)
        if layer == 0:
            h, qa, kva, ret, gl = _in_proj(xt, w, ln=(ln_emb_g, ln_emb_b))
        else:
            h = xt
            qa, kva, ret, gl = _in_proj(xt, w)
        prev_f, prev_b = _ret_state(ret, tabs, batch, nb)
        ya, r = _mixer(qa, kva, ret, prev_f, prev_b, sink_logits[layer].astype(F32), tabs, batch, nb)
        x1 = _out_proj(h, ya, r, gl, b_gate[layer], w_o_attn[layer].astype(BF16),
                       w_o_ret[layer].astype(BF16), w_out[layer].astype(BF16),
                       ln1_g[layer], ln1_b[layer])
        i = layer // 2
        if layer % 2 == 0:
            xt = _ffn(x1, ffn_w1[i].astype(BF16), ffn_w3[i].astype(BF16), ffn_w2[i].astype(BF16),
                      ln2_g[layer], ln2_b[layer])
        else:
            xt = _moe(x1, router_w[i], router_b[i], moe_w1[i].astype(BF16), moe_w3[i].astype(BF16),
                      moe_w2[i].astype(BF16), ln2_g[layer], ln2_b[layer])
    return xt.reshape(batch, seq, D_MODEL)
```

```python
import functools

import jax
import jax.numpy as jnp
from jax import lax
from jax.experimental import pallas as pl
from jax.experimental.pallas import tpu as pltpu

F32 = jnp.float32
BF16 = jnp.bfloat16

D_MODEL = 1024
DEPTH = 2
N_Q_HEADS = 8
N_KV_HEADS = 2
HEAD_DIM = 64
WINDOW = 128
BLK = 128
ATTN_Q_WIDTH = N_Q_HEADS * HEAD_DIM
ATTN_KV_WIDTH = N_KV_HEADS * HEAD_DIM
N_RET_HEADS = 8
RET_KEY_DIM = 32
RET_VAL_DIM = 64
RET_QK_WIDTH = N_RET_HEADS * RET_KEY_DIM
RET_V_WIDTH = N_RET_HEADS * RET_VAL_DIM
RET_WIDTH = 2 * RET_QK_WIDTH + 3 * RET_V_WIDTH
D_FF_DENSE = 2816
N_EXPERTS = 8
D_FF_EXPERT = 1408
IN_WIDTH = ATTN_Q_WIDTH + 2 * ATTN_KV_WIDTH + RET_WIDTH + 2 * D_MODEL
DEEPNORM_ALPHA = (2 * DEPTH) ** 0.25
LN_EPS = 1e-5
GN_EPS = 1e-5
NEG_INF = -1e30

TOKEN_TILE = 512
MOE_TOKEN_TILE = 1024
MIX_CHUNKS = 2
FF_CHUNK = 256
VMEM_LIMIT_BYTES = 56 * 1024 * 1024


def _params(*sem):
    return pltpu.CompilerParams(dimension_semantics=sem, vmem_limit_bytes=VMEM_LIMIT_BYTES)


def _layer_norm(x, g, b):
    mu = jnp.mean(x, axis=-1, keepdims=True)
    d = x - mu
    var = jnp.mean(d * d, axis=-1, keepdims=True)
    return d * lax.rsqrt(var + LN_EPS) * g + b


def _sigmoid(x):
    return 1.0 / (1.0 + jnp.exp(-x))


def _dot(a, b):
    return jnp.dot(a, b, preferred_element_type=F32)


def _dot_nt(a, b):
    return lax.dot_general(a, b, (((1,), (1,)), ((), ())), preferred_element_type=F32)


def _head_block_mask(rows, cols, row_div, col_div):
    r = lax.broadcasted_iota(jnp.int32, (rows, cols), 0) // row_div
    c = lax.broadcasted_iota(jnp.int32, (rows, cols), 1) // col_div
    return r == c


def _in_proj_kernel(apply_ln, *refs):
    if apply_ln:
        (x_ref, g_ref, b_ref, w_ref, kdf_ref, kdb_ref,
         h_ref, qa_ref, kva_ref, ret_ref, gl_ref, kvf_ref, kvb_ref) = refs
        x = _layer_norm(x_ref[...], g_ref[...], b_ref[...])
        h_ref[...] = x
    else:
        x_ref, w_ref, kdf_ref, kdb_ref, qa_ref, kva_ref, ret_ref, gl_ref, kvf_ref, kvb_ref = refs
        x = x_ref[...]
    xb = x.astype(BF16)

    def proj(lo, hi):
        return _dot(xb, w_ref[:, lo:hi])

    c0 = ATTN_Q_WIDTH
    c1 = c0 + 2 * ATTN_KV_WIDTH
    qa_ref[...] = (proj(0, c0) * (HEAD_DIM ** -0.5)).astype(BF16)
    kva_ref[...] = proj(c0, c1).astype(BF16)
    ret_ref[:, :RET_QK_WIDTH] = proj(c1, c1 + RET_QK_WIDTH).astype(BF16)
    rk = (proj(c1 + RET_QK_WIDTH, c1 + 2 * RET_QK_WIDTH) * (RET_KEY_DIM ** -0.5)).astype(BF16)
    ret_ref[:, RET_QK_WIDTH:2 * RET_QK_WIDTH] = rk
    rest = proj(c1 + 2 * RET_QK_WIDTH, c1 + RET_WIDTH).astype(BF16)
    ret_ref[:, 2 * RET_QK_WIDTH:] = rest
    gl_ref[...] = proj(c1 + RET_WIDTH, IN_WIDTH).astype(BF16)

    mask = _head_block_mask(RET_QK_WIDTH, RET_V_WIDTH, RET_KEY_DIM, RET_VAL_DIM)
    for c in range(x.shape[0] // BLK):
        rows = slice(c * BLK, (c + 1) * BLK)
        k32 = rk[rows].astype(F32)
        v = rest[rows, :RET_V_WIDTH]
        for kd_ref, out_ref in ((kdf_ref, kvf_ref), (kdb_ref, kvb_ref)):
            kz = (k32 * kd_ref[...]).T.astype(BF16)
            full = jnp.where(mask, _dot(kz, v), 0.0)
            comp = full[0:RET_KEY_DIM]
            for h in range(1, N_RET_HEADS):
                comp = comp + full[h * RET_KEY_DIM:(h + 1) * RET_KEY_DIM]
            out_ref[c] = comp


def _in_proj(x, w, tabs, ln=None):
    t = x.shape[0]
    tm = TOKEN_TILE
    row = lambda i: (i, 0)
    const = lambda i: (0, 0)
    in_specs = [pl.BlockSpec((tm, D_MODEL), row)]
    args = [x]
    out_shape = []
    out_specs = []
    if ln is not None:
        in_specs += [pl.BlockSpec((1, D_MODEL), const), pl.BlockSpec((1, D_MODEL), const)]
        args += [ln[0].reshape(1, D_MODEL), ln[1].reshape(1, D_MODEL)]
        out_shape.append(jax.ShapeDtypeStruct((t, D_MODEL), F32))
        out_specs.append(pl.BlockSpec((tm, D_MODEL), row))
    in_specs += [pl.BlockSpec((D_MODEL, IN_WIDTH), const),
                 pl.BlockSpec((BLK, RET_QK_WIDTH), const), pl.BlockSpec((BLK, RET_QK_WIDTH), const)]
    args += [w, tabs["kdec_f"], tabs["kdec_b"]]
    for width in (ATTN_Q_WIDTH, 2 * ATTN_KV_WIDTH, RET_WIDTH, 2 * D_MODEL):
        out_shape.append(jax.ShapeDtypeStruct((t, width), BF16))
        out_specs.append(pl.BlockSpec((tm, width), row))
    for _ in range(2):
        out_shape.append(jax.ShapeDtypeStruct((t // BLK, RET_KEY_DIM, RET_V_WIDTH), F32))
        out_specs.append(pl.BlockSpec((tm // BLK, RET_KEY_DIM, RET_V_WIDTH), lambda i: (i, 0, 0)))
    return pl.pallas_call(
        functools.partial(_in_proj_kernel, ln is not None),
        grid=(t // tm,),
        in_specs=in_specs,
        out_specs=out_specs,
        out_shape=out_shape,
        compiler_params=_params("parallel"),
        name="in_proj",
    )(*args)


def _ret_scan_kernel(nb, kvf_ref, kvb_ref, cdf_ref, cdb_ref, pf_ref, pb_ref):
    zero = jnp.zeros((RET_KEY_DIM, RET_V_WIDTH), F32)

    def fwd(i, s):
        pf_ref[i] = s
        return s * cdf_ref[...] + kvf_ref[i]

    def bwd(i, s):
        n = nb - 1 - i
        pb_ref[n] = s
        return s * cdb_ref[...] + kvb_ref[n]

    lax.fori_loop(0, nb, fwd, zero)
    lax.fori_loop(0, nb, bwd, zero)


def _ret_scan(kvf, kvb, tabs, batch, nb):
    seq = lambda b: (b, 0, 0)
    const = lambda b: (0, 0)
    spec = pl.BlockSpec((nb, RET_KEY_DIM, RET_V_WIDTH), seq)
    shape = jax.ShapeDtypeStruct(kvf.shape, F32)
    return pl.pallas_call(
        functools.partial(_ret_scan_kernel, nb),
        grid=(batch,),
        in_specs=[spec, spec, pl.BlockSpec((1, RET_V_WIDTH), const), pl.BlockSpec((1, RET_V_WIDTH), const)],
        out_specs=[spec, spec],
        out_shape=[shape, shape],
        compiler_params=_params("parallel"),
        name="ret_scan",
    )(kvf, kvb, tabs["cdec_f"], tabs["cdec_b"])


def _split_bf16(y):
    hi = y.astype(BF16)
    lo = (y - hi.astype(F32)).astype(BF16)
    return hi, lo


def _mixer_kernel(nb, sink_ref, qa_ref, kvp_ref, kvc_ref, kvn_ref, ret_ref, pf_ref, pb_ref,
                  bias_ref, df_ref, db_ref, qdf_ref, qdb_ref, gavg_ref, ya_ref, r_ref):
    n = pl.program_id(1)
    q = qa_ref[...]
    kv = jnp.concatenate([kvp_ref[...], kvc_ref[...], kvn_ref[...]], axis=0).astype(F32)
    lo_half = lax.broadcasted_iota(jnp.int32, (kv.shape[0], 2 * HEAD_DIM), 1) < HEAD_DIM

    def lane_variants(a):
        swapped = pltpu.roll(a, HEAD_DIM, 1)
        z = jnp.zeros_like(a)
        return (((jnp.where(lo_half, a, z)).astype(BF16), (jnp.where(lo_half, z, swapped)).astype(BF16)),
                ((jnp.where(lo_half, swapped, z)).astype(BF16), (jnp.where(lo_half, z, a)).astype(BF16)))

    k_var = lane_variants(kv[:, :ATTN_KV_WIDTH])
    v_var = lane_variants(kv[:, ATTN_KV_WIDTH:])
    top_rows = lax.broadcasted_iota(jnp.int32, (2 * BLK, 1), 0) < BLK
    half_heads = N_RET_HEADS // 2
    k_mask = _head_block_mask(N_RET_HEADS * BLK, RET_QK_WIDTH, BLK, RET_KEY_DIM)
    v_mask = _head_block_mask(half_heads * BLK, half_heads * RET_VAL_DIM, BLK, RET_VAL_DIM)
    s_mask = _head_block_mask(RET_QK_WIDTH, RET_V_WIDTH, RET_KEY_DIM, RET_VAL_DIM)
    pair = 2 * HEAD_DIM
    ys = []
    for j in range(MIX_CHUNKS):
        c = n * MIX_CHUNKS + j
        var = jnp.where(c == 0, 0, jnp.where(c == nb - 1, 2, 1))
        tok = slice(j * BLK, (j + 1) * BLK)
        keys = slice(j * BLK, (j + 3) * BLK)

        for g in range(N_KV_HEADS):
            q2 = jnp.concatenate([q[tok, pair * 2 * g:pair * (2 * g + 1)],
                                  q[tok, pair * (2 * g + 1):pair * (2 * g + 2)]], axis=0)
            out = None
            for r in range(2):
                sink = jnp.where(top_rows, sink_ref[4 * g + r], sink_ref[4 * g + 2 + r])
                s = _dot_nt(q2, k_var[g][r][keys]) + bias_ref[var, g, r]
                m = jnp.maximum(jnp.max(s, axis=1, keepdims=True), sink)
                e = jnp.exp(s - m)
                denom = jnp.sum(e, axis=1, keepdims=True) + jnp.exp(sink - m)
                pv = _dot(e.astype(BF16), v_var[g][r][keys]) / denom
                out = pv if out is None else out + pv
            ya_ref[tok, pair * 2 * g:pair * (2 * g + 1)] = out[:BLK].astype(BF16)
            ya_ref[tok, pair * (2 * g + 1):pair * (2 * g + 2)] = out[BLK:].astype(BF16)

        rq = ret_ref[tok, :RET_QK_WIDTH]
        rk = ret_ref[tok, RET_QK_WIDTH:2 * RET_QK_WIDTH]
        rv = ret_ref[tok, 2 * RET_QK_WIDTH:2 * RET_QK_WIDTH + RET_V_WIDTH]
        k_exp = jnp.where(k_mask, jnp.concatenate([rk] * N_RET_HEADS, axis=0), jnp.zeros((), BF16))
        qk = _dot_nt(rq, k_exp)
        p_fb = jnp.concatenate([(qk * df_ref[...]).astype(BF16), (qk * db_ref[...]).astype(BF16)], axis=0)
        intra = []
        for g in range(2):
            cols = slice(g * half_heads * RET_VAL_DIM, (g + 1) * half_heads * RET_VAL_DIM)
            v_bd = jnp.where(v_mask, jnp.concatenate([rv[:, cols]] * half_heads, axis=0), jnp.zeros((), BF16))
            intra.append(_dot(p_fb[:, g * half_heads * BLK:(g + 1) * half_heads * BLK], v_bd))
        cross = []
        for state_ref, qdec_ref in ((pf_ref, qdf_ref), (pb_ref, qdb_ref)):
            s_bd = jnp.where(s_mask, jnp.concatenate([state_ref[j]] * N_RET_HEADS, axis=0), 0.0)
            cross.append(_dot(rq, s_bd.astype(BF16)) * qdec_ref[...])
        ys.append(jnp.concatenate(intra, axis=1) + jnp.concatenate(cross, axis=0))

    y = jnp.concatenate(ys, axis=0)
    d = y - _dot(y.astype(BF16), gavg_ref[...])
    normed = d * lax.rsqrt(_dot((d * d).astype(BF16), gavg_ref[...]) + GN_EPS)
    for j in range(MIX_CHUNKS):
        tok = slice(j * BLK, (j + 1) * BLK)
        g_f = ret_ref[tok, 2 * RET_QK_WIDTH + RET_V_WIDTH:2 * RET_QK_WIDTH + 2 * RET_V_WIDTH].astype(F32)
        g_b = ret_ref[tok, 2 * RET_QK_WIDTH + 2 * RET_V_WIDTH:].astype(F32)
        r = (g_f * _sigmoid(g_f) * normed[2 * j * BLK:(2 * j + 1) * BLK]
             + g_b * _sigmoid(g_b) * normed[(2 * j + 1) * BLK:(2 * j + 2) * BLK])
        r_ref[tok, :] = r.astype(BF16)


def _mixer(qa, kva, ret, prev_f, prev_b, sink, tabs, batch, nb):
    t = qa.shape[0]
    steps = nb // MIX_CHUNKS
    rows = MIX_CHUNKS * BLK
    cur = lambda b, n: (b * steps + n, 0)
    prv = lambda b, n: (b * nb + jnp.maximum(n * MIX_CHUNKS - 1, 0), 0)
    nxt = lambda b, n: (b * nb + jnp.minimum((n + 1) * MIX_CHUNKS, nb - 1), 0)
    c2 = lambda b, n: (0, 0)
    state_spec = pl.BlockSpec((MIX_CHUNKS, RET_KEY_DIM, RET_V_WIDTH), lambda b, n: (b * steps + n, 0, 0))
    return pl.pallas_call(
        functools.partial(_mixer_kernel, nb),
        grid=(batch, steps),
        in_specs=[
            pl.BlockSpec(memory_space=pltpu.SMEM),
            pl.BlockSpec((rows, ATTN_Q_WIDTH), cur),
            pl.BlockSpec((BLK, 2 * ATTN_KV_WIDTH), prv),
            pl.BlockSpec((rows, 2 * ATTN_KV_WIDTH), cur),
            pl.BlockSpec((BLK, 2 * ATTN_KV_WIDTH), nxt),
            pl.BlockSpec((rows, RET_WIDTH), cur),
            state_spec,
            state_spec,
            pl.BlockSpec((3, N_KV_HEADS, 2, 2 * BLK, 3 * BLK), lambda b, n: (0, 0, 0, 0, 0)),
            pl.BlockSpec((BLK, N_RET_HEADS * BLK), c2),
            pl.BlockSpec((BLK, N_RET_HEADS * BLK), c2),
            pl.BlockSpec((BLK, RET_V_WIDTH), c2),
            pl.BlockSpec((BLK, RET_V_WIDTH), c2),
            pl.BlockSpec((RET_V_WIDTH, RET_V_WIDTH), c2),
        ],
        out_specs=[pl.BlockSpec((rows, ATTN_Q_WIDTH), cur), pl.BlockSpec((rows, RET_V_WIDTH), cur)],
        out_shape=[jax.ShapeDtypeStruct((t, ATTN_Q_WIDTH), BF16),
                   jax.ShapeDtypeStruct((t, RET_V_WIDTH), BF16)],
        compiler_params=_params("parallel", "parallel"),
        name="mixer",
    )(sink, qa, kva, kva, kva, ret, prev_f, prev_b, tabs["attn_bias"], tabs["intra_f"],
      tabs["intra_b"], tabs["qdec_f"], tabs["qdec_b"], tabs["group_avg"])


def _out_proj_kernel(h_ref, ya_ref, r_ref, gl_ref, bg_ref, woa_ref, wor_ref, wout_ref,
                     g_ref, b_ref, x_ref):
    gates = _sigmoid(gl_ref[...].astype(F32) + bg_ref[...])
    merged = (gates[:, :D_MODEL] * _dot(ya_ref[...], woa_ref[...])
              + gates[:, D_MODEL:] * _dot(r_ref[...], wor_ref[...]))
    mix = _dot(merged.astype(BF16), wout_ref[...])
    x_ref[...] = _layer_norm(DEEPNORM_ALPHA * h_ref[...] + mix, g_ref[...], b_ref[...])


def _out_proj(h, ya, r, gl, b_gate, woa, wor, wout, g, b):
    t = h.shape[0]
    tm = TOKEN_TILE
    row = lambda i: (i, 0)
    const = lambda i: (0, 0)
    return pl.pallas_call(
        _out_proj_kernel,
        grid=(t // tm,),
        in_specs=[
            pl.BlockSpec((tm, D_MODEL), row),
            pl.BlockSpec((tm, ATTN_Q_WIDTH), row),
            pl.BlockSpec((tm, RET_V_WIDTH), row),
            pl.BlockSpec((tm, 2 * D_MODEL), row),
            pl.BlockSpec((1, 2 * D_MODEL), const),
            pl.BlockSpec((ATTN_Q_WIDTH, D_MODEL), const),
            pl.BlockSpec((RET_V_WIDTH, D_MODEL), const),
            pl.BlockSpec((D_MODEL, D_MODEL), const),
            pl.BlockSpec((1, D_MODEL), const),
            pl.BlockSpec((1, D_MODEL), const),
        ],
        out_specs=pl.BlockSpec((tm, D_MODEL), row),
        out_shape=jax.ShapeDtypeStruct((t, D_MODEL), F32),
        compiler_params=_params("parallel"),
        name="out_proj",
    )(h, ya, r, gl, b_gate.reshape(1, -1), woa, wor, wout, g.reshape(1, -1), b.reshape(1, -1))


def _ffn_kernel(x_ref, w1_ref, w3_ref, w2_ref, g_ref, b_ref, o_ref):
    x = x_ref[...]
    xb = x.astype(BF16)
    acc = None
    for c in range(D_FF_DENSE // FF_CHUNK):
        cols = slice(c * FF_CHUNK, (c + 1) * FF_CHUNK)
        a = _dot(xb, w1_ref[:, cols])
        hid = a * _sigmoid(a) * _dot(xb, w3_ref[:, cols])
        part = _dot(hid.astype(BF16), w2_ref[cols, :])
        acc = part if acc is None else acc + part
    o_ref[...] = _layer_norm(DEEPNORM_ALPHA * x + acc, g_ref[...], b_ref[...])


def _ffn(x, w1, w3, w2, g, b):
    t = x.shape[0]
    tm = TOKEN_TILE
    row = lambda i: (i, 0)
    const = lambda i: (0, 0)
    return pl.pallas_call(
        _ffn_kernel,
        grid=(t // tm,),
        in_specs=[
            pl.BlockSpec((tm, D_MODEL), row),
            pl.BlockSpec((D_MODEL, D_FF_DENSE), const),
            pl.BlockSpec((D_MODEL, D_FF_DENSE), const),
            pl.BlockSpec((D_FF_DENSE, D_MODEL), const),
            pl.BlockSpec((1, D_MODEL), const),
            pl.BlockSpec((1, D_MODEL), const),
        ],
        out_specs=pl.BlockSpec((tm, D_MODEL), row),
        out_shape=jax.ShapeDtypeStruct((t, D_MODEL), F32),
        compiler_params=_params("parallel"),
        name="ffn",
    )(x, w1, w3, w2, g.reshape(1, -1), b.reshape(1, -1))


def _moe_kernel(x_ref, rw_ref, rb_ref, w1_ref, w3_ref, w2_ref, g_ref, b_ref, o_ref,
                gate_ref, acc_ref):
    e = pl.program_id(1)
    x = x_ref[...]
    lane = lax.broadcasted_iota(jnp.int32, gate_ref.shape, 1)

    @pl.when(e == 0)
    def _():
        xh, xl = _split_bf16(x)
        wh, wl = _split_bf16(rw_ref[...])
        logits = _dot(xh, wh) + _dot(xh, wl) + _dot(xl, wh) + rb_ref[...]
        logits = jnp.where(lane < N_EXPERTS, logits, NEG_INF)
        m1 = jnp.max(logits, axis=1, keepdims=True)
        i1 = jnp.min(jnp.where(logits == m1, lane, gate_ref.shape[1]), axis=1, keepdims=True)
        rest = jnp.where(lane == i1, NEG_INF, logits)
        m2 = jnp.max(rest, axis=1, keepdims=True)
        i2 = jnp.min(jnp.where(rest == m2, lane, gate_ref.shape[1]), axis=1, keepdims=True)
        e2 = jnp.exp(m2 - m1)
        w_top = 1.0 / (1.0 + e2)
        gate_ref[...] = jnp.where(lane == i1, w_top, 0.0) + jnp.where(lane == i2, e2 * w_top, 0.0)
        acc_ref[...] = jnp.zeros_like(acc_ref)

    xb = x.astype(BF16)
    a = _dot(xb, w1_ref[...])
    hid = a * _sigmoid(a) * _dot(xb, w3_ref[...])
    y = _dot(hid.astype(BF16), w2_ref[...])
    g_e = jnp.sum(jnp.where(lane == e, gate_ref[...], 0.0), axis=1, keepdims=True)
    acc_ref[...] += g_e * y

    @pl.when(e == N_EXPERTS - 1)
    def _():
        o_ref[...] = _layer_norm(DEEPNORM_ALPHA * x + acc_ref[...], g_ref[...], b_ref[...])


def _moe(x, router_w, router_b, w1, w3, w2, g, b):
    t = x.shape[0]
    tm = MOE_TOKEN_TILE
    lanes = 128
    rw = jnp.zeros((D_MODEL, lanes), F32).at[:, :N_EXPERTS].set(router_w)
    rb = jnp.zeros((1, lanes), F32).at[0, :N_EXPERTS].set(router_b)
    row = lambda i, e: (i, 0)
    const = lambda i, e: (0, 0)
    return pl.pallas_call(
        _moe_kernel,
        grid=(t // tm, N_EXPERTS),
        in_specs=[
            pl.BlockSpec((tm, D_MODEL), row),
            pl.BlockSpec((D_MODEL, lanes), const),
            pl.BlockSpec((1, lanes), const),
            pl.BlockSpec((None, D_MODEL, D_FF_EXPERT), lambda i, e: (e, 0, 0)),
            pl.BlockSpec((None, D_MODEL, D_FF_EXPERT), lambda i, e: (e, 0, 0)),
            pl.BlockSpec((None, D_FF_EXPERT, D_MODEL), lambda i, e: (e, 0, 0)),
            pl.BlockSpec((1, D_MODEL), const),
            pl.BlockSpec((1, D_MODEL), const),
        ],
        out_specs=pl.BlockSpec((tm, D_MODEL), row),
        out_shape=jax.ShapeDtypeStruct((t, D_MODEL), F32),
        scratch_shapes=[pltpu.VMEM((tm, lanes), F32), pltpu.VMEM((tm, D_MODEL), F32)],
        compiler_params=_params("parallel", "arbitrary"),
        name="moe",
    )(x, rw, rb, w1, w3, w2, g.reshape(1, -1), b.reshape(1, -1))


def _attn_bias_table():
    qi = jnp.arange(BLK)[:, None]
    kj = jnp.arange(3 * BLK)[None, :]
    dist = jnp.abs(qi - kj + BLK)
    slopes = jnp.exp2(-8.0 * jnp.arange(1, N_Q_HEADS + 1, dtype=F32) / N_Q_HEADS)
    bias = -slopes[:, None, None] * dist.astype(F32)[None]
    in_window = (dist <= WINDOW)[None]
    variants = []
    for ok in (kj >= BLK, kj >= 0, kj < 2 * BLK):
        per_head = jnp.where(in_window & ok[None], bias, NEG_INF)
        per_head = per_head.reshape(N_KV_HEADS, 2, 2, BLK, 3 * BLK)
        variants.append(jnp.transpose(per_head, (0, 2, 1, 3, 4)).reshape(N_KV_HEADS, 2, 2 * BLK, 3 * BLK))
    return jnp.stack(variants)


def _retention_tables(decay_fwd, decay_bwd):
    pos = jnp.arange(BLK, dtype=F32)
    diff = jnp.arange(BLK)[:, None] - jnp.arange(BLK)[None, :]
    lg_f = jax.nn.log_sigmoid(decay_fwd.astype(F32))
    lg_b = jax.nn.log_sigmoid(decay_bwd.astype(F32))

    def intra(lg, d, mask):
        dec = jnp.where(mask[None], jnp.exp(jnp.maximum(d, 0).astype(F32)[None] * lg[:, None, None]), 0.0)
        return jnp.transpose(dec, (1, 0, 2)).reshape(BLK, N_RET_HEADS * BLK)

    def per_pos(lg, expo, width):
        return jnp.repeat(jnp.exp(expo[:, None] * lg[None, :]), width, axis=1)

    return {
        "intra_f": intra(lg_f, diff, diff >= 0),
        "intra_b": intra(lg_b, -diff, diff < 0),
        "qdec_f": per_pos(lg_f, pos + 1.0, RET_VAL_DIM),
        "qdec_b": per_pos(lg_b, BLK - pos, RET_VAL_DIM),
        "kdec_f": per_pos(lg_f, BLK - 1.0 - pos, RET_KEY_DIM),
        "kdec_b": per_pos(lg_b, pos, RET_KEY_DIM),
        "cdec_f": jnp.repeat(jnp.exp(BLK * lg_f), RET_VAL_DIM)[None, :],
        "cdec_b": jnp.repeat(jnp.exp(BLK * lg_b), RET_VAL_DIM)[None, :],
    }


def _group_avg_matrix():
    head = jnp.arange(RET_V_WIDTH) // RET_VAL_DIM
    return jnp.where(head[:, None] == head[None, :], 1.0 / RET_VAL_DIM, 0.0).astype(BF16)


def kernel(x, ln_emb_g, ln_emb_b, w_in, b_gate, sink_logits, decay_fwd, decay_bwd,
           w_o_attn, w_o_ret, w_out, ln1_g, ln1_b, ffn_w1, ffn_w3, ffn_w2,
           router_w, router_b, moe_w1, moe_w3, moe_w2, ln2_g, ln2_b):
    batch, seq, _ = x.shape
    nb = seq // BLK
    xt = x.reshape(batch * seq, D_MODEL)
    attn_bias = _attn_bias_table()
    group_avg = _group_avg_matrix()
    for layer in range(DEPTH):
        tabs = _retention_tables(decay_fwd[layer], decay_bwd[layer])
        tabs["attn_bias"] = attn_bias
        tabs["group_avg"] = group_avg
        w = w_in[layer].astype(BF16)
        if layer == 0:
            h, qa, kva, ret, gl, kvf, kvb = _in_proj(xt, w, tabs, ln=(ln_emb_g, ln_emb_b))
        else:
            h = xt
            qa, kva, ret, gl, kvf, kvb = _in_proj(xt, w, tabs)
        prev_f, prev_b = _ret_scan(kvf, kvb, tabs, batch, nb)
        ya, r = _mixer(qa, kva, ret, prev_f, prev_b, sink_logits[layer].astype(F32), tabs, batch, nb)
        x1 = _out_proj(h, ya, r, gl, b_gate[layer], w_o_attn[layer].astype(BF16),
                       w_o_ret[layer].astype(BF16), w_out[layer].astype(BF16),
                       ln1_g[layer], ln1_b[layer])
        i = layer // 2
        if layer % 2 == 0:
            xt = _ffn(x1, ffn_w1[i].astype(BF16), ffn_w3[i].astype(BF16), ffn_w2[i].astype(BF16),
                      ln2_g[layer], ln2_b[layer])
        else:
            xt = _moe(x1, router_w[i], router_b[i], moe_w1[i].astype(BF16), moe_w3[i].astype(BF16),
                      moe_w2[i].astype(BF16), ln2_g[layer], ln2_b[layer])
    return xt.reshape(batch, seq, D_MODEL)
```

```python
import functools

import jax
import jax.numpy as jnp
from jax import lax
from jax.experimental import pallas as pl
from jax.experimental.pallas import tpu as pltpu

F32 = jnp.float32
BF16 = jnp.bfloat16

D_MODEL = 1024
DEPTH = 2
N_Q_HEADS = 8
N_KV_HEADS = 2
HEAD_DIM = 64
WINDOW = 128
BLK = 128
ATTN_Q_WIDTH = N_Q_HEADS * HEAD_DIM
ATTN_KV_WIDTH = N_KV_HEADS * HEAD_DIM
N_RET_HEADS = 8
RET_KEY_DIM = 32
RET_VAL_DIM = 64
RET_QK_WIDTH = N_RET_HEADS * RET_KEY_DIM
RET_V_WIDTH = N_RET_HEADS * RET_VAL_DIM
RET_WIDTH = 2 * RET_QK_WIDTH + 3 * RET_V_WIDTH
D_FF_DENSE = 2816
N_EXPERTS = 8
D_FF_EXPERT = 1408
IN_WIDTH = ATTN_Q_WIDTH + 2 * ATTN_KV_WIDTH + RET_WIDTH + 2 * D_MODEL
DEEPNORM_ALPHA = (2 * DEPTH) ** 0.25
LN_EPS = 1e-5
GN_EPS = 1e-5
NEG_INF = -1e30

TOKEN_TILE = 512
MOE_TOKEN_TILE = 1024
MOE_SLAB = 288
MOE_SLAB_K = 384
LANES = 128
SLOT_RADIX = 32.0
MIX_CHUNKS = 2
FF_CHUNK = 256
VMEM_LIMIT_BYTES = 56 * 1024 * 1024


def _params(*sem):
    return pltpu.CompilerParams(dimension_semantics=sem, vmem_limit_bytes=VMEM_LIMIT_BYTES)


def _layer_norm(x, g, b):
    mu = jnp.mean(x, axis=-1, keepdims=True)
    d = x - mu
    var = jnp.mean(d * d, axis=-1, keepdims=True)
    return d * lax.rsqrt(var + LN_EPS) * g + b


def _sigmoid(x):
    return 1.0 / (1.0 + jnp.exp(-x))


def _dot(a, b):
    return jnp.dot(a, b, preferred_element_type=F32)


def _dot_nt(a, b):
    return lax.dot_general(a, b, (((1,), (1,)), ((), ())), preferred_element_type=F32)


def _head_block_mask(rows, cols, row_div, col_div):
    r = lax.broadcasted_iota(jnp.int32, (rows, cols), 0) // row_div
    c = lax.broadcasted_iota(jnp.int32, (rows, cols), 1) // col_div
    return r == c


def _in_proj_kernel(apply_ln, *refs):
    if apply_ln:
        (x_ref, g_ref, b_ref, w_ref, kdf_ref, kdb_ref,
         h_ref, qa_ref, kva_ref, ret_ref, gl_ref, kvf_ref, kvb_ref) = refs
        x = _layer_norm(x_ref[...], g_ref[...], b_ref[...])
        h_ref[...] = x
    else:
        x_ref, w_ref, kdf_ref, kdb_ref, qa_ref, kva_ref, ret_ref, gl_ref, kvf_ref, kvb_ref = refs
        x = x_ref[...]
    xb = x.astype(BF16)

    def proj(lo, hi):
        return _dot(xb, w_ref[:, lo:hi])

    c0 = ATTN_Q_WIDTH
    c1 = c0 + 2 * ATTN_KV_WIDTH
    qa_ref[...] = (proj(0, c0) * (HEAD_DIM ** -0.5)).astype(BF16)
    kva_ref[...] = proj(c0, c1).astype(BF16)
    ret_ref[:, :RET_QK_WIDTH] = proj(c1, c1 + RET_QK_WIDTH).astype(BF16)
    rk = (proj(c1 + RET_QK_WIDTH, c1 + 2 * RET_QK_WIDTH) * (RET_KEY_DIM ** -0.5)).astype(BF16)
    ret_ref[:, RET_QK_WIDTH:2 * RET_QK_WIDTH] = rk
    rest = proj(c1 + 2 * RET_QK_WIDTH, c1 + RET_WIDTH).astype(BF16)
    ret_ref[:, 2 * RET_QK_WIDTH:] = rest
    gl_ref[...] = proj(c1 + RET_WIDTH, IN_WIDTH).astype(BF16)

    mask = _head_block_mask(RET_QK_WIDTH, RET_V_WIDTH, RET_KEY_DIM, RET_VAL_DIM)
    for c in range(x.shape[0] // BLK):
        rows = slice(c * BLK, (c + 1) * BLK)
        k32 = rk[rows].astype(F32)
        v = rest[rows, :RET_V_WIDTH]
        for kd_ref, out_ref in ((kdf_ref, kvf_ref), (kdb_ref, kvb_ref)):
            kz = (k32 * kd_ref[...]).T.astype(BF16)
            full = jnp.where(mask, _dot(kz, v), 0.0)
            comp = full[0:RET_KEY_DIM]
            for h in range(1, N_RET_HEADS):
                comp = comp + full[h * RET_KEY_DIM:(h + 1) * RET_KEY_DIM]
            out_ref[c] = comp


def _in_proj(x, w, tabs, ln=None):
    t = x.shape[0]
    tm = TOKEN_TILE
    row = lambda i: (i, 0)
    const = lambda i: (0, 0)
    in_specs = [pl.BlockSpec((tm, D_MODEL), row)]
    args = [x]
    out_shape = []
    out_specs = []
    if ln is not None:
        in_specs += [pl.BlockSpec((1, D_MODEL), const), pl.BlockSpec((1, D_MODEL), const)]
        args += [ln[0].reshape(1, D_MODEL), ln[1].reshape(1, D_MODEL)]
        out_shape.append(jax.ShapeDtypeStruct((t, D_MODEL), F32))
        out_specs.append(pl.BlockSpec((tm, D_MODEL), row))
    in_specs += [pl.BlockSpec((D_MODEL, IN_WIDTH), const),
                 pl.BlockSpec((BLK, RET_QK_WIDTH), const), pl.BlockSpec((BLK, RET_QK_WIDTH), const)]
    args += [w, tabs["kdec_f"], tabs["kdec_b"]]
    for width in (ATTN_Q_WIDTH, 2 * ATTN_KV_WIDTH, RET_WIDTH, 2 * D_MODEL):
        out_shape.append(jax.ShapeDtypeStruct((t, width), BF16))
        out_specs.append(pl.BlockSpec((tm, width), row))
    for _ in range(2):
        out_shape.append(jax.ShapeDtypeStruct((t // BLK, RET_KEY_DIM, RET_V_WIDTH), F32))
        out_specs.append(pl.BlockSpec((tm // BLK, RET_KEY_DIM, RET_V_WIDTH), lambda i: (i, 0, 0)))
    return pl.pallas_call(
        functools.partial(_in_proj_kernel, ln is not None),
        grid=(t // tm,),
        in_specs=in_specs,
        out_specs=out_specs,
        out_shape=out_shape,
        compiler_params=_params("parallel"),
        name="in_proj",
    )(*args)


def _ret_scan_kernel(nb, kvf_ref, kvb_ref, cdf_ref, cdb_ref, pf_ref, pb_ref):
    zero = jnp.zeros((RET_KEY_DIM, RET_V_WIDTH), F32)

    def fwd(i, s):
        pf_ref[i] = s
        return s * cdf_ref[...] + kvf_ref[i]

    def bwd(i, s):
        n = nb - 1 - i
        pb_ref[n] = s
        return s * cdb_ref[...] + kvb_ref[n]

    lax.fori_loop(0, nb, fwd, zero)
    lax.fori_loop(0, nb, bwd, zero)


def _ret_scan(kvf, kvb, tabs, batch, nb):
    seq = lambda b: (b, 0, 0)
    const = lambda b: (0, 0)
    spec = pl.BlockSpec((nb, RET_KEY_DIM, RET_V_WIDTH), seq)
    shape = jax.ShapeDtypeStruct(kvf.shape, F32)
    return pl.pallas_call(
        functools.partial(_ret_scan_kernel, nb),
        grid=(batch,),
        in_specs=[spec, spec, pl.BlockSpec((1, RET_V_WIDTH), const), pl.BlockSpec((1, RET_V_WIDTH), const)],
        out_specs=[spec, spec],
        out_shape=[shape, shape],
        compiler_params=_params("parallel"),
        name="ret_scan",
    )(kvf, kvb, tabs["cdec_f"], tabs["cdec_b"])


def _split_bf16(y):
    hi = y.astype(BF16)
    lo = (y - hi.astype(F32)).astype(BF16)
    return hi, lo


def _mixer_kernel(nb, sink_ref, qa_ref, kvp_ref, kvc_ref, kvn_ref, ret_ref, pf_ref, pb_ref,
                  bias_ref, df_ref, db_ref, qdf_ref, qdb_ref, gavg_ref, ya_ref, r_ref):
    n = pl.program_id(1)
    q = qa_ref[...]
    kv = jnp.concatenate([kvp_ref[...], kvc_ref[...], kvn_ref[...]], axis=0).astype(F32)
    lo_half = lax.broadcasted_iota(jnp.int32, (kv.shape[0], 2 * HEAD_DIM), 1) < HEAD_DIM

    def lane_variants(a):
        swapped = pltpu.roll(a, HEAD_DIM, 1)
        z = jnp.zeros_like(a)
        return (((jnp.where(lo_half, a, z)).astype(BF16), (jnp.where(lo_half, z, swapped)).astype(BF16)),
                ((jnp.where(lo_half, swapped, z)).astype(BF16), (jnp.where(lo_half, z, a)).astype(BF16)))

    k_var = lane_variants(kv[:, :ATTN_KV_WIDTH])
    v_var = lane_variants(kv[:, ATTN_KV_WIDTH:])
    top_rows = lax.broadcasted_iota(jnp.int32, (2 * BLK, 1), 0) < BLK
    half_heads = N_RET_HEADS // 2
    k_mask = _head_block_mask(N_RET_HEADS * BLK, RET_QK_WIDTH, BLK, RET_KEY_DIM)
    v_mask = _head_block_mask(half_heads * BLK, half_heads * RET_VAL_DIM, BLK, RET_VAL_DIM)
    s_mask = _head_block_mask(RET_QK_WIDTH, RET_V_WIDTH, RET_KEY_DIM, RET_VAL_DIM)
    pair = 2 * HEAD_DIM
    ys = []
    for j in range(MIX_CHUNKS):
        c = n * MIX_CHUNKS + j
        var = jnp.where(c == 0, 0, jnp.where(c == nb - 1, 2, 1))
        tok = slice(j * BLK, (j + 1) * BLK)
        keys = slice(j * BLK, (j + 3) * BLK)

        for g in range(N_KV_HEADS):
            q2 = jnp.concatenate([q[tok, pair * 2 * g:pair * (2 * g + 1)],
                                  q[tok, pair * (2 * g + 1):pair * (2 * g + 2)]], axis=0)
            out = None
            for r in range(2):
                sink = jnp.where(top_rows, sink_ref[4 * g + r], sink_ref[4 * g + 2 + r])
                s = _dot_nt(q2, k_var[g][r][keys]) + bias_ref[var, g, r]
                m = jnp.maximum(jnp.max(s, axis=1, keepdims=True), sink)
                e = jnp.exp(s - m)
                denom = jnp.sum(e, axis=1, keepdims=True) + jnp.exp(sink - m)
                pv = _dot(e.astype(BF16), v_var[g][r][keys]) / denom
                out = pv if out is None else out + pv
            ya_ref[tok, pair * 2 * g:pair * (2 * g + 1)] = out[:BLK].astype(BF16)
            ya_ref[tok, pair * (2 * g + 1):pair * (2 * g + 2)] = out[BLK:].astype(BF16)

        rq = ret_ref[tok, :RET_QK_WIDTH]
        rk = ret_ref[tok, RET_QK_WIDTH:2 * RET_QK_WIDTH]
        rv = ret_ref[tok, 2 * RET_QK_WIDTH:2 * RET_QK_WIDTH + RET_V_WIDTH]
        k_exp = jnp.where(k_mask, jnp.concatenate([rk] * N_RET_HEADS, axis=0), jnp.zeros((), BF16))
        qk = _dot_nt(rq, k_exp)
        p_fb = jnp.concatenate([(qk * df_ref[...]).astype(BF16), (qk * db_ref[...]).astype(BF16)], axis=0)
        intra = []
        for g in range(2):
            cols = slice(g * half_heads * RET_VAL_DIM, (g + 1) * half_heads * RET_VAL_DIM)
            v_bd = jnp.where(v_mask, jnp.concatenate([rv[:, cols]] * half_heads, axis=0), jnp.zeros((), BF16))
            intra.append(_dot(p_fb[:, g * half_heads * BLK:(g + 1) * half_heads * BLK], v_bd))
        cross = []
        for state_ref, qdec_ref in ((pf_ref, qdf_ref), (pb_ref, qdb_ref)):
            s_bd = jnp.where(s_mask, jnp.concatenate([state_ref[j]] * N_RET_HEADS, axis=0), 0.0)
            cross.append(_dot(rq, s_bd.astype(BF16)) * qdec_ref[...])
        ys.append(jnp.concatenate(intra, axis=1) + jnp.concatenate(cross, axis=0))

    y = jnp.concatenate(ys, axis=0)
    d = y - _dot(y.astype(BF16), gavg_ref[...])
    normed = d * lax.rsqrt(_dot((d * d).astype(BF16), gavg_ref[...]) + GN_EPS)
    for j in range(MIX_CHUNKS):
        tok = slice(j * BLK, (j + 1) * BLK)
        g_f = ret_ref[tok, 2 * RET_QK_WIDTH + RET_V_WIDTH:2 * RET_QK_WIDTH + 2 * RET_V_WIDTH].astype(F32)
        g_b = ret_ref[tok, 2 * RET_QK_WIDTH + 2 * RET_V_WIDTH:].astype(F32)
        r = (g_f * _sigmoid(g_f) * normed[2 * j * BLK:(2 * j + 1) * BLK]
             + g_b * _sigmoid(g_b) * normed[(2 * j + 1) * BLK:(2 * j + 2) * BLK])
        r_ref[tok, :] = r.astype(BF16)


def _mixer(qa, kva, ret, prev_f, prev_b, sink, tabs, batch, nb):
    t = qa.shape[0]
    steps = nb // MIX_CHUNKS
    rows = MIX_CHUNKS * BLK
    cur = lambda b, n: (b * steps + n, 0)
    prv = lambda b, n: (b * nb + jnp.maximum(n * MIX_CHUNKS - 1, 0), 0)
    nxt = lambda b, n: (b * nb + jnp.minimum((n + 1) * MIX_CHUNKS, nb - 1), 0)
    c2 = lambda b, n: (0, 0)
    state_spec = pl.BlockSpec((MIX_CHUNKS, RET_KEY_DIM, RET_V_WIDTH), lambda b, n: (b * steps + n, 0, 0))
    return pl.pallas_call(
        functools.partial(_mixer_kernel, nb),
        grid=(batch, steps),
        in_specs=[
            pl.BlockSpec(memory_space=pltpu.SMEM),
            pl.BlockSpec((rows, ATTN_Q_WIDTH), cur),
            pl.BlockSpec((BLK, 2 * ATTN_KV_WIDTH), prv),
            pl.BlockSpec((rows, 2 * ATTN_KV_WIDTH), cur),
            pl.BlockSpec((BLK, 2 * ATTN_KV_WIDTH), nxt),
            pl.BlockSpec((rows, RET_WIDTH), cur),
            state_spec,
            state_spec,
            pl.BlockSpec((3, N_KV_HEADS, 2, 2 * BLK, 3 * BLK), lambda b, n: (0, 0, 0, 0, 0)),
            pl.BlockSpec((BLK, N_RET_HEADS * BLK), c2),
            pl.BlockSpec((BLK, N_RET_HEADS * BLK), c2),
            pl.BlockSpec((BLK, RET_V_WIDTH), c2),
            pl.BlockSpec((BLK, RET_V_WIDTH), c2),
            pl.BlockSpec((RET_V_WIDTH, RET_V_WIDTH), c2),
        ],
        out_specs=[pl.BlockSpec((rows, ATTN_Q_WIDTH), cur), pl.BlockSpec((rows, RET_V_WIDTH), cur)],
        out_shape=[jax.ShapeDtypeStruct((t, ATTN_Q_WIDTH), BF16),
                   jax.ShapeDtypeStruct((t, RET_V_WIDTH), BF16)],
        compiler_params=_params("parallel", "parallel"),
        name="mixer",
    )(sink, qa, kva, kva, kva, ret, prev_f, prev_b, tabs["attn_bias"], tabs["intra_f"],
      tabs["intra_b"], tabs["qdec_f"], tabs["qdec_b"], tabs["group_avg"])


def _out_proj_kernel(h_ref, ya_ref, r_ref, gl_ref, bg_ref, woa_ref, wor_ref, wout_ref,
                     g_ref, b_ref, x_ref):
    gates = _sigmoid(gl_ref[...].astype(F32) + bg_ref[...])
    merged = (gates[:, :D_MODEL] * _dot(ya_ref[...], woa_ref[...])
              + gates[:, D_MODEL:] * _dot(r_ref[...], wor_ref[...]))
    mix = _dot(merged.astype(BF16), wout_ref[...])
    x_ref[...] = _layer_norm(DEEPNORM_ALPHA * h_ref[...] + mix, g_ref[...], b_ref[...])


def _out_proj(h, ya, r, gl, b_gate, woa, wor, wout, g, b):
    t = h.shape[0]
    tm = TOKEN_TILE
    row = lambda i: (i, 0)
    const = lambda i: (0, 0)
    return pl.pallas_call(
        _out_proj_kernel,
        grid=(t // tm,),
        in_specs=[
            pl.BlockSpec((tm, D_MODEL), row),
            pl.BlockSpec((tm, ATTN_Q_WIDTH), row),
            pl.BlockSpec((tm, RET_V_WIDTH), row),
            pl.BlockSpec((tm, 2 * D_MODEL), row),
            pl.BlockSpec((1, 2 * D_MODEL), const),
            pl.BlockSpec((ATTN_Q_WIDTH, D_MODEL), const),
            pl.BlockSpec((RET_V_WIDTH, D_MODEL), const),
            pl.BlockSpec((D_MODEL, D_MODEL), const),
            pl.BlockSpec((1, D_MODEL), const),
            pl.BlockSpec((1, D_MODEL), const),
        ],
        out_specs=pl.BlockSpec((tm, D_MODEL), row),
        out_shape=jax.ShapeDtypeStruct((t, D_MODEL), F32),
        compiler_params=_params("parallel"),
        name="out_proj",
    )(h, ya, r, gl, b_gate.reshape(1, -1), woa, wor, wout, g.reshape(1, -1), b.reshape(1, -1))


def _ffn_kernel(x_ref, w1_ref, w3_ref, w2_ref, g_ref, b_ref, o_ref):
    x = x_ref[...]
    xb = x.astype(BF16)
    acc = None
    for c in range(D_FF_DENSE // FF_CHUNK):
        cols = slice(c * FF_CHUNK, (c + 1) * FF_CHUNK)
        a = _dot(xb, w1_ref[:, cols])
        hid = a * _sigmoid(a) * _dot(xb, w3_ref[:, cols])
        part = _dot(hid.astype(BF16), w2_ref[cols, :])
        acc = part if acc is None else acc + part
    o_ref[...] = _layer_norm(DEEPNORM_ALPHA * x + acc, g_ref[...], b_ref[...])


def _ffn(x, w1, w3, w2, g, b):
    t = x.shape[0]
    tm = TOKEN_TILE
    row = lambda i: (i, 0)
    const = lambda i: (0, 0)
    return pl.pallas_call(
        _ffn_kernel,
        grid=(t // tm,),
        in_specs=[
            pl.BlockSpec((tm, D_MODEL), row),
            pl.BlockSpec((D_MODEL, D_FF_DENSE), const),
            pl.BlockSpec((D_MODEL, D_FF_DENSE), const),
            pl.BlockSpec((D_FF_DENSE, D_MODEL), const),
            pl.BlockSpec((1, D_MODEL), const),
            pl.BlockSpec((1, D_MODEL), const),
        ],
        out_specs=pl.BlockSpec((tm, D_MODEL), row),
        out_shape=jax.ShapeDtypeStruct((t, D_MODEL), F32),
        compiler_params=_params("parallel"),
        name="ffn",
    )(x, w1, w3, w2, g.reshape(1, -1), b.reshape(1, -1))


def _route_kernel(x_ref, rw_ref, rb_ref, tri_ref, xb_ref, tcol_ref, drow_ref, cnt_ref):
    x = x_ref[...]
    tm, lanes = x.shape[0], LANES
    lane = lax.broadcasted_iota(jnp.int32, (tm, lanes), 1)
    xh, xl = _split_bf16(x)
    wh, wl = _split_bf16(rw_ref[...])
    logits = _dot(xh, wh) + _dot(xh, wl) + _dot(xl, wh) + rb_ref[...]
    logits = jnp.where(lane < N_EXPERTS, logits, NEG_INF)
    m1 = jnp.max(logits, axis=1, keepdims=True)
    i1 = jnp.min(jnp.where(logits == m1, lane, lanes), axis=1, keepdims=True)
    rest = jnp.where(lane == i1, NEG_INF, logits)
    m2 = jnp.max(rest, axis=1, keepdims=True)
    i2 = jnp.min(jnp.where(rest == m2, lane, lanes), axis=1, keepdims=True)
    e2 = jnp.exp(m2 - m1)
    w_top = 1.0 / (1.0 + e2)
    gate = jnp.where(lane == i1, w_top, 0.0) + jnp.where(lane == i2, e2 * w_top, 0.0)
    sel = jnp.where(lane == i1, 1.0, jnp.where(lane == i2, 1.0, 0.0))
    prefix = _dot(tri_ref[...], sel.astype(BF16))
    slot1 = jnp.where(sel > 0.0, prefix, 0.0)
    hi = jnp.floor(slot1 * (1.0 / SLOT_RADIX))
    tcol_ref[:, :lanes] = gate.astype(BF16)
    tcol_ref[:, lanes:2 * lanes] = hi.astype(BF16)
    tcol_ref[:, 2 * lanes:] = (slot1 - SLOT_RADIX * hi).astype(BF16)
    drow_ref[...] = (slot1 - 1.0).T[:N_EXPERTS]
    cnt_ref[...] = prefix[tm - 1:tm].astype(jnp.int32)
    xb_ref[...] = x.astype(BF16)


def _route(x, router_w, router_b):
    t = x.shape[0]
    tm = MOE_TOKEN_TILE
    lanes = 128
    rw = jnp.zeros((D_MODEL, lanes), F32).at[:, :N_EXPERTS].set(router_w)
    rb = jnp.zeros((1, lanes), F32).at[0, :N_EXPERTS].set(router_b)
    tri = (jnp.arange(tm)[:, None] >= jnp.arange(tm)[None, :]).astype(BF16)
    row = lambda i: (i, 0)
    const = lambda i: (0, 0)
    return pl.pallas_call(
        _route_kernel,
        grid=(t // tm,),
        in_specs=[
            pl.BlockSpec((tm, D_MODEL), row),
            pl.BlockSpec((D_MODEL, lanes), const),
            pl.BlockSpec((1, lanes), const),
            pl.BlockSpec((tm, tm), const),
        ],
        out_specs=[
            pl.BlockSpec((tm, D_MODEL), row),
            pl.BlockSpec((tm, 3 * lanes), row),
            pl.BlockSpec((N_EXPERTS, tm), row),
            pl.BlockSpec((None, 1, lanes), lambda i: (i, 0, 0)),
        ],
        out_shape=[
            jax.ShapeDtypeStruct((t, D_MODEL), BF16),
            jax.ShapeDtypeStruct((t, 3 * lanes), BF16),
            jax.ShapeDtypeStruct((t // tm * N_EXPERTS, tm), F32),
            jax.ShapeDtypeStruct((t // tm, 1, lanes), jnp.int32),
        ],
        compiler_params=_params("parallel"),
        name="route",
    )(x, rw, rb, tri)


def _moe_kernel(cnt_ref, x_ref, xb_ref, tcol_ref, drow_ref, w1_ref, w3_ref, w2_ref,
                g_ref, b_ref, o_ref, acc_ref):
    i = pl.program_id(0)
    e = pl.program_id(1)

    @pl.when(e == 0)
    def _():
        acc_ref[...] = jnp.zeros_like(acc_ref)

    pick_e = jnp.where(lax.broadcasted_iota(jnp.int32, (LANES, LANES), 0) == e, 1.0, 0.0).astype(BF16)
    gate_b = _dot(tcol_ref[:, :LANES], pick_e)
    slot_b = (SLOT_RADIX * _dot(tcol_ref[:, LANES:2 * LANES], pick_e)
              + _dot(tcol_ref[:, 2 * LANES:], pick_e) - 1.0)
    slot_row = drow_ref[pl.ds(e, 1), :]
    rows = lax.broadcasted_iota(jnp.int32, (MOE_SLAB, 1), 0).astype(F32)
    cols = lax.broadcasted_iota(jnp.int32, (1, LANES), 1).astype(F32)
    n_slabs = (cnt_ref[i * N_EXPERTS + e] + MOE_SLAB - 1) // MOE_SLAB

    def slab(s, carry):
        base = (s * MOE_SLAB).astype(F32)
        pick = jnp.where(slot_row == rows + base, 1.0, 0.0).astype(BF16)
        xs = _dot(pick, xb_ref[...]).astype(BF16)
        a = _dot(xs, w1_ref[...])
        hid = a * _sigmoid(a) * _dot(xs, w3_ref[...])
        y = _dot(hid.astype(BF16), w2_ref[...]).astype(BF16)
        y = jnp.concatenate([y, jnp.zeros((MOE_SLAB_K - MOE_SLAB, D_MODEL), BF16)], axis=0)
        put = jnp.concatenate(
            [jnp.where(slot_b == cols + (base + k * LANES), gate_b, 0.0).astype(BF16)
             for k in range(MOE_SLAB_K // LANES)], axis=1)
        acc_ref[...] += _dot(put, y)
        return carry

    lax.fori_loop(0, n_slabs, slab, 0)

    @pl.when(e == N_EXPERTS - 1)
    def _():
        o_ref[...] = _layer_norm(DEEPNORM_ALPHA * x_ref[...] + acc_ref[...], g_ref[...], b_ref[...])


def _moe(x, router_w, router_b, w1, w3, w2, g, b):
    t = x.shape[0]
    tm = MOE_TOKEN_TILE
    xb, tcol, drow, cnt = _route(x, router_w, router_b)
    counts = cnt[:, 0, :N_EXPERTS].reshape(-1)
    row = lambda i, e, c: (i, 0)
    const = lambda i, e, c: (0, 0)
    expert = lambda i, e, c: (e, 0, 0)
    grid_spec = pltpu.PrefetchScalarGridSpec(
        num_scalar_prefetch=1,
        grid=(t // tm, N_EXPERTS),
        in_specs=[
            pl.BlockSpec((tm, D_MODEL), row),
            pl.BlockSpec((tm, D_MODEL), row),
            pl.BlockSpec((tm, 3 * LANES), row),
            pl.BlockSpec((N_EXPERTS, tm), row),
            pl.BlockSpec((None, D_MODEL, D_FF_EXPERT), expert),
            pl.BlockSpec((None, D_MODEL, D_FF_EXPERT), expert),
            pl.BlockSpec((None, D_FF_EXPERT, D_MODEL), expert),
            pl.BlockSpec((1, D_MODEL), const),
            pl.BlockSpec((1, D_MODEL), const),
        ],
        out_specs=pl.BlockSpec((tm, D_MODEL), row),
        scratch_shapes=[pltpu.VMEM((tm, D_MODEL), F32)],
    )
    return pl.pallas_call(
        _moe_kernel,
        grid_spec=grid_spec,
        out_shape=jax.ShapeDtypeStruct((t, D_MODEL), F32),
        compiler_params=_params("parallel", "arbitrary"),
        name="moe",
    )(counts, x, xb, tcol, drow, w1, w3, w2, g.reshape(1, -1), b.reshape(1, -1))


def _attn_bias_table():
    qi = jnp.arange(BLK)[:, None]
    kj = jnp.arange(3 * BLK)[None, :]
    dist = jnp.abs(qi - kj + BLK)
    slopes = jnp.exp2(-8.0 * jnp.arange(1, N_Q_HEADS + 1, dtype=F32) / N_Q_HEADS)
    bias = -slopes[:, None, None] * dist.astype(F32)[None]
    in_window = (dist <= WINDOW)[None]
    variants = []
    for ok in (kj >= BLK, kj >= 0, kj < 2 * BLK):
        per_head = jnp.where(in_window & ok[None], bias, NEG_INF)
        per_head = per_head.reshape(N_KV_HEADS, 2, 2, BLK, 3 * BLK)
        variants.append(jnp.transpose(per_head, (0, 2, 1, 3, 4)).reshape(N_KV_HEADS, 2, 2 * BLK, 3 * BLK))
    return jnp.stack(variants)


def _retention_tables(decay_fwd, decay_bwd):
    pos = jnp.arange(BLK, dtype=F32)
    diff = jnp.arange(BLK)[:, None] - jnp.arange(BLK)[None, :]
    lg_f = jax.nn.log_sigmoid(decay_fwd.astype(F32))
    lg_b = jax.nn.log_sigmoid(decay_bwd.astype(F32))

    def intra(lg, d, mask):
        dec = jnp.where(mask[None], jnp.exp(jnp.maximum(d, 0).astype(F32)[None] * lg[:, None, None]), 0.0)
        return jnp.transpose(dec, (1, 0, 2)).reshape(BLK, N_RET_HEADS * BLK)

    def per_pos(lg, expo, width):
        return jnp.repeat(jnp.exp(expo[:, None] * lg[None, :]), width, axis=1)

    return {
        "intra_f": intra(lg_f, diff, diff >= 0),
        "intra_b": intra(lg_b, -diff, diff < 0),
        "qdec_f": per_pos(lg_f, pos + 1.0, RET_VAL_DIM),
        "qdec_b": per_pos(lg_b, BLK - pos, RET_VAL_DIM),
        "kdec_f": per_pos(lg_f, BLK - 1.0 - pos, RET_KEY_DIM),
        "kdec_b": per_pos(lg_b, pos, RET_KEY_DIM),
        "cdec_f": jnp.repeat(jnp.exp(BLK * lg_f), RET_VAL_DIM)[None, :],
        "cdec_b": jnp.repeat(jnp.exp(BLK * lg_b), RET_VAL_DIM)[None, :],
    }


def _group_avg_matrix():
    head = jnp.arange(RET_V_WIDTH) // RET_VAL_DIM
    return jnp.where(head[:, None] == head[None, :], 1.0 / RET_VAL_DIM, 0.0).astype(BF16)


def kernel(x, ln_emb_g, ln_emb_b, w_in, b_gate, sink_logits, decay_fwd, decay_bwd,
           w_o_attn, w_o_ret, w_out, ln1_g, ln1_b, ffn_w1, ffn_w3, ffn_w2,
           router_w, router_b, moe_w1, moe_w3, moe_w2, ln2_g, ln2_b):
    batch, seq, _ = x.shape
    nb = seq // BLK
    xt = x.reshape(batch * seq, D_MODEL)
    attn_bias = _attn_bias_table()
    group_avg = _group_avg_matrix()
    for layer in range(DEPTH):
        tabs = _retention_tables(decay_fwd[layer], decay_bwd[layer])
        tabs["attn_bias"] = attn_bias
        tabs["group_avg"] = group_avg
        w = w_in[layer].astype(BF16)
        if layer == 0:
            h, qa, kva, ret, gl, kvf, kvb = _in_proj(xt, w, tabs, ln=(ln_emb_g, ln_emb_b))
        else:
            h = xt
            qa, kva, ret, gl, kvf, kvb = _in_proj(xt, w, tabs)
        prev_f, prev_b = _ret_scan(kvf, kvb, tabs, batch, nb)
        ya, r = _mixer(qa, kva, ret, prev_f, prev_b, sink_logits[layer].astype(F32), tabs, batch, nb)
        x1 = _out_proj(h, ya, r, gl, b_gate[layer], w_o_attn[layer].astype(BF16),
                       w_o_ret[layer].astype(BF16), w_out[layer].astype(BF16),
                       ln1_g[layer], ln1_b[layer])
        i = layer // 2
        if layer % 2 == 0:
            xt = _ffn(x1, ffn_w1[i].astype(BF16), ffn_w3[i].astype(BF16), ffn_w2[i].astype(BF16),
                      ln2_g[layer], ln2_b[layer])
        else:
            xt = _moe(x1, router_w[i], router_b[i], moe_w1[i].astype(BF16), moe_w3[i].astype(BF16),
                      moe_w2[i].astype(BF16), ln2_g[layer], ln2_b[layer])
    return xt.reshape(batch, seq, D_MODEL)
```

```python
import functools

import jax
import jax.numpy as jnp
from jax import lax
from jax.experimental import pallas as pl
from jax.experimental.pallas import tpu as pltpu

F32 = jnp.float32
BF16 = jnp.bfloat16

D_MODEL = 1024
DEPTH = 2
N_Q_HEADS = 8
N_KV_HEADS = 2
HEAD_DIM = 64
WINDOW = 128
BLK = 128
ATTN_Q_WIDTH = N_Q_HEADS * HEAD_DIM
ATTN_KV_WIDTH = N_KV_HEADS * HEAD_DIM
N_RET_HEADS = 8
RET_KEY_DIM = 32
RET_VAL_DIM = 64
RET_QK_WIDTH = N_RET_HEADS * RET_KEY_DIM
RET_V_WIDTH = N_RET_HEADS * RET_VAL_DIM
RET_WIDTH = 2 * RET_QK_WIDTH + 3 * RET_V_WIDTH
D_FF_DENSE = 2816
N_EXPERTS = 8
D_FF_EXPERT = 1408
IN_WIDTH = ATTN_Q_WIDTH + 2 * ATTN_KV_WIDTH + RET_WIDTH + 2 * D_MODEL
DEEPNORM_ALPHA = (2 * DEPTH) ** 0.25
LN_EPS = 1e-5
GN_EPS = 1e-5
NEG_INF = -1e30
LOG2E = 1.4426950408889634

TOKEN_TILE = 512
MOE_TOKEN_TILE = 1024
MOE_SLAB = 288
MOE_SLAB_K = 384
MOE_COMBINE_K = 256
LANES = 128
SLOT_RADIX = 32.0
MIX_CHUNKS = 4
FF_CHUNK = 256
VMEM_LIMIT_BYTES = 56 * 1024 * 1024


def _params(*sem):
    return pltpu.CompilerParams(dimension_semantics=sem, vmem_limit_bytes=VMEM_LIMIT_BYTES)


def _layer_norm(x, g, b):
    mu = jnp.mean(x, axis=-1, keepdims=True)
    d = x - mu
    var = jnp.mean(d * d, axis=-1, keepdims=True)
    return d * lax.rsqrt(var + LN_EPS) * g + b


def _sigmoid(x):
    return 1.0 / (1.0 + jnp.exp(-x))


def _dot(a, b):
    return jnp.dot(a, b, preferred_element_type=F32)


def _dot_nt(a, b):
    return lax.dot_general(a, b, (((1,), (1,)), ((), ())), preferred_element_type=F32)


def _head_block_mask(rows, cols, row_div, col_div):
    r = lax.broadcasted_iota(jnp.int32, (rows, cols), 0) // row_div
    c = lax.broadcasted_iota(jnp.int32, (rows, cols), 1) // col_div
    return r == c


def _in_proj_kernel(apply_ln, *refs):
    if apply_ln:
        (x_ref, g_ref, b_ref, w_ref, kdf_ref, kdb_ref,
         h_ref, qa_ref, kva_ref, ret_ref, gl_ref, kvf_ref, kvb_ref) = refs
        x = _layer_norm(x_ref[...], g_ref[...], b_ref[...])
        h_ref[...] = x
    else:
        x_ref, w_ref, kdf_ref, kdb_ref, qa_ref, kva_ref, ret_ref, gl_ref, kvf_ref, kvb_ref = refs
        x = x_ref[...]
    xb = x.astype(BF16)

    def proj(lo, hi):
        return _dot(xb, w_ref[:, lo:hi])

    c0 = ATTN_Q_WIDTH
    c1 = c0 + 2 * ATTN_KV_WIDTH
    qa_ref[...] = (proj(0, c0) * (HEAD_DIM ** -0.5 * LOG2E)).astype(BF16)
    kva_ref[...] = proj(c0, c1).astype(BF16)
    ret_ref[:, :RET_QK_WIDTH] = proj(c1, c1 + RET_QK_WIDTH).astype(BF16)
    rk = (proj(c1 + RET_QK_WIDTH, c1 + 2 * RET_QK_WIDTH) * (RET_KEY_DIM ** -0.5)).astype(BF16)
    ret_ref[:, RET_QK_WIDTH:2 * RET_QK_WIDTH] = rk
    rest = proj(c1 + 2 * RET_QK_WIDTH, c1 + RET_WIDTH).astype(BF16)
    ret_ref[:, 2 * RET_QK_WIDTH:] = rest
    gl_ref[...] = proj(c1 + RET_WIDTH, IN_WIDTH).astype(BF16)

    mask = _head_block_mask(RET_QK_WIDTH, RET_V_WIDTH, RET_KEY_DIM, RET_VAL_DIM)
    for c in range(x.shape[0] // BLK):
        rows = slice(c * BLK, (c + 1) * BLK)
        k32 = rk[rows].astype(F32)
        v = rest[rows, :RET_V_WIDTH]
        for kd_ref, out_ref in ((kdf_ref, kvf_ref), (kdb_ref, kvb_ref)):
            kz = (k32 * kd_ref[...]).T.astype(BF16)
            full = jnp.where(mask, _dot(kz, v), 0.0)
            comp = full[0:RET_KEY_DIM]
            for h in range(1, N_RET_HEADS):
                comp = comp + full[h * RET_KEY_DIM:(h + 1) * RET_KEY_DIM]
            out_ref[c] = comp


def _in_proj(x, w, tabs, ln=None):
    t = x.shape[0]
    tm = TOKEN_TILE
    row = lambda i: (i, 0)
    const = lambda i: (0, 0)
    in_specs = [pl.BlockSpec((tm, D_MODEL), row)]
    args = [x]
    out_shape = []
    out_specs = []
    if ln is not None:
        in_specs += [pl.BlockSpec((1, D_MODEL), const), pl.BlockSpec((1, D_MODEL), const)]
        args += [ln[0].reshape(1, D_MODEL), ln[1].reshape(1, D_MODEL)]
        out_shape.append(jax.ShapeDtypeStruct((t, D_MODEL), F32))
        out_specs.append(pl.BlockSpec((tm, D_MODEL), row))
    in_specs += [pl.BlockSpec((D_MODEL, IN_WIDTH), const),
                 pl.BlockSpec((BLK, RET_QK_WIDTH), const), pl.BlockSpec((BLK, RET_QK_WIDTH), const)]
    args += [w, tabs["kdec_f"], tabs["kdec_b"]]
    for width in (ATTN_Q_WIDTH, 2 * ATTN_KV_WIDTH, RET_WIDTH, 2 * D_MODEL):
        out_shape.append(jax.ShapeDtypeStruct((t, width), BF16))
        out_specs.append(pl.BlockSpec((tm, width), row))
    for _ in range(2):
        out_shape.append(jax.ShapeDtypeStruct((t // BLK, RET_KEY_DIM, RET_V_WIDTH), F32))
        out_specs.append(pl.BlockSpec((tm // BLK, RET_KEY_DIM, RET_V_WIDTH), lambda i: (i, 0, 0)))
    return pl.pallas_call(
        functools.partial(_in_proj_kernel, ln is not None),
        grid=(t // tm,),
        in_specs=in_specs,
        out_specs=out_specs,
        out_shape=out_shape,
        compiler_params=_params("parallel"),
        name="in_proj",
    )(*args)


def _ret_scan_kernel(nb, kvf_ref, kvb_ref, cdf_ref, cdb_ref, pf_ref, pb_ref):
    zero = jnp.zeros((RET_KEY_DIM, RET_V_WIDTH), F32)

    def fwd(i, s):
        pf_ref[i] = s
        return s * cdf_ref[...] + kvf_ref[i]

    def bwd(i, s):
        n = nb - 1 - i
        pb_ref[n] = s
        return s * cdb_ref[...] + kvb_ref[n]

    lax.fori_loop(0, nb, fwd, zero)
    lax.fori_loop(0, nb, bwd, zero)


def _ret_scan(kvf, kvb, tabs, batch, nb):
    seq = lambda b: (b, 0, 0)
    const = lambda b: (0, 0)
    spec = pl.BlockSpec((nb, RET_KEY_DIM, RET_V_WIDTH), seq)
    shape = jax.ShapeDtypeStruct(kvf.shape, F32)
    return pl.pallas_call(
        functools.partial(_ret_scan_kernel, nb),
        grid=(batch,),
        in_specs=[spec, spec, pl.BlockSpec((1, RET_V_WIDTH), const), pl.BlockSpec((1, RET_V_WIDTH), const)],
        out_specs=[spec, spec],
        out_shape=[shape, shape],
        compiler_params=_params("parallel"),
        name="ret_scan",
    )(kvf, kvb, tabs["cdec_f"], tabs["cdec_b"])


def _split_bf16(y):
    hi = y.astype(BF16)
    lo = (y - hi.astype(F32)).astype(BF16)
    return hi, lo


def _mixer_kernel(nb, sink_ref, qa_ref, kvp_ref, kvc_ref, kvn_ref, ret_ref, pf_ref, pb_ref,
                  bias_ref, df_ref, db_ref, qdf_ref, qdb_ref, gavg_ref, ya_ref, r_ref):
    n = pl.program_id(1)
    q = qa_ref[...]
    kv = jnp.concatenate([kvp_ref[...], kvc_ref[...], kvn_ref[...]], axis=0).astype(F32)
    lo_half = lax.broadcasted_iota(jnp.int32, (kv.shape[0], 2 * HEAD_DIM), 1) < HEAD_DIM

    def lane_variants(a):
        swapped = pltpu.roll(a, HEAD_DIM, 1)
        z = jnp.zeros_like(a)
        return (((jnp.where(lo_half, a, z)).astype(BF16), (jnp.where(lo_half, z, swapped)).astype(BF16)),
                ((jnp.where(lo_half, swapped, z)).astype(BF16), (jnp.where(lo_half, z, a)).astype(BF16)))

    k_var = lane_variants(kv[:, :ATTN_KV_WIDTH])
    v_var = lane_variants(kv[:, ATTN_KV_WIDTH:])
    top_rows = lax.broadcasted_iota(jnp.int32, (2 * BLK, 1), 0) < BLK
    out_lo_half = lax.broadcasted_iota(jnp.int32, (2 * BLK, 2 * HEAD_DIM), 1) < HEAD_DIM
    half_heads = N_RET_HEADS // 2
    k_mask = _head_block_mask(N_RET_HEADS * BLK, RET_QK_WIDTH, BLK, RET_KEY_DIM)
    v_mask = _head_block_mask(half_heads * BLK, half_heads * RET_VAL_DIM, BLK, RET_VAL_DIM)
    s_mask = _head_block_mask(RET_QK_WIDTH, RET_V_WIDTH, RET_KEY_DIM, RET_VAL_DIM)
    pair = 2 * HEAD_DIM
    ys = []
    for j in range(MIX_CHUNKS):
        c = n * MIX_CHUNKS + j
        var = jnp.where(c == 0, 0, jnp.where(c == nb - 1, 2, 1))
        tok = slice(j * BLK, (j + 1) * BLK)
        keys = slice(j * BLK, (j + 3) * BLK)

        for g in range(N_KV_HEADS):
            q2 = jnp.concatenate([q[tok, pair * 2 * g:pair * (2 * g + 1)],
                                  q[tok, pair * (2 * g + 1):pair * (2 * g + 2)]], axis=0)
            s_all = _dot_nt(q2, jnp.concatenate([k_var[g][0][keys], k_var[g][1][keys]], axis=0))
            probs, denoms = [], []
            for r in range(2):
                sink = jnp.where(top_rows, sink_ref[4 * g + r], sink_ref[4 * g + 2 + r])
                s = s_all[:, 3 * BLK * r:3 * BLK * (r + 1)] + bias_ref[var, g, r]
                m = jnp.maximum(jnp.max(s, axis=1, keepdims=True), sink)
                e = jnp.exp2(s - m)
                denoms.append(jnp.sum(e, axis=1, keepdims=True) + jnp.exp2(sink - m))
                probs.append(e.astype(BF16))
            pv = _dot(jnp.concatenate(probs, axis=1),
                      jnp.concatenate([v_var[g][0][keys], v_var[g][1][keys]], axis=0))
            out = pv / jnp.where(out_lo_half, denoms[0], denoms[1])
            ya_ref[tok, pair * 2 * g:pair * (2 * g + 1)] = out[:BLK].astype(BF16)
            ya_ref[tok, pair * (2 * g + 1):pair * (2 * g + 2)] = out[BLK:].astype(BF16)

        rq = ret_ref[tok, :RET_QK_WIDTH]
        rk = ret_ref[tok, RET_QK_WIDTH:2 * RET_QK_WIDTH]
        rv = ret_ref[tok, 2 * RET_QK_WIDTH:2 * RET_QK_WIDTH + RET_V_WIDTH]
        k_exp = jnp.where(k_mask, jnp.concatenate([rk] * N_RET_HEADS, axis=0), jnp.zeros((), BF16))
        qk = _dot_nt(rq, k_exp)
        p_fb = jnp.concatenate([(qk * df_ref[...]).astype(BF16), (qk * db_ref[...]).astype(BF16)], axis=0)
        intra = []
        for g in range(2):
            cols = slice(g * half_heads * RET_VAL_DIM, (g + 1) * half_heads * RET_VAL_DIM)
            v_bd = jnp.where(v_mask, jnp.concatenate([rv[:, cols]] * half_heads, axis=0), jnp.zeros((), BF16))
            intra.append(_dot(p_fb[:, g * half_heads * BLK:(g + 1) * half_heads * BLK], v_bd))
        cross = []
        for state_ref, qdec_ref in ((pf_ref, qdf_ref), (pb_ref, qdb_ref)):
            s_bd = jnp.where(s_mask, jnp.concatenate([state_ref[j]] * N_RET_HEADS, axis=0), 0.0)
            cross.append(_dot(rq, s_bd.astype(BF16)) * qdec_ref[...])
        ys.append(jnp.concatenate(intra, axis=1) + jnp.concatenate(cross, axis=0))

    y = jnp.concatenate(ys, axis=0)
    d = y - _dot(y.astype(BF16), gavg_ref[...])
    normed = d * lax.rsqrt(_dot((d * d).astype(BF16), gavg_ref[...]) + GN_EPS)
    for j in range(MIX_CHUNKS):
        tok = slice(j * BLK, (j + 1) * BLK)
        g_f = ret_ref[tok, 2 * RET_QK_WIDTH + RET_V_WIDTH:2 * RET_QK_WIDTH + 2 * RET_V_WIDTH].astype(F32)
        g_b = ret_ref[tok, 2 * RET_QK_WIDTH + 2 * RET_V_WIDTH:].astype(F32)
        r = (g_f * _sigmoid(g_f) * normed[2 * j * BLK:(2 * j + 1) * BLK]
             + g_b * _sigmoid(g_b) * normed[(2 * j + 1) * BLK:(2 * j + 2) * BLK])
        r_ref[tok, :] = r.astype(BF16)


def _mixer(qa, kva, ret, prev_f, prev_b, sink, tabs, batch, nb):
    t = qa.shape[0]
    steps = nb // MIX_CHUNKS
    rows = MIX_CHUNKS * BLK
    cur = lambda b, n: (b * steps + n, 0)
    prv = lambda b, n: (b * nb + jnp.maximum(n * MIX_CHUNKS - 1, 0), 0)
    nxt = lambda b, n: (b * nb + jnp.minimum((n + 1) * MIX_CHUNKS, nb - 1), 0)
    c2 = lambda b, n: (0, 0)
    state_spec = pl.BlockSpec((MIX_CHUNKS, RET_KEY_DIM, RET_V_WIDTH), lambda b, n: (b * steps + n, 0, 0))
    return pl.pallas_call(
        functools.partial(_mixer_kernel, nb),
        grid=(batch, steps),
        in_specs=[
            pl.BlockSpec(memory_space=pltpu.SMEM),
            pl.BlockSpec((rows, ATTN_Q_WIDTH), cur),
            pl.BlockSpec((BLK, 2 * ATTN_KV_WIDTH), prv),
            pl.BlockSpec((rows, 2 * ATTN_KV_WIDTH), cur),
            pl.BlockSpec((BLK, 2 * ATTN_KV_WIDTH), nxt),
            pl.BlockSpec((rows, RET_WIDTH), cur),
            state_spec,
            state_spec,
            pl.BlockSpec((3, N_KV_HEADS, 2, 2 * BLK, 3 * BLK), lambda b, n: (0, 0, 0, 0, 0)),
            pl.BlockSpec((BLK, N_RET_HEADS * BLK), c2),
            pl.BlockSpec((BLK, N_RET_HEADS * BLK), c2),
            pl.BlockSpec((BLK, RET_V_WIDTH), c2),
            pl.BlockSpec((BLK, RET_V_WIDTH), c2),
            pl.BlockSpec((RET_V_WIDTH, RET_V_WIDTH), c2),
        ],
        out_specs=[pl.BlockSpec((rows, ATTN_Q_WIDTH), cur), pl.BlockSpec((rows, RET_V_WIDTH), cur)],
        out_shape=[jax.ShapeDtypeStruct((t, ATTN_Q_WIDTH), BF16),
                   jax.ShapeDtypeStruct((t, RET_V_WIDTH), BF16)],
        compiler_params=_params("parallel", "parallel"),
        name="mixer",
    )(sink, qa, kva, kva, kva, ret, prev_f, prev_b, tabs["attn_bias"], tabs["intra_f"],
      tabs["intra_b"], tabs["qdec_f"], tabs["qdec_b"], tabs["group_avg"])


def _out_proj_kernel(h_ref, ya_ref, r_ref, gl_ref, bg_ref, woa_ref, wor_ref, wout_ref,
                     g_ref, b_ref, x_ref):
    gates = _sigmoid(gl_ref[...].astype(F32) + bg_ref[...])
    merged = (gates[:, :D_MODEL] * _dot(ya_ref[...], woa_ref[...])
              + gates[:, D_MODEL:] * _dot(r_ref[...], wor_ref[...]))
    mix = _dot(merged.astype(BF16), wout_ref[...])
    x_ref[...] = _layer_norm(DEEPNORM_ALPHA * h_ref[...] + mix, g_ref[...], b_ref[...])


def _out_proj(h, ya, r, gl, b_gate, woa, wor, wout, g, b):
    t = h.shape[0]
    tm = TOKEN_TILE
    row = lambda i: (i, 0)
    const = lambda i: (0, 0)
    return pl.pallas_call(
        _out_proj_kernel,
        grid=(t // tm,),
        in_specs=[
            pl.BlockSpec((tm, D_MODEL), row),
            pl.BlockSpec((tm, ATTN_Q_WIDTH), row),
            pl.BlockSpec((tm, RET_V_WIDTH), row),
            pl.BlockSpec((tm, 2 * D_MODEL), row),
            pl.BlockSpec((1, 2 * D_MODEL), const),
            pl.BlockSpec((ATTN_Q_WIDTH, D_MODEL), const),
            pl.BlockSpec((RET_V_WIDTH, D_MODEL), const),
            pl.BlockSpec((D_MODEL, D_MODEL), const),
            pl.BlockSpec((1, D_MODEL), const),
            pl.BlockSpec((1, D_MODEL), const),
        ],
        out_specs=pl.BlockSpec((tm, D_MODEL), row),
        out_shape=jax.ShapeDtypeStruct((t, D_MODEL), F32),
        compiler_params=_params("parallel"),
        name="out_proj",
    )(h, ya, r, gl, b_gate.reshape(1, -1), woa, wor, wout, g.reshape(1, -1), b.reshape(1, -1))


def _ffn_kernel(x_ref, w1_ref, w3_ref, w2_ref, g_ref, b_ref, o_ref):
    x = x_ref[...]
    xb = x.astype(BF16)
    acc = None
    for c in range(D_FF_DENSE // FF_CHUNK):
        cols = slice(c * FF_CHUNK, (c + 1) * FF_CHUNK)
        a = _dot(xb, w1_ref[:, cols])
        hid = a * _sigmoid(a) * _dot(xb, w3_ref[:, cols])
        part = _dot(hid.astype(BF16), w2_ref[cols, :])
        acc = part if acc is None else acc + part
    o_ref[...] = _layer_norm(DEEPNORM_ALPHA * x + acc, g_ref[...], b_ref[...])


def _ffn(x, w1, w3, w2, g, b):
    t = x.shape[0]
    tm = TOKEN_TILE
    row = lambda i: (i, 0)
    const = lambda i: (0, 0)
    return pl.pallas_call(
        _ffn_kernel,
        grid=(t // tm,),
        in_specs=[
            pl.BlockSpec((tm, D_MODEL), row),
            pl.BlockSpec((D_MODEL, D_FF_DENSE), const),
            pl.BlockSpec((D_MODEL, D_FF_DENSE), const),
            pl.BlockSpec((D_FF_DENSE, D_MODEL), const),
            pl.BlockSpec((1, D_MODEL), const),
            pl.BlockSpec((1, D_MODEL), const),
        ],
        out_specs=pl.BlockSpec((tm, D_MODEL), row),
        out_shape=jax.ShapeDtypeStruct((t, D_MODEL), F32),
        compiler_params=_params("parallel"),
        name="ffn",
    )(x, w1, w3, w2, g.reshape(1, -1), b.reshape(1, -1))


def _route_kernel(x_ref, rw_ref, rb_ref, tri_ref, xb_ref, tcol_ref, pcol_ref, drow_ref, cnt_ref):
    x = x_ref[...]
    tm, lanes = x.shape[0], LANES
    lane = lax.broadcasted_iota(jnp.int32, (tm, lanes), 1)
    xh, xl = _split_bf16(x)
    wh, wl = _split_bf16(rw_ref[...])
    logits = _dot(xh, wh) + _dot(xh, wl) + _dot(xl, wh) + rb_ref[...]
    logits = jnp.where(lane < N_EXPERTS, logits, NEG_INF)
    m1 = jnp.max(logits, axis=1, keepdims=True)
    i1 = jnp.min(jnp.where(logits == m1, lane, lanes), axis=1, keepdims=True)
    rest = jnp.where(lane == i1, NEG_INF, logits)
    m2 = jnp.max(rest, axis=1, keepdims=True)
    i2 = jnp.min(jnp.where(rest == m2, lane, lanes), axis=1, keepdims=True)
    e2 = jnp.exp(m2 - m1)
    w_top = 1.0 / (1.0 + e2)
    gate = jnp.where(lane == i1, w_top, 0.0) + jnp.where(lane == i2, e2 * w_top, 0.0)
    sel = jnp.where(lane == i1, 1.0, jnp.where(lane == i2, 1.0, 0.0))
    prefix = _dot(tri_ref[...], sel.astype(BF16))
    slot1 = jnp.where(sel > 0.0, prefix, 0.0)
    hi = jnp.floor(slot1 * (1.0 / SLOT_RADIX))
    tcol_ref[:, :lanes] = gate.astype(BF16)
    tcol_ref[:, lanes:2 * lanes] = hi.astype(BF16)
    tcol_ref[:, 2 * lanes:] = (slot1 - SLOT_RADIX * hi).astype(BF16)

    def first_slab_pos(idx, picked):
        rank = jnp.sum(jnp.where(picked, prefix, 0.0), axis=1, keepdims=True) - 1.0
        return jnp.where(rank < MOE_SLAB, idx.astype(F32) * MOE_SLAB + rank, -1.0)

    pcol_ref[...] = jnp.where(
        lane == 0, first_slab_pos(i1, lane == i1), jnp.where(
            lane == 1, w_top, jnp.where(
                lane == 2, first_slab_pos(i2, lane == i2), jnp.where(lane == 3, e2 * w_top, 0.0))))
    drow_ref[...] = (slot1 - 1.0).T[:N_EXPERTS]
    cnt_ref[...] = prefix[tm - 1:tm].astype(jnp.int32)
    xb_ref[...] = x.astype(BF16)


def _route(x, router_w, router_b):
    t = x.shape[0]
    tm = MOE_TOKEN_TILE
    lanes = 128
    rw = jnp.zeros((D_MODEL, lanes), F32).at[:, :N_EXPERTS].set(router_w)
    rb = jnp.zeros((1, lanes), F32).at[0, :N_EXPERTS].set(router_b)
    tri = (jnp.arange(tm)[:, None] >= jnp.arange(tm)[None, :]).astype(BF16)
    row = lambda i: (i, 0)
    const = lambda i: (0, 0)
    return pl.pallas_call(
        _route_kernel,
        grid=(t // tm,),
        in_specs=[
            pl.BlockSpec((tm, D_MODEL), row),
            pl.BlockSpec((D_MODEL, lanes), const),
            pl.BlockSpec((1, lanes), const),
            pl.BlockSpec((tm, tm), const),
        ],
        out_specs=[
            pl.BlockSpec((tm, D_MODEL), row),
            pl.BlockSpec((tm, 3 * lanes), row),
            pl.BlockSpec((tm, lanes), row),
            pl.BlockSpec((N_EXPERTS, tm), row),
            pl.BlockSpec((None, 1, lanes), lambda i: (i, 0, 0)),
        ],
        out_shape=[
            jax.ShapeDtypeStruct((t, D_MODEL), BF16),
            jax.ShapeDtypeStruct((t, 3 * lanes), BF16),
            jax.ShapeDtypeStruct((t, lanes), F32),
            jax.ShapeDtypeStruct((t // tm * N_EXPERTS, tm), F32),
            jax.ShapeDtypeStruct((t // tm, 1, lanes), jnp.int32),
        ],
        compiler_params=_params("parallel"),
        name="route",
    )(x, rw, rb, tri)


def _moe_kernel(cnt_ref, x_ref, xb_ref, tcol_ref, pcol_ref, drow_ref, w1_ref, w3_ref, w2_ref,
                g_ref, b_ref, o_ref, y_ref):
    i = pl.program_id(0)
    e = pl.program_id(1)

    @pl.when(e == 0)
    def _():
        o_ref[...] = jnp.zeros_like(o_ref)

    slot_row = drow_ref[pl.ds(e, 1), :]
    rows = lax.broadcasted_iota(jnp.int32, (MOE_SLAB, 1), 0).astype(F32)

    def expert_ffn(base):
        pick = jnp.where(slot_row == rows + base, 1.0, 0.0).astype(BF16)
        xs = _dot(pick, xb_ref[...]).astype(BF16)
        a = _dot(xs, w1_ref[...])
        hid = a * _sigmoid(a) * _dot(xs, w3_ref[...])
        return _dot(hid.astype(BF16), w2_ref[...]).astype(BF16)

    y_ref[pl.ds(pl.multiple_of(e * MOE_SLAB, 16), MOE_SLAB), :] = expert_ffn(0.0)

    def later_slab(s, carry):
        base = (s * MOE_SLAB).astype(F32)
        pick_e = jnp.where(lax.broadcasted_iota(jnp.int32, (LANES, LANES), 0) == e, 1.0, 0.0).astype(BF16)
        gate_b = _dot(tcol_ref[:, :LANES], pick_e)
        slot_b = (SLOT_RADIX * _dot(tcol_ref[:, LANES:2 * LANES], pick_e)
                  + _dot(tcol_ref[:, 2 * LANES:], pick_e) - 1.0)
        cols = lax.broadcasted_iota(jnp.int32, (1, LANES), 1).astype(F32)
        y = jnp.concatenate([expert_ffn(base), jnp.zeros((MOE_SLAB_K - MOE_SLAB, D_MODEL), BF16)], axis=0)
        put = jnp.concatenate(
            [jnp.where(slot_b == cols + (base + k * LANES), gate_b, 0.0).astype(BF16)
             for k in range(MOE_SLAB_K // LANES)], axis=1)
        o_ref[...] += _dot(put, y)
        return carry

    n_slabs = (cnt_ref[i * N_EXPERTS + e] + MOE_SLAB - 1) // MOE_SLAB
    lax.fori_loop(1, n_slabs, later_slab, 0)

    @pl.when(e == N_EXPERTS - 1)
    def _():
        p = pcol_ref[...]
        pos_a, gate_a, pos_b, gate_b = p[:, 0:1], p[:, 1:2], p[:, 2:3], p[:, 3:4]
        acc = DEEPNORM_ALPHA * x_ref[...] + o_ref[...]
        for k in range(N_EXPERTS * MOE_SLAB // MOE_COMBINE_K):
            cols = lax.broadcasted_iota(jnp.int32, (1, MOE_COMBINE_K), 1).astype(F32) + float(k * MOE_COMBINE_K)
            put = jnp.where(pos_a == cols, gate_a, jnp.where(pos_b == cols, gate_b, 0.0)).astype(BF16)
            acc = acc + _dot(put, y_ref[k * MOE_COMBINE_K:(k + 1) * MOE_COMBINE_K, :])
        o_ref[...] = _layer_norm(acc, g_ref[...], b_ref[...])


def _moe(x, router_w, router_b, w1, w3, w2, g, b):
    t = x.shape[0]
    tm = MOE_TOKEN_TILE
    xb, tcol, pcol, drow, cnt = _route(x, router_w, router_b)
    counts = cnt[:, 0, :N_EXPERTS].reshape(-1)
    row = lambda i, e, c: (i, 0)
    const = lambda i, e, c: (0, 0)
    expert = lambda i, e, c: (e, 0, 0)
    grid_spec = pltpu.PrefetchScalarGridSpec(
        num_scalar_prefetch=1,
        grid=(t // tm, N_EXPERTS),
        in_specs=[
            pl.BlockSpec((tm, D_MODEL), row),
            pl.BlockSpec((tm, D_MODEL), row),
            pl.BlockSpec((tm, 3 * LANES), row),
            pl.BlockSpec((tm, LANES), row),
            pl.BlockSpec((N_EXPERTS, tm), row),
            pl.BlockSpec((None, D_MODEL, D_FF_EXPERT), expert),
            pl.BlockSpec((None, D_MODEL, D_FF_EXPERT), expert),
            pl.BlockSpec((None, D_FF_EXPERT, D_MODEL), expert),
            pl.BlockSpec((1, D_MODEL), const),
            pl.BlockSpec((1, D_MODEL), const),
        ],
        out_specs=pl.BlockSpec((tm, D_MODEL), row),
        scratch_shapes=[pltpu.VMEM((N_EXPERTS * MOE_SLAB, D_MODEL), BF16)],
    )
    return pl.pallas_call(
        _moe_kernel,
        grid_spec=grid_spec,
        out_shape=jax.ShapeDtypeStruct((t, D_MODEL), F32),
        compiler_params=_params("parallel", "arbitrary"),
        name="moe",
    )(counts, x, xb, tcol, pcol, drow, w1, w3, w2, g.reshape(1, -1), b.reshape(1, -1))


def _attn_bias_table():
    qi = jnp.arange(BLK)[:, None]
    kj = jnp.arange(3 * BLK)[None, :]
    dist = jnp.abs(qi - kj + BLK)
    slopes = jnp.exp2(-8.0 * jnp.arange(1, N_Q_HEADS + 1, dtype=F32) / N_Q_HEADS)
    bias = -(LOG2E * slopes)[:, None, None] * dist.astype(F32)[None]
    in_window = (dist <= WINDOW)[None]
    variants = []
    for ok in (kj >= BLK, kj >= 0, kj < 2 * BLK):
        per_head = jnp.where(in_window & ok[None], bias, NEG_INF)
        per_head = per_head.reshape(N_KV_HEADS, 2, 2, BLK, 3 * BLK)
        variants.append(jnp.transpose(per_head, (0, 2, 1, 3, 4)).reshape(N_KV_HEADS, 2, 2 * BLK, 3 * BLK))
    return jnp.stack(variants)


def _retention_tables(decay_fwd, decay_bwd):
    pos = jnp.arange(BLK, dtype=F32)
    diff = jnp.arange(BLK)[:, None] - jnp.arange(BLK)[None, :]
    lg_f = jax.nn.log_sigmoid(decay_fwd.astype(F32))
    lg_b = jax.nn.log_sigmoid(decay_bwd.astype(F32))

    def intra(lg, d, mask):
        dec = jnp.where(mask[None], jnp.exp(jnp.maximum(d, 0).astype(F32)[None] * lg[:, None, None]), 0.0)
        return jnp.transpose(dec, (1, 0, 2)).reshape(BLK, N_RET_HEADS * BLK)

    def per_pos(lg, expo, width):
        return jnp.repeat(jnp.exp(expo[:, None] * lg[None, :]), width, axis=1)

    return {
        "intra_f": intra(lg_f, diff, diff >= 0),
        "intra_b": intra(lg_b, -diff, diff < 0),
        "qdec_f": per_pos(lg_f, pos + 1.0, RET_VAL_DIM),
        "qdec_b": per_pos(lg_b, BLK - pos, RET_VAL_DIM),
        "kdec_f": per_pos(lg_f, BLK - 1.0 - pos, RET_KEY_DIM),
        "kdec_b": per_pos(lg_b, pos, RET_KEY_DIM),
        "cdec_f": jnp.repeat(jnp.exp(BLK * lg_f), RET_VAL_DIM)[None, :],
        "cdec_b": jnp.repeat(jnp.exp(BLK * lg_b), RET_VAL_DIM)[None, :],
    }


def _group_avg_matrix():
    head = jnp.arange(RET_V_WIDTH) // RET_VAL_DIM
    return jnp.where(head[:, None] == head[None, :], 1.0 / RET_VAL_DIM, 0.0).astype(BF16)


def kernel(x, ln_emb_g, ln_emb_b, w_in, b_gate, sink_logits, decay_fwd, decay_bwd,
           w_o_attn, w_o_ret, w_out, ln1_g, ln1_b, ffn_w1, ffn_w3, ffn_w2,
           router_w, router_b, moe_w1, moe_w3, moe_w2, ln2_g, ln2_b):
    batch, seq, _ = x.shape
    nb = seq // BLK
    xt = x.reshape(batch * seq, D_MODEL)
    attn_bias = _attn_bias_table()
    group_avg = _group_avg_matrix()
    for layer in range(DEPTH):
        tabs = _retention_tables(decay_fwd[layer], decay_bwd[layer])
        tabs["attn_bias"] = attn_bias
        tabs["group_avg"] = group_avg
        w = w_in[layer].astype(BF16)
        if layer == 0:
            h, qa, kva, ret, gl, kvf, kvb = _in_proj(xt, w, tabs, ln=(ln_emb_g, ln_emb_b))
        else:
            h = xt
            qa, kva, ret, gl, kvf, kvb = _in_proj(xt, w, tabs)
        prev_f, prev_b = _ret_scan(kvf, kvb, tabs, batch, nb)
        ya, r = _mixer(qa, kva, ret, prev_f, prev_b, LOG2E * sink_logits[layer].astype(F32), tabs, batch, nb)
        x1 = _out_proj(h, ya, r, gl, b_gate[layer], w_o_attn[layer].astype(BF16),
                       w_o_ret[layer].astype(BF16), w_out[layer].astype(BF16),
                       ln1_g[layer], ln1_b[layer])
        i = layer // 2
        if layer % 2 == 0:
            xt = _ffn(x1, ffn_w1[i].astype(BF16), ffn_w3[i].astype(BF16), ffn_w2[i].astype(BF16),
                      ln2_g[layer], ln2_b[layer])
        else:
            xt = _moe(x1, router_w[i], router_b[i], moe_w1[i].astype(BF16), moe_w3[i].astype(BF16),
                      moe_w2[i].astype(BF16), ln2_g[layer], ln2_b[layer])
    return xt.reshape(batch, seq, D_MODEL)
```

```python
import functools

import jax
import jax.numpy as jnp
from jax import lax
from jax.experimental import pallas as pl
from jax.experimental.pallas import tpu as pltpu

F32 = jnp.float32
BF16 = jnp.bfloat16

D_MODEL = 1024
DEPTH = 2
N_Q_HEADS = 8
N_KV_HEADS = 2
HEAD_DIM = 64
WINDOW = 128
BLK = 128
ATTN_Q_WIDTH = N_Q_HEADS * HEAD_DIM
ATTN_KV_WIDTH = N_KV_HEADS * HEAD_DIM
N_RET_HEADS = 8
RET_KEY_DIM = 32
RET_VAL_DIM = 64
RET_QK_WIDTH = N_RET_HEADS * RET_KEY_DIM
RET_V_WIDTH = N_RET_HEADS * RET_VAL_DIM
RET_WIDTH = 2 * RET_QK_WIDTH + 3 * RET_V_WIDTH
D_FF_DENSE = 2816
N_EXPERTS = 8
D_FF_EXPERT = 1408
IN_WIDTH = ATTN_Q_WIDTH + 2 * ATTN_KV_WIDTH + RET_WIDTH + 2 * D_MODEL
DEEPNORM_ALPHA = (2 * DEPTH) ** 0.25
LN_EPS = 1e-5
GN_EPS = 1e-5
NEG_INF = -1e30
LOG2E = 1.4426950408889634

TOKEN_TILE = 512
MOE_TOKEN_TILE = 1024
MOE_SLAB = 288
MOE_SLAB_K = 384
MOE_COMBINE_K = 256
LANES = 128
SLOT_RADIX = 32.0
MIX_CHUNKS = 4
FF_CHUNK = 256
VMEM_LIMIT_BYTES = 56 * 1024 * 1024


def _params(*sem):
    return pltpu.CompilerParams(dimension_semantics=sem, vmem_limit_bytes=VMEM_LIMIT_BYTES)


def _layer_norm(x, g, b):
    mu = jnp.mean(x, axis=-1, keepdims=True)
    d = x - mu
    var = jnp.mean(d * d, axis=-1, keepdims=True)
    return d * lax.rsqrt(var + LN_EPS) * g + b


def _sigmoid(x):
    return 1.0 / (1.0 + jnp.exp(-x))


def _dot(a, b):
    return jnp.dot(a, b, preferred_element_type=F32)


def _dot_nt(a, b):
    return lax.dot_general(a, b, (((1,), (1,)), ((), ())), preferred_element_type=F32)


def _head_block_mask(rows, cols, row_div, col_div):
    r = lax.broadcasted_iota(jnp.int32, (rows, cols), 0) // row_div
    c = lax.broadcasted_iota(jnp.int32, (rows, cols), 1) // col_div
    return r == c


def _in_proj_kernel(apply_ln, *refs):
    if apply_ln:
        (x_ref, g_ref, b_ref, w_ref, kdf_ref, kdb_ref,
         h_ref, qa_ref, kva_ref, ret_ref, gl_ref, kvf_ref, kvb_ref) = refs
        x = _layer_norm(x_ref[...], g_ref[...], b_ref[...])
        h_ref[...] = x
    else:
        x_ref, w_ref, kdf_ref, kdb_ref, qa_ref, kva_ref, ret_ref, gl_ref, kvf_ref, kvb_ref = refs
        x = x_ref[...]
    xb = x.astype(BF16)

    def proj(lo, hi):
        return _dot(xb, w_ref[:, lo:hi])

    c0 = ATTN_Q_WIDTH
    c1 = c0 + 2 * ATTN_KV_WIDTH
    qa_ref[...] = (proj(0, c0) * (HEAD_DIM ** -0.5 * LOG2E)).astype(BF16)
    kva_ref[...] = proj(c0, c1).astype(BF16)
    ret_ref[:, :RET_QK_WIDTH] = proj(c1, c1 + RET_QK_WIDTH).astype(BF16)
    rk = (proj(c1 + RET_QK_WIDTH, c1 + 2 * RET_QK_WIDTH) * (RET_KEY_DIM ** -0.5)).astype(BF16)
    ret_ref[:, RET_QK_WIDTH:2 * RET_QK_WIDTH] = rk
    rest = proj(c1 + 2 * RET_QK_WIDTH, c1 + RET_WIDTH).astype(BF16)
    ret_ref[:, 2 * RET_QK_WIDTH:] = rest
    gl_ref[...] = proj(c1 + RET_WIDTH, IN_WIDTH).astype(BF16)

    half = N_RET_HEADS // 2
    mask = _head_block_mask(half * RET_KEY_DIM, half * RET_VAL_DIM, RET_KEY_DIM, RET_VAL_DIM)
    for c in range(x.shape[0] // BLK):
        rows = slice(c * BLK, (c + 1) * BLK)
        k32 = rk[rows].astype(F32)
        v = rest[rows, :RET_V_WIDTH]
        for kd_ref, out_ref in ((kdf_ref, kvf_ref), (kdb_ref, kvb_ref)):
            kz = (k32 * kd_ref[...]).T.astype(BF16)
            comps = []
            for g in range(2):
                full = jnp.where(mask, _dot(kz[g * half * RET_KEY_DIM:(g + 1) * half * RET_KEY_DIM],
                                            v[:, g * half * RET_VAL_DIM:(g + 1) * half * RET_VAL_DIM]), 0.0)
                comp = full[0:RET_KEY_DIM]
                for h in range(1, half):
                    comp = comp + full[h * RET_KEY_DIM:(h + 1) * RET_KEY_DIM]
                comps.append(comp)
            out_ref[c] = jnp.concatenate(comps, axis=1)


def _in_proj(x, w, tabs, ln=None):
    t = x.shape[0]
    tm = TOKEN_TILE
    row = lambda i: (i, 0)
    const = lambda i: (0, 0)
    in_specs = [pl.BlockSpec((tm, D_MODEL), row)]
    args = [x]
    out_shape = []
    out_specs = []
    if ln is not None:
        in_specs += [pl.BlockSpec((1, D_MODEL), const), pl.BlockSpec((1, D_MODEL), const)]
        args += [ln[0].reshape(1, D_MODEL), ln[1].reshape(1, D_MODEL)]
        out_shape.append(jax.ShapeDtypeStruct((t, D_MODEL), F32))
        out_specs.append(pl.BlockSpec((tm, D_MODEL), row))
    in_specs += [pl.BlockSpec((D_MODEL, IN_WIDTH), const),
                 pl.BlockSpec((BLK, RET_QK_WIDTH), const), pl.BlockSpec((BLK, RET_QK_WIDTH), const)]
    args += [w, tabs["kdec_f"], tabs["kdec_b"]]
    for width in (ATTN_Q_WIDTH, 2 * ATTN_KV_WIDTH, RET_WIDTH, 2 * D_MODEL):
        out_shape.append(jax.ShapeDtypeStruct((t, width), BF16))
        out_specs.append(pl.BlockSpec((tm, width), row))
    for _ in range(2):
        out_shape.append(jax.ShapeDtypeStruct((t // BLK, RET_KEY_DIM, RET_V_WIDTH), F32))
        out_specs.append(pl.BlockSpec((tm // BLK, RET_KEY_DIM, RET_V_WIDTH), lambda i: (i, 0, 0)))
    return pl.pallas_call(
        functools.partial(_in_proj_kernel, ln is not None),
        grid=(t // tm,),
        in_specs=in_specs,
        out_specs=out_specs,
        out_shape=out_shape,
        compiler_params=_params("parallel"),
        name="in_proj",
    )(*args)


def _ret_scan_kernel(nb, kvf_ref, kvb_ref, cdf_ref, cdb_ref, pf_ref, pb_ref):
    zero = jnp.zeros((RET_KEY_DIM, RET_V_WIDTH), F32)

    def fwd(i, s):
        pf_ref[i] = s
        return s * cdf_ref[...] + kvf_ref[i]

    def bwd(i, s):
        n = nb - 1 - i
        pb_ref[n] = s
        return s * cdb_ref[...] + kvb_ref[n]

    lax.fori_loop(0, nb, fwd, zero)
    lax.fori_loop(0, nb, bwd, zero)


def _ret_scan(kvf, kvb, tabs, batch, nb):
    seq = lambda b: (b, 0, 0)
    const = lambda b: (0, 0)
    spec = pl.BlockSpec((nb, RET_KEY_DIM, RET_V_WIDTH), seq)
    shape = jax.ShapeDtypeStruct(kvf.shape, F32)
    return pl.pallas_call(
        functools.partial(_ret_scan_kernel, nb),
        grid=(batch,),
        in_specs=[spec, spec, pl.BlockSpec((1, RET_V_WIDTH), const), pl.BlockSpec((1, RET_V_WIDTH), const)],
        out_specs=[spec, spec],
        out_shape=[shape, shape],
        compiler_params=_params("parallel"),
        name="ret_scan",
    )(kvf, kvb, tabs["cdec_f"], tabs["cdec_b"])


def _split_bf16(y):
    hi = y.astype(BF16)
    lo = (y - hi.astype(F32)).astype(BF16)
    return hi, lo


def _mixer_body(n, nb, sink_ref, qa_ref, kvp_ref, kvc_ref, kvn_ref, ret_ref, pf_ref, pb_ref,
                bias_ref, df_ref, db_ref, qdf_ref, qdb_ref, gavg_ref, ya_ref, r_ref):
    q = qa_ref[...]
    kv = jnp.concatenate([kvp_ref[...], kvc_ref[...], kvn_ref[...]], axis=0).astype(F32)
    lo_half = lax.broadcasted_iota(jnp.int32, (kv.shape[0], 2 * HEAD_DIM), 1) < HEAD_DIM

    def lane_variants(a):
        swapped = pltpu.roll(a, HEAD_DIM, 1)
        z = jnp.zeros_like(a)
        return (((jnp.where(lo_half, a, z)).astype(BF16), (jnp.where(lo_half, z, swapped)).astype(BF16)),
                ((jnp.where(lo_half, swapped, z)).astype(BF16), (jnp.where(lo_half, z, a)).astype(BF16)))

    k_var = lane_variants(kv[:, :ATTN_KV_WIDTH])
    v_var = lane_variants(kv[:, ATTN_KV_WIDTH:])
    top_rows = lax.broadcasted_iota(jnp.int32, (2 * BLK, 1), 0) < BLK
    out_lo_half = lax.broadcasted_iota(jnp.int32, (2 * BLK, 2 * HEAD_DIM), 1) < HEAD_DIM
    half_heads = N_RET_HEADS // 2
    k_mask = _head_block_mask(N_RET_HEADS * BLK, RET_QK_WIDTH, BLK, RET_KEY_DIM)
    v_mask = _head_block_mask(half_heads * BLK, half_heads * RET_VAL_DIM, BLK, RET_VAL_DIM)
    s_mask = _head_block_mask(RET_QK_WIDTH, RET_V_WIDTH, RET_KEY_DIM, RET_VAL_DIM)
    pair = 2 * HEAD_DIM
    ys = []
    for j in range(MIX_CHUNKS):
        c = n * MIX_CHUNKS + j
        var = jnp.where(c == 0, 0, jnp.where(c == nb - 1, 2, 1))
        tok = slice(j * BLK, (j + 1) * BLK)
        keys = slice(j * BLK, (j + 3) * BLK)

        for g in range(N_KV_HEADS):
            q2 = jnp.concatenate([q[tok, pair * 2 * g:pair * (2 * g + 1)],
                                  q[tok, pair * (2 * g + 1):pair * (2 * g + 2)]], axis=0)
            s_all = _dot_nt(q2, jnp.concatenate([k_var[g][0][keys], k_var[g][1][keys]], axis=0))
            probs, denoms = [], []
            for r in range(2):
                sink = jnp.where(top_rows, sink_ref[4 * g + r], sink_ref[4 * g + 2 + r])
                s = s_all[:, 3 * BLK * r:3 * BLK * (r + 1)] + bias_ref[var, g, r]
                m = jnp.maximum(jnp.max(s, axis=1, keepdims=True), sink)
                e = jnp.exp2(s - m)
                denoms.append(jnp.sum(e, axis=1, keepdims=True) + jnp.exp2(sink - m))
                probs.append(e.astype(BF16))
            pv = _dot(jnp.concatenate(probs, axis=1),
                      jnp.concatenate([v_var[g][0][keys], v_var[g][1][keys]], axis=0))
            out = pv / jnp.where(out_lo_half, denoms[0], denoms[1])
            ya_ref[tok, pair * 2 * g:pair * (2 * g + 1)] = out[:BLK].astype(BF16)
            ya_ref[tok, pair * (2 * g + 1):pair * (2 * g + 2)] = out[BLK:].astype(BF16)

        rq = ret_ref[tok, :RET_QK_WIDTH]
        rk = ret_ref[tok, RET_QK_WIDTH:2 * RET_QK_WIDTH]
        rv = ret_ref[tok, 2 * RET_QK_WIDTH:2 * RET_QK_WIDTH + RET_V_WIDTH]
        k_exp = jnp.where(k_mask, jnp.concatenate([rk] * N_RET_HEADS, axis=0), jnp.zeros((), BF16))
        qk = _dot_nt(rq, k_exp)
        p_fb = jnp.concatenate([(qk * df_ref[...]).astype(BF16), (qk * db_ref[...]).astype(BF16)], axis=0)
        intra = []
        for g in range(2):
            cols = slice(g * half_heads * RET_VAL_DIM, (g + 1) * half_heads * RET_VAL_DIM)
            v_bd = jnp.where(v_mask, jnp.concatenate([rv[:, cols]] * half_heads, axis=0), jnp.zeros((), BF16))
            intra.append(_dot(p_fb[:, g * half_heads * BLK:(g + 1) * half_heads * BLK], v_bd))
        cross = []
        for state_ref, qdec_ref in ((pf_ref, qdf_ref), (pb_ref, qdb_ref)):
            s_bd = jnp.where(s_mask, jnp.concatenate([state_ref[j]] * N_RET_HEADS, axis=0), 0.0)
            cross.append(_dot(rq, s_bd.astype(BF16)) * qdec_ref[...])
        ys.append(jnp.concatenate(intra, axis=1) + jnp.concatenate(cross, axis=0))

    y = jnp.concatenate(ys, axis=0)
    d = y - _dot(y.astype(BF16), gavg_ref[...])
    normed = d * lax.rsqrt(_dot((d * d).astype(BF16), gavg_ref[...]) + GN_EPS)
    for j in range(MIX_CHUNKS):
        tok = slice(j * BLK, (j + 1) * BLK)
        g_f = ret_ref[tok, 2 * RET_QK_WIDTH + RET_V_WIDTH:2 * RET_QK_WIDTH + 2 * RET_V_WIDTH].astype(F32)
        g_b = ret_ref[tok, 2 * RET_QK_WIDTH + 2 * RET_V_WIDTH:].astype(F32)
        r = (g_f * _sigmoid(g_f) * normed[2 * j * BLK:(2 * j + 1) * BLK]
             + g_b * _sigmoid(g_b) * normed[(2 * j + 1) * BLK:(2 * j + 2) * BLK])
        r_ref[tok, :] = r.astype(BF16)


def _mix_proj_kernel(nb, n_blocks, sink_ref, qa_ref, kvp_ref, kvc_ref, kvn_ref, ret_ref, pf_ref, pb_ref,
                     bias_ref, df_ref, db_ref, qdf_ref, qdb_ref, gavg_ref,
                     h_ref, gl_ref, bg_ref, woa_ref, wor_ref, wout_ref, g_ref, b_ref,
                     x_ref, ya_scr, r_scr):
    s = pl.program_id(0)

    @pl.when(s == 0)
    def _():
        ya_scr[...] = jnp.zeros_like(ya_scr)
        r_scr[...] = jnp.zeros_like(r_scr)

    slot = lax.rem(s, 2)
    ya_prev = ya_scr[1 - slot]
    r_prev = r_scr[1 - slot]
    gates = _sigmoid(gl_ref[...].astype(F32) + bg_ref[...])
    merged = (gates[:, :D_MODEL] * _dot(ya_prev, woa_ref[...])
              + gates[:, D_MODEL:] * _dot(r_prev, wor_ref[...]))
    mix = _dot(merged.astype(BF16), wout_ref[...])
    x_ref[...] = _layer_norm(DEEPNORM_ALPHA * h_ref[...] + mix, g_ref[...], b_ref[...])

    n = lax.rem(jnp.minimum(s, n_blocks - 1), nb // MIX_CHUNKS)
    _mixer_body(n, nb, sink_ref, qa_ref, kvp_ref, kvc_ref, kvn_ref, ret_ref, pf_ref, pb_ref,
                bias_ref, df_ref, db_ref, qdf_ref, qdb_ref, gavg_ref, ya_scr.at[slot], r_scr.at[slot])


def _mix_proj(qa, kva, ret, prev_f, prev_b, sink, tabs, h, gl, b_gate, woa, wor, wout, g, b, batch, nb):
    t = qa.shape[0]
    steps = nb // MIX_CHUNKS
    rows = MIX_CHUNKS * BLK
    n_blocks = batch * steps

    def mixed(s):
        return jnp.minimum(s, n_blocks - 1)

    def projected(s):
        return jnp.maximum(s - 1, 0)

    cur = lambda s: (mixed(s), 0)
    prv = lambda s: (mixed(s) * MIX_CHUNKS - jnp.where(lax.rem(mixed(s), steps) == 0, 0, 1), 0)
    nxt = lambda s: (mixed(s) * MIX_CHUNKS + MIX_CHUNKS - jnp.where(lax.rem(mixed(s), steps) == steps - 1, 1, 0), 0)
    out = lambda s: (projected(s), 0)
    c2 = lambda s: (0, 0)
    state_spec = pl.BlockSpec((MIX_CHUNKS, RET_KEY_DIM, RET_V_WIDTH), lambda s: (mixed(s), 0, 0))
    return pl.pallas_call(
        functools.partial(_mix_proj_kernel, nb, n_blocks),
        grid=(n_blocks + 1,),
        in_specs=[
            pl.BlockSpec(memory_space=pltpu.SMEM),
            pl.BlockSpec((rows, ATTN_Q_WIDTH), cur),
            pl.BlockSpec((BLK, 2 * ATTN_KV_WIDTH), prv),
            pl.BlockSpec((rows, 2 * ATTN_KV_WIDTH), cur),
            pl.BlockSpec((BLK, 2 * ATTN_KV_WIDTH), nxt),
            pl.BlockSpec((rows, RET_WIDTH), cur),
            state_spec,
            state_spec,
            pl.BlockSpec((3, N_KV_HEADS, 2, 2 * BLK, 3 * BLK), lambda s: (0, 0, 0, 0, 0)),
            pl.BlockSpec((BLK, N_RET_HEADS * BLK), c2),
            pl.BlockSpec((BLK, N_RET_HEADS * BLK), c2),
            pl.BlockSpec((BLK, RET_V_WIDTH), c2),
            pl.BlockSpec((BLK, RET_V_WIDTH), c2),
            pl.BlockSpec((RET_V_WIDTH, RET_V_WIDTH), c2),
            pl.BlockSpec((rows, D_MODEL), out),
            pl.BlockSpec((rows, 2 * D_MODEL), out),
            pl.BlockSpec((1, 2 * D_MODEL), c2),
            pl.BlockSpec((ATTN_Q_WIDTH, D_MODEL), c2),
            pl.BlockSpec((RET_V_WIDTH, D_MODEL), c2),
            pl.BlockSpec((D_MODEL, D_MODEL), c2),
            pl.BlockSpec((1, D_MODEL), c2),
            pl.BlockSpec((1, D_MODEL), c2),
        ],
        out_specs=pl.BlockSpec((rows, D_MODEL), out),
        out_shape=jax.ShapeDtypeStruct((t, D_MODEL), F32),
        scratch_shapes=[pltpu.VMEM((2, rows, ATTN_Q_WIDTH), BF16), pltpu.VMEM((2, rows, RET_V_WIDTH), BF16)],
        compiler_params=_params("arbitrary"),
        name="mix_proj",
    )(sink, qa, kva, kva, kva, ret, prev_f, prev_b, tabs["attn_bias"], tabs["intra_f"],
      tabs["intra_b"], tabs["qdec_f"], tabs["qdec_b"], tabs["group_avg"],
      h, gl, b_gate.reshape(1, -1), woa, wor, wout, g.reshape(1, -1), b.reshape(1, -1))


def _ffn_kernel(x_ref, w1_ref, w3_ref, w2_ref, g_ref, b_ref, o_ref):
    x = x_ref[...]
    xb = x.astype(BF16)
    acc = None
    for c in range(D_FF_DENSE // FF_CHUNK):
        cols = slice(c * FF_CHUNK, (c + 1) * FF_CHUNK)
        a = _dot(xb, w1_ref[:, cols])
        hid = a * _sigmoid(a) * _dot(xb, w3_ref[:, cols])
        part = _dot(hid.astype(BF16), w2_ref[cols, :])
        acc = part if acc is None else acc + part
    o_ref[...] = _layer_norm(DEEPNORM_ALPHA * x + acc, g_ref[...], b_ref[...])


def _ffn(x, w1, w3, w2, g, b):
    t = x.shape[0]
    tm = TOKEN_TILE
    row = lambda i: (i, 0)
    const = lambda i: (0, 0)
    return pl.pallas_call(
        _ffn_kernel,
        grid=(t // tm,),
        in_specs=[
            pl.BlockSpec((tm, D_MODEL), row),
            pl.BlockSpec((D_MODEL, D_FF_DENSE), const),
            pl.BlockSpec((D_MODEL, D_FF_DENSE), const),
            pl.BlockSpec((D_FF_DENSE, D_MODEL), const),
            pl.BlockSpec((1, D_MODEL), const),
            pl.BlockSpec((1, D_MODEL), const),
        ],
        out_specs=pl.BlockSpec((tm, D_MODEL), row),
        out_shape=jax.ShapeDtypeStruct((t, D_MODEL), F32),
        compiler_params=_params("parallel"),
        name="ffn",
    )(x, w1, w3, w2, g.reshape(1, -1), b.reshape(1, -1))


def _route_kernel(x_ref, rw_ref, rb_ref, tri_ref, xb_ref, tcol_ref, pcol_ref, drow_ref, cnt_ref):
    x = x_ref[...]
    tm, lanes = x.shape[0], LANES
    lane = lax.broadcasted_iota(jnp.int32, (tm, lanes), 1)
    xh, xl = _split_bf16(x)
    wh, wl = _split_bf16(rw_ref[...])
    hi_part = _dot(xh, jnp.concatenate([wh, wl], axis=1))
    logits = hi_part[:, :lanes] + hi_part[:, lanes:] + _dot(xl, wh) + rb_ref[...]
    logits = jnp.where(lane < N_EXPERTS, logits, NEG_INF)
    m1 = jnp.max(logits, axis=1, keepdims=True)
    i1 = jnp.min(jnp.where(logits == m1, lane, lanes), axis=1, keepdims=True)
    rest = jnp.where(lane == i1, NEG_INF, logits)
    m2 = jnp.max(rest, axis=1, keepdims=True)
    i2 = jnp.min(jnp.where(rest == m2, lane, lanes), axis=1, keepdims=True)
    e2 = jnp.exp(m2 - m1)
    w_top = 1.0 / (1.0 + e2)
    gate =jnp.where(lane == i1, w_top, 0.0) + jnp.where(lane == i2, e2 * w_top, 0.0)
    sel = jnp.where(lane == i1, 1.0, jnp.where(lane == i2, 1.0, 0.0))
    prefix = _dot(tri_ref[...], sel.astype(BF16))
    slot1 = jnp.where(sel > 0.0, prefix, 0.0)
    hi = jnp.floor(slot1 * (1.0 / SLOT_RADIX))
    tcol_ref[:, :lanes] = gate.astype(BF16)
    tcol_ref[:, lanes:2 * lanes] = hi.astype(BF16)
    tcol_ref[:, 2 * lanes:] = (slot1 - SLOT_RADIX * hi).astype(BF16)

    def first_slab_pos(idx, picked):
        rank = jnp.sum(jnp.where(picked, prefix, 0.0), axis=1, keepdims=True) - 1.0
        return jnp.where(rank < MOE_SLAB, idx.astype(F32) * MOE_SLAB + rank, -1.0)

    pcol_ref[...] = jnp.where(
        lane == 0, first_slab_pos(i1, lane == i1), jnp.where(
            lane == 1, w_top, jnp.where(
                lane == 2, first_slab_pos(i2, lane == i2), jnp.where(lane == 3, e2 * w_top, 0.0))))
    drow_ref[...] = (slot1 - 1.0).T[:N_EXPERTS]
    cnt_ref[...] = prefix[tm - 1:tm].astype(jnp.int32)
    xb_ref[...] = x.astype(BF16)


def _route(x, router_w, router_b):
    t = x.shape[0]
    tm = MOE_TOKEN_TILE
    lanes = 128
    rw = jnp.zeros((D_MODEL, lanes), F32).at[:, :N_EXPERTS].set(router_w)
    rb = jnp.zeros((1, lanes), F32).at[0, :N_EXPERTS].set(router_b)
    tri = (jnp.arange(tm)[:, None] >= jnp.arange(tm)[None, :]).astype(BF16)
    row = lambda i: (i, 0)
    const = lambda i: (0, 0)
    return pl.pallas_call(
        _route_kernel,
        grid=(t // tm,),
        in_specs=[
            pl.BlockSpec((tm, D_MODEL), row),
            pl.BlockSpec((D_MODEL, lanes), const),
            pl.BlockSpec((1, lanes), const),
            pl.BlockSpec((tm, tm), const),
        ],
        out_specs=[
            pl.BlockSpec((tm, D_MODEL), row),
            pl.BlockSpec((tm, 3 * lanes), row),
            pl.BlockSpec((tm, lanes), row),
            pl.BlockSpec((N_EXPERTS, tm), row),
            pl.BlockSpec((None, 1, lanes), lambda i: (i, 0, 0)),
        ],
        out_shape=[
            jax.ShapeDtypeStruct((t, D_MODEL), BF16),
            jax.ShapeDtypeStruct((t, 3 * lanes), BF16),
            jax.ShapeDtypeStruct((t, lanes), F32),
            jax.ShapeDtypeStruct((t // tm * N_EXPERTS, tm), F32),
            jax.ShapeDtypeStruct((t // tm, 1, lanes), jnp.int32),
        ],
        compiler_params=_params("parallel"),
        name="route",
    )(x, rw, rb, tri)


def _moe_kernel(cnt_ref, x_ref, xb_ref, tcol_ref, pcol_ref, drow_ref, w1_ref, w3_ref, w2_ref,
                g_ref, b_ref, o_ref, y_ref):
    i = pl.program_id(0)
    e = pl.program_id(1)

    @pl.when(e == 0)
    def _():
        o_ref[...] = jnp.zeros_like(o_ref)

    slot_row = drow_ref[pl.ds(e, 1), :]
    rows = lax.broadcasted_iota(jnp.int32, (MOE_SLAB, 1), 0).astype(F32)

    def expert_ffn(base):
        pick = jnp.where(slot_row == rows + base, 1.0, 0.0).astype(BF16)
        xs = _dot(pick, xb_ref[...]).astype(BF16)
        a = _dot(xs, w1_ref[...])
        hid = a * _sigmoid(a) * _dot(xs, w3_ref[...])
        return _dot(hid.astype(BF16), w2_ref[...]).astype(BF16)

    y_ref[pl.ds(pl.multiple_of(e * MOE_SLAB, 16), MOE_SLAB), :] = expert_ffn(0.0)

    def later_slab(s, carry):
        base = (s * MOE_SLAB).astype(F32)
        pick_e = jnp.where(lax.broadcasted_iota(jnp.int32, (LANES, LANES), 0) == e, 1.0, 0.0).astype(BF16)
        gate_b = _dot(tcol_ref[:, :LANES], pick_e)
        slot_b = (SLOT_RADIX * _dot(tcol_ref[:, LANES:2 * LANES], pick_e)
                  + _dot(tcol_ref[:, 2 * LANES:], pick_e) - 1.0)
        cols = lax.broadcasted_iota(jnp.int32, (1, LANES), 1).astype(F32)
        y = jnp.concatenate([expert_ffn(base), jnp.zeros((MOE_SLAB_K - MOE_SLAB, D_MODEL), BF16)], axis=0)
        put = jnp.concatenate(
            [jnp.where(slot_b == cols + (base + k * LANES), gate_b, 0.0).astype(BF16)
             for k in range(MOE_SLAB_K // LANES)], axis=1)
        o_ref[...] += _dot(put, y)
        return carry

    n_slabs = (cnt_ref[i * N_EXPERTS + e] + MOE_SLAB - 1) // MOE_SLAB
    lax.fori_loop(1, n_slabs, later_slab, 0)

    @pl.when(e == N_EXPERTS - 1)
    def _():
        p = pcol_ref[...]
        pos_a, gate_a, pos_b, gate_b = p[:, 0:1], p[:, 1:2], p[:, 2:3], p[:, 3:4]
        acc = DEEPNORM_ALPHA * x_ref[...] + o_ref[...]
        for k in range(N_EXPERTS * MOE_SLAB // MOE_COMBINE_K):
            cols = lax.broadcasted_iota(jnp.int32, (1, MOE_COMBINE_K), 1).astype(F32) + float(k * MOE_COMBINE_K)
            put = jnp.where(pos_a == cols, gate_a, jnp.where(pos_b == cols, gate_b, 0.0)).astype(BF16)
            acc = acc + _dot(put, y_ref[k * MOE_COMBINE_K:(k + 1) * MOE_COMBINE_K, :])
        o_ref[...] = _layer_norm(acc, g_ref[...], b_ref[...])


def _moe(x, router_w, router_b, w1, w3, w2, g, b):
    t = x.shape[0]
    tm = MOE_TOKEN_TILE
    xb, tcol, pcol, drow, cnt = _route(x, router_w, router_b)
    counts = cnt[:, 0, :N_EXPERTS].reshape(-1)
    row = lambda i, e, c: (i, 0)
    const = lambda i, e, c: (0, 0)
    expert = lambda i, e, c: (e, 0, 0)
    grid_spec = pltpu.PrefetchScalarGridSpec(
        num_scalar_prefetch=1,
        grid=(t // tm, N_EXPERTS),
        in_specs=[
            pl.BlockSpec((tm, D_MODEL), row),
            pl.BlockSpec((tm, D_MODEL), row),
            pl.BlockSpec((tm, 3 * LANES), row),
            pl.BlockSpec((tm, LANES), row),
            pl.BlockSpec((N_EXPERTS, tm), row),
            pl.BlockSpec((None, D_MODEL, D_FF_EXPERT), expert),
            pl.BlockSpec((None, D_MODEL, D_FF_EXPERT), expert),
            pl.BlockSpec((None, D_FF_EXPERT, D_MODEL), expert),
            pl.BlockSpec((1, D_MODEL), const),
            pl.BlockSpec((1, D_MODEL), const),
        ],
        out_specs=pl.BlockSpec((tm, D_MODEL), row),
        scratch_shapes=[pltpu.VMEM((N_EXPERTS * MOE_SLAB, D_MODEL), BF16)],
    )
    return pl.pallas_call(
        _moe_kernel,
        grid_spec=grid_spec,
        out_shape=jax.ShapeDtypeStruct((t, D_MODEL), F32),
        compiler_params=_params("parallel", "arbitrary"),
        name="moe",
    )(counts, x, xb, tcol, pcol, drow, w1, w3, w2, g.reshape(1, -1), b.reshape(1, -1))


def _attn_bias_table():
    qi = jnp.arange(BLK)[:, None]
    kj = jnp.arange(3 * BLK)[None, :]
    dist = jnp.abs(qi - kj + BLK)
    slopes = jnp.exp2(-8.0 * jnp.arange(1, N_Q_HEADS + 1, dtype=F32) / N_Q_HEADS)
    bias = -(LOG2E * slopes)[:, None, None] * dist.astype(F32)[None]
    in_window = (dist <= WINDOW)[None]
    variants = []
    for ok in (kj >= BLK, kj >= 0, kj < 2 * BLK):
        per_head = jnp.where(in_window & ok[None], bias, NEG_INF)
        per_head = per_head.reshape(N_KV_HEADS, 2, 2, BLK, 3 * BLK)
        variants.append(jnp.transpose(per_head, (0, 2, 1, 3, 4)).reshape(N_KV_HEADS, 2, 2 * BLK, 3 * BLK))
    return jnp.stack(variants)


def _retention_tables(decay_fwd, decay_bwd):
    pos = jnp.arange(BLK, dtype=F32)
    diff = jnp.arange(BLK)[:, None] - jnp.arange(BLK)[None, :]
    lg_f = jax.nn.log_sigmoid(decay_fwd.astype(F32))
    lg_b = jax.nn.log_sigmoid(decay_bwd.astype(F32))

    def intra(lg, d, mask):
        dec = jnp.where(mask[None], jnp.exp(jnp.maximum(d, 0).astype(F32)[None] * lg[:, None, None]), 0.0)
        return jnp.transpose(dec, (1, 0, 2)).reshape(BLK, N_RET_HEADS * BLK)

    def per_pos(lg, expo, width):
        return jnp.repeat(jnp.exp(expo[:, None] * lg[None, :]), width, axis=1)

    return {
        "intra_f": intra(lg_f, diff, diff >= 0),
        "intra_b": intra(lg_b, -diff, diff < 0),
        "qdec_f": per_pos(lg_f, pos + 1.0, RET_VAL_DIM),
        "qdec_b": per_pos(lg_b, BLK - pos, RET_VAL_DIM),
        "kdec_f": per_pos(lg_f, BLK - 1.0 - pos, RET_KEY_DIM),
        "kdec_b": per_pos(lg_b, pos, RET_KEY_DIM),
        "cdec_f": jnp.repeat(jnp.exp(BLK * lg_f), RET_VAL_DIM)[None, :],
        "cdec_b": jnp.repeat(jnp.exp(BLK * lg_b), RET_VAL_DIM)[None, :],
    }


def _group_avg_matrix():
    head = jnp.arange(RET_V_WIDTH) // RET_VAL_DIM
    return jnp.where(head[:, None] == head[None, :], 1.0 / RET_VAL_DIM, 0.0).astype(BF16)


def kernel(x, ln_emb_g, ln_emb_b, w_in, b_gate, sink_logits, decay_fwd, decay_bwd,
           w_o_attn, w_o_ret, w_out, ln1_g, ln1_b, ffn_w1, ffn_w3, ffn_w2,
           router_w, router_b, moe_w1, moe_w3, moe_w2, ln2_g, ln2_b):
    batch, seq, _ = x.shape
    nb = seq // BLK
    xt = x.reshape(batch * seq, D_MODEL)
    attn_bias = _attn_bias_table()
    group_avg = _group_avg_matrix()
    for layer in range(DEPTH):
        tabs = _retention_tables(decay_fwd[layer], decay_bwd[layer])
        tabs["attn_bias"] = attn_bias
        tabs["group_avg"] = group_avg
        w = w_in[layer].astype(BF16)
        if layer == 0:
            h, qa, kva, ret, gl, kvf, kvb = _in_proj(xt, w, tabs, ln=(ln_emb_g, ln_emb_b))
        else:
            h = xt
            qa, kva, ret, gl, kvf, kvb = _in_proj(xt, w, tabs)
        prev_f, prev_b = _ret_scan(kvf, kvb, tabs, batch, nb)
        x1 = _mix_proj(qa, kva, ret, prev_f, prev_b, LOG2E * sink_logits[layer].astype(F32), tabs,
                       h, gl, b_gate[layer], w_o_attn[layer].astype(BF16),
                       w_o_ret[layer].astype(BF16), w_out[layer].astype(BF16),
                       ln1_g[layer], ln1_b[layer], batch, nb)
        i = layer // 2
        if layer % 2 == 0:
            xt = _ffn(x1, ffn_w1[i].astype(BF16), ffn_w3[i].astype(BF16), ffn_w2[i].astype(BF16),
                      ln2_g[layer], ln2_b[layer])
        else:
            xt = _moe(x1, router_w[i], router_b[i], moe_w1[i].astype(BF16), moe_w3[i].astype(BF16),
                      moe_w2[i].astype(BF16), ln2_g[layer], ln2_b[layer])
    return xt.reshape(batch, seq, D_MODEL)
```

```python
import functools

import jax
import jax.numpy as jnp
import numpy as np
from jax import lax
from jax.experimental import pallas as pl
from jax.experimental.pallas import tpu as pltpu

F32 = jnp.float32
BF16 = jnp.bfloat16

D_MODEL = 1024
DEPTH = 2
N_Q_HEADS = 8
N_KV_HEADS = 2
HEAD_DIM = 64
WINDOW = 128
BLK = 128
ATTN_Q_WIDTH = N_Q_HEADS * HEAD_DIM
ATTN_KV_WIDTH = N_KV_HEADS * HEAD_DIM
N_RET_HEADS = 8
RET_KEY_DIM = 32
RET_VAL_DIM = 64
RET_QK_WIDTH = N_RET_HEADS * RET_KEY_DIM
RET_V_WIDTH = N_RET_HEADS * RET_VAL_DIM
RET_WIDTH = 2 * RET_QK_WIDTH + 3 * RET_V_WIDTH
D_FF_DENSE = 2816
N_EXPERTS = 8
D_FF_EXPERT = 1408
IN_WIDTH = ATTN_Q_WIDTH + 2 * ATTN_KV_WIDTH + RET_WIDTH + 2 * D_MODEL
DEEPNORM_ALPHA = (2 * DEPTH) ** 0.25
LN_EPS = 1e-5
GN_EPS = 1e-5
NEG_INF = -1e30
LOG2E = 1.4426950408889634

TOKEN_TILE = 1024
FFN_TOKEN_TILE = 1024
MOE_TOKEN_TILE = 1024
MOE_SLAB = 288
MOE_SLAB_K = 384
MOE_COMBINE_K = 256
LANES = 128
SLOT_RADIX = 32.0
MIX_CHUNKS = 4
FF_CHUNK = 256
VMEM_LIMIT_BYTES = 56 * 1024 * 1024


def _params(*sem):
    return pltpu.CompilerParams(dimension_semantics=sem, vmem_limit_bytes=VMEM_LIMIT_BYTES)


def _layer_norm(x, g, b):
    mu = jnp.mean(x, axis=-1, keepdims=True)
    d = x - mu
    var = jnp.mean(d * d, axis=-1, keepdims=True)
    return d * lax.rsqrt(var + LN_EPS) * g + b


def _sigmoid(x):
    return 1.0 / (1.0 + jnp.exp(-x))


def _dot(a, b):
    return jnp.dot(a, b, preferred_element_type=F32)


def _dot_nt(a, b):
    return lax.dot_general(a, b, (((1,), (1,)), ((), ())), preferred_element_type=F32)


def _head_block_mask(rows, cols, row_div, col_div):
    r = lax.broadcasted_iota(jnp.int32, (rows, cols), 0) // row_div
    c = lax.broadcasted_iota(jnp.int32, (rows, cols), 1) // col_div
    return r == c


def _in_proj_kernel(apply_ln, tiles_per_seq, *refs):
    sb_ref = refs[-1]
    tile = pl.num_programs(0) - 1 - pl.program_id(0)

    @pl.when(lax.rem(tile, tiles_per_seq) == tiles_per_seq - 1)
    def _():
        sb_ref[...] = jnp.zeros_like(sb_ref)

    if apply_ln:
        (x_ref, g_ref, b_ref, w_ref, kdf_ref, kdb_ref, cdb_ref,
         h_ref, qa_ref, kva_ref, ret_ref, gl_ref, kvf_ref, pb_ref, _) = refs
        x = _layer_norm(x_ref[...], g_ref[...], b_ref[...])
        h_ref[...] = x
    else:
        (x_ref, w_ref, kdf_ref, kdb_ref, cdb_ref,
         qa_ref, kva_ref, ret_ref, gl_ref, kvf_ref, pb_ref, _) = refs
        x = x_ref[...]
    xb = x.astype(BF16)

    def proj(lo, hi):
        return _dot(xb, w_ref[:, lo:hi])

    c0 = ATTN_Q_WIDTH
    c1 = c0 + 2 * ATTN_KV_WIDTH
    qa_ref[...] = (proj(0, c0) * (HEAD_DIM ** -0.5 * LOG2E)).astype(BF16)
    kva_ref[...] = proj(c0, c1).astype(BF16)
    ret_ref[:, :RET_QK_WIDTH] = proj(c1, c1 + RET_QK_WIDTH).astype(BF16)
    rk = (proj(c1 + RET_QK_WIDTH, c1 + 2 * RET_QK_WIDTH) * (RET_KEY_DIM ** -0.5)).astype(BF16)
    ret_ref[:, RET_QK_WIDTH:2 * RET_QK_WIDTH] = rk
    rest = proj(c1 + 2 * RET_QK_WIDTH, c1 + RET_WIDTH).astype(BF16)
    ret_ref[:, 2 * RET_QK_WIDTH:] = rest
    gl_ref[...] = (0.5 * proj(c1 + RET_WIDTH, IN_WIDTH)).astype(BF16)

    half = N_RET_HEADS // 2
    mask = _head_block_mask(half * RET_KEY_DIM, half * RET_VAL_DIM, RET_KEY_DIM, RET_VAL_DIM)
    def increment(c, kd_ref):
        rows = slice(c * BLK, (c + 1) * BLK)
        kz = (rk[rows].astype(F32) * kd_ref[...]).T.astype(BF16)
        v = rest[rows, :RET_V_WIDTH]
        comps = []
        for g in range(2):
            full = jnp.where(mask, _dot(kz[g * half * RET_KEY_DIM:(g + 1) * half * RET_KEY_DIM],
                                        v[:, g * half * RET_VAL_DIM:(g + 1) * half * RET_VAL_DIM]), 0.0)
            comp = full[0:RET_KEY_DIM]
            for h in range(1, half):
                comp = comp + full[h * RET_KEY_DIM:(h + 1) * RET_KEY_DIM]
            comps.append(comp)
        return jnp.concatenate(comps, axis=1)

    state = sb_ref[...]
    for c in reversed(range(x.shape[0] // BLK)):
        kvf_ref[c] = increment(c, kdf_ref)
        pb_ref[c] = state
        state = state * cdb_ref[...] + increment(c, kdb_ref)
    sb_ref[...] = state


def _in_proj(x, w, tabs, seq, ln=None):
    t = x.shape[0]
    tm = TOKEN_TILE
    tiles = t // tm
    row = lambda i: (tiles - 1 - i, 0)
    const = lambda i: (0, 0)
    in_specs = [pl.BlockSpec((tm, D_MODEL), row)]
    args = [x]
    out_shape = []
    out_specs = []
    if ln is not None:
        in_specs += [pl.BlockSpec((1, D_MODEL), const), pl.BlockSpec((1, D_MODEL), const)]
        args += [ln[0].reshape(1, D_MODEL), ln[1].reshape(1, D_MODEL)]
        out_shape.append(jax.ShapeDtypeStruct((t, D_MODEL), F32))
        out_specs.append(pl.BlockSpec((tm, D_MODEL), row))
    in_specs += [pl.BlockSpec((D_MODEL, IN_WIDTH), const),
                 pl.BlockSpec((BLK, RET_QK_WIDTH), const), pl.BlockSpec((BLK, RET_QK_WIDTH), const),
                 pl.BlockSpec((1, RET_V_WIDTH), const)]
    args += [w, tabs["kdec_f"], tabs["kdec_b"], tabs["cdec_b"]]
    for width in (ATTN_Q_WIDTH, 2 * ATTN_KV_WIDTH, RET_WIDTH, 2 * D_MODEL):
        out_shape.append(jax.ShapeDtypeStruct((t, width), BF16))
        out_specs.append(pl.BlockSpec((tm, width), row))
    for _ in range(2):
        out_shape.append(jax.ShapeDtypeStruct((t // BLK, RET_KEY_DIM, RET_V_WIDTH), F32))
        out_specs.append(pl.BlockSpec((tm // BLK, RET_KEY_DIM, RET_V_WIDTH), lambda i: (tiles - 1 - i, 0, 0)))
    return pl.pallas_call(
        functools.partial(_in_proj_kernel, ln is not None, seq // tm),
        grid=(tiles,),
        in_specs=in_specs,
        out_specs=out_specs,
        out_shape=out_shape,
        scratch_shapes=[pltpu.VMEM((RET_KEY_DIM, RET_V_WIDTH), F32)],
        compiler_params=_params("arbitrary"),
        name="in_proj",
    )(*args)


def _split_bf16(y):
    hi = y.astype(BF16)
    lo = (y - hi.astype(F32)).astype(BF16)
    return hi, lo


def _mixer_body(n, nb, sink_ref, qa_ref, kvp_ref, kvc_ref, kvn_ref, ret_ref, pf_ref, pb_ref,
                bias_ref, df_ref, db_ref, qdf_ref, qdb_ref, gavg_ref, ya_ref, r_ref):
    q = qa_ref[...]
    kv = jnp.concatenate([kvp_ref[...], kvc_ref[...], kvn_ref[...]], axis=0).astype(F32)
    lo_half = lax.broadcasted_iota(jnp.int32, (kv.shape[0], 2 * HEAD_DIM), 1) < HEAD_DIM

    def lane_variants(a):
        swapped = pltpu.roll(a, HEAD_DIM, 1)
        z = jnp.zeros_like(a)
        return (((jnp.where(lo_half, a, z)).astype(BF16), (jnp.where(lo_half, z, swapped)).astype(BF16)),
                ((jnp.where(lo_half, swapped, z)).astype(BF16), (jnp.where(lo_half, z, a)).astype(BF16)))

    k_var = lane_variants(kv[:, :ATTN_KV_WIDTH])
    v_var = lane_variants(kv[:, ATTN_KV_WIDTH:])
    top_rows = lax.broadcasted_iota(jnp.int32, (2 * BLK, 1), 0) < BLK
    out_lo_half = lax.broadcasted_iota(jnp.int32, (2 * BLK, 2 * HEAD_DIM), 1) < HEAD_DIM
    half_heads = N_RET_HEADS // 2
    k_mask = _head_block_mask(N_RET_HEADS * BLK, RET_QK_WIDTH, BLK, RET_KEY_DIM)
    v_mask = _head_block_mask(half_heads * BLK, half_heads * RET_VAL_DIM, BLK, RET_VAL_DIM)
    s_mask = _head_block_mask(RET_QK_WIDTH, RET_V_WIDTH, RET_KEY_DIM, RET_VAL_DIM)
    pair = 2 * HEAD_DIM
    ys = []
    for j in range(MIX_CHUNKS):
        c = n * MIX_CHUNKS + j
        var = jnp.where(c == 0, 0, jnp.where(c == nb - 1, 2, 1))
        tok = slice(j * BLK, (j + 1) * BLK)
        keys = slice(j * BLK, (j + 3) * BLK)

        for g in range(N_KV_HEADS):
            q2 = jnp.concatenate([q[tok, pair * 2 * g:pair * (2 * g + 1)],
                                  q[tok, pair * (2 * g + 1):pair * (2 * g + 2)]], axis=0)
            s_all = _dot_nt(q2, jnp.concatenate([k_var[g][0][keys], k_var[g][1][keys]], axis=0))
            probs, denoms = [], []
            for r in range(2):
                sink = jnp.where(top_rows, sink_ref[4 * g + r], sink_ref[4 * g + 2 + r])
                s = s_all[:, 3 * BLK * r:3 * BLK * (r + 1)] + bias_ref[var, g, r]
                m = jnp.maximum(jnp.max(s, axis=1, keepdims=True), sink)
                e = jnp.exp2(s - m)
                denoms.append(jnp.sum(e, axis=1, keepdims=True) + jnp.exp2(sink - m))
                probs.append(e.astype(BF16))
            pv = _dot(jnp.concatenate(probs, axis=1),
                      jnp.concatenate([v_var[g][0][keys], v_var[g][1][keys]], axis=0))
            out = pv / jnp.where(out_lo_half, denoms[0], denoms[1])
            ya_ref[tok, pair * 2 * g:pair * (2 * g + 1)] = out[:BLK].astype(BF16)
            ya_ref[tok, pair * (2 * g + 1):pair * (2 * g + 2)] = out[BLK:].astype(BF16)

        rq = ret_ref[tok, :RET_QK_WIDTH]
        rk = ret_ref[tok, RET_QK_WIDTH:2 * RET_QK_WIDTH]
        rv = ret_ref[tok, 2 * RET_QK_WIDTH:2 * RET_QK_WIDTH + RET_V_WIDTH]
        k_exp = jnp.where(k_mask, jnp.concatenate([rk] * N_RET_HEADS, axis=0), jnp.zeros((), BF16))
        qk = _dot_nt(rq, k_exp)
        p_fb = jnp.concatenate([(qk * df_ref[...]).astype(BF16), (qk * db_ref[...]).astype(BF16)], axis=0)
        intra = []
        for g in range(2):
            cols = slice(g * half_heads * RET_VAL_DIM, (g + 1) * half_heads * RET_VAL_DIM)
            v_bd = jnp.where(v_mask, jnp.concatenate([rv[:, cols]] * half_heads, axis=0), jnp.zeros((), BF16))
            intra.append(_dot(p_fb[:, g * half_heads * BLK:(g + 1) * half_heads * BLK], v_bd))
        cross = []
        for state_ref, qdec_ref in ((pf_ref, qdf_ref), (pb_ref, qdb_ref)):
            s_bd = jnp.where(s_mask, jnp.concatenate([state_ref[j]] * N_RET_HEADS, axis=0), 0.0)
            cross.append(_dot(rq, s_bd.astype(BF16)) * qdec_ref[...])
        ys.append(jnp.concatenate(intra, axis=1) + jnp.concatenate(cross, axis=0))

    y = jnp.concatenate(ys, axis=0)
    d = y - _dot(y.astype(BF16), gavg_ref[...])
    normed = d * lax.rsqrt(_dot((d * d).astype(BF16), gavg_ref[...]) + GN_EPS)
    for j in range(MIX_CHUNKS):
        tok = slice(j * BLK, (j + 1) * BLK)
        g_f = ret_ref[tok, 2 * RET_QK_WIDTH + RET_V_WIDTH:2 * RET_QK_WIDTH + 2 * RET_V_WIDTH].astype(F32)
        g_b = ret_ref[tok, 2 * RET_QK_WIDTH + 2 * RET_V_WIDTH:].astype(F32)
        r = (g_f * _sigmoid(g_f) * normed[2 * j * BLK:(2 * j + 1) * BLK]
             + g_b * _sigmoid(g_b) * normed[(2 * j + 1) * BLK:(2 * j + 2) * BLK])
        r_ref[tok, :] = r.astype(BF16)


def _mix_proj_kernel(nb, n_blocks, sink_ref, qa_ref, kvp_ref, kvc_ref, kvn_ref, ret_ref, kvf_ref, pb_ref,
                     bias_ref, df_ref, db_ref, qdf_ref, qdb_ref, gavg_ref, cdf_ref,
                     h_ref, gl_ref, bg_ref, woa_ref, wor_ref, wout_ref, g_ref, b_ref,
                     x_ref, ya_scr, r_scr, sf_ref, pf_ref):
    s = pl.program_id(0)

    n = lax.rem(jnp.minimum(s, n_blocks - 1), nb // MIX_CHUNKS)

    @pl.when(s == 0)
    def _():
        ya_scr[...] = jnp.zeros_like(ya_scr)
        r_scr[...] = jnp.zeros_like(r_scr)

    @pl.when(n == 0)
    def _():
        sf_ref[...] = jnp.zeros_like(sf_ref)

    slot = lax.rem(s, 2)
    ya_prev = ya_scr[1 - slot]
    r_prev = r_scr[1 - slot]
    t = jnp.tanh(gl_ref[...].astype(F32) + bg_ref[...])
    pa = _dot(ya_prev, woa_ref[...])
    pr = _dot(r_prev, wor_ref[...])
    merged = pa + pr + t[:, :D_MODEL] * pa + t[:, D_MODEL:] * pr
    mix = _dot(merged.astype(BF16), wout_ref[...])
    x_ref[...] = _layer_norm(DEEPNORM_ALPHA * h_ref[...] + mix, g_ref[...], b_ref[...])

    state = sf_ref[...]
    for j in range(MIX_CHUNKS):
        pf_ref[j] = state
        state = state * cdf_ref[...] + kvf_ref[j]
    sf_ref[...] = state

    _mixer_body(n, nb, sink_ref, qa_ref, kvp_ref, kvc_ref, kvn_ref, ret_ref, pf_ref, pb_ref,
                bias_ref, df_ref, db_ref, qdf_ref, qdb_ref, gavg_ref, ya_scr.at[slot], r_scr.at[slot])


def _mix_proj(qa, kva, ret, kvf, prev_b, sink, tabs, h, gl, b_gate, woa, wor, wout, g, b, batch, nb):
    t = qa.shape[0]
    steps = nb // MIX_CHUNKS
    rows = MIX_CHUNKS * BLK
    n_blocks = batch * steps

    def mixed(s):
        return jnp.minimum(s, n_blocks - 1)

    def projected(s):
        return jnp.maximum(s - 1, 0)

    cur = lambda s: (mixed(s), 0)
    prv = lambda s: (mixed(s) * MIX_CHUNKS - jnp.where(lax.rem(mixed(s), steps) == 0, 0, 1), 0)
    nxt = lambda s: (mixed(s) * MIX_CHUNKS + MIX_CHUNKS - jnp.where(lax.rem(mixed(s), steps) == steps - 1, 1, 0), 0)
    out = lambda s: (projected(s), 0)
    c2 = lambda s: (0, 0)
    state_spec = pl.BlockSpec((MIX_CHUNKS, RET_KEY_DIM, RET_V_WIDTH), lambda s: (mixed(s), 0, 0))
    return pl.pallas_call(
        functools.partial(_mix_proj_kernel, nb, n_blocks),
        grid=(n_blocks + 1,),
        in_specs=[
            pl.BlockSpec(memory_space=pltpu.SMEM),
            pl.BlockSpec((rows, ATTN_Q_WIDTH), cur),
            pl.BlockSpec((BLK, 2 * ATTN_KV_WIDTH), prv),
            pl.BlockSpec((rows, 2 * ATTN_KV_WIDTH), cur),
            pl.BlockSpec((BLK, 2 * ATTN_KV_WIDTH), nxt),
            pl.BlockSpec((rows, RET_WIDTH), cur),
            state_spec,
            state_spec,
            pl.BlockSpec((3, N_KV_HEADS, 2, 2 * BLK, 3 * BLK), lambda s: (0, 0, 0, 0, 0)),
            pl.BlockSpec((BLK, N_RET_HEADS * BLK), c2),
            pl.BlockSpec((BLK, N_RET_HEADS * BLK), c2),
            pl.BlockSpec((BLK, RET_V_WIDTH), c2),
            pl.BlockSpec((BLK, RET_V_WIDTH), c2),
            pl.BlockSpec((RET_V_WIDTH, RET_V_WIDTH), c2),
            pl.BlockSpec((1, RET_V_WIDTH), c2),
            pl.BlockSpec((rows, D_MODEL), out),
            pl.BlockSpec((rows, 2 * D_MODEL), out),
            pl.BlockSpec((1, 2 * D_MODEL), c2),
            pl.BlockSpec((ATTN_Q_WIDTH, D_MODEL), c2),
            pl.BlockSpec((RET_V_WIDTH, D_MODEL), c2),
            pl.BlockSpec((D_MODEL, D_MODEL), c2),
            pl.BlockSpec((1, D_MODEL), c2),
            pl.BlockSpec((1, D_MODEL), c2),
        ],
        out_specs=pl.BlockSpec((rows, D_MODEL), out),
        out_shape=jax.ShapeDtypeStruct((t, D_MODEL), F32),
        scratch_shapes=[pltpu.VMEM((2, rows, ATTN_Q_WIDTH), BF16), pltpu.VMEM((2, rows, RET_V_WIDTH), BF16),
                        pltpu.VMEM((RET_KEY_DIM, RET_V_WIDTH), F32),
                        pltpu.VMEM((MIX_CHUNKS, RET_KEY_DIM, RET_V_WIDTH), F32)],
        compiler_params=_params("arbitrary"),
        name="mix_proj",
    )(sink, qa, kva, kva, kva, ret, kvf, prev_b, tabs["attn_bias"], tabs["intra_f"],
      tabs["intra_b"], tabs["qdec_f"], tabs["qdec_b"], tabs["group_avg"], tabs["cdec_f"],
      h, gl, b_gate.reshape(1, -1), woa, wor, wout, g.reshape(1, -1), b.reshape(1, -1))


def _ffn_kernel(x_ref, w1_ref, w3_ref, w2_ref, g_ref, b_ref, o_ref):
    x = x_ref[...]
    xb = x.astype(BF16)
    acc = None
    for c in range(D_FF_DENSE // FF_CHUNK):
        cols = slice(c * FF_CHUNK, (c + 1) * FF_CHUNK)
        a = _dot(xb, w1_ref[:, cols])
        hid = a * _sigmoid(a) * _dot(xb, w3_ref[:, cols])
        part = _dot(hid.astype(BF16), w2_ref[cols, :])
        acc = part if acc is None else acc + part
    o_ref[...] = _layer_norm(DEEPNORM_ALPHA * x + acc, g_ref[...], b_ref[...])


def _ffn(x, w1, w3, w2, g, b):
    t = x.shape[0]
    tm = FFN_TOKEN_TILE
    row = lambda i: (i, 0)
    const = lambda i: (0, 0)
    return pl.pallas_call(
        _ffn_kernel,
        grid=(t // tm,),
        in_specs=[
            pl.BlockSpec((tm, D_MODEL), row),
            pl.BlockSpec((D_MODEL, D_FF_DENSE), const),
            pl.BlockSpec((D_MODEL, D_FF_DENSE), const),
            pl.BlockSpec((D_FF_DENSE, D_MODEL), const),
            pl.BlockSpec((1, D_MODEL), const),
            pl.BlockSpec((1, D_MODEL), const),
        ],
        out_specs=pl.BlockSpec((tm, D_MODEL), row),
        out_shape=jax.ShapeDtypeStruct((t, D_MODEL), F32),
        compiler_params=_params("parallel"),
        name="ffn",
    )(x, w1, w3, w2, g.reshape(1, -1), b.reshape(1, -1))


def _route_kernel(x_ref, rw_ref, rb_ref, tri_ref, xb_ref, tcol_ref, pcol_ref, drow_ref, cnt_ref):
    x = x_ref[...]
    tm, lanes = x.shape[0], LANES
    lane = lax.broadcasted_iota(jnp.int32, (tm, lanes), 1)
    xh, xl = _split_bf16(x)
    wh, wl = _split_bf16(rw_ref[...])
    hi_part = _dot(xh, jnp.concatenate([wh, wl], axis=1))
    logits = hi_part[:, :lanes] + hi_part[:, lanes:] + _dot(xl, wh) + rb_ref[...]
    logits = jnp.where(lane < N_EXPERTS, logits, NEG_INF)
    m1 = jnp.max(logits, axis=1, keepdims=True)
    i1 = jnp.min(jnp.where(logits == m1, lane, lanes), axis=1, keepdims=True)
    rest = jnp.where(lane == i1, NEG_INF, logits)
    m2 = jnp.max(rest, axis=1, keepdims=True)
    i2 = jnp.min(jnp.where(rest == m2, lane, lanes), axis=1, keepdims=True)
    e2 = jnp.exp(m2 - m1)
    w_top = 1.0 / (1.0 + e2)
    gate =jnp.where(lane == i1, w_top, 0.0) + jnp.where(lane == i2, e2 * w_top, 0.0)
    sel = jnp.where(lane == i1, 1.0, jnp.where(lane == i2, 1.0, 0.0))
    prefix = _dot(tri_ref[...], sel.astype(BF16))
    slot1 = jnp.where(sel > 0.0, prefix, 0.0)
    hi = jnp.floor(slot1 * (1.0 / SLOT_RADIX))
    tcol_ref[:, :lanes] = gate.astype(BF16)
    tcol_ref[:, lanes:2 * lanes] = hi.astype(BF16)
    tcol_ref[:, 2 * lanes:] = (slot1 - SLOT_RADIX * hi).astype(BF16)

    def first_slab_pos(idx, picked):
        rank = jnp.sum(jnp.where(picked, prefix, 0.0), axis=1, keepdims=True) - 1.0
        return jnp.where(rank < MOE_SLAB, idx.astype(F32) * MOE_SLAB + rank, -1.0)

    pcol_ref[...] = jnp.where(
        lane == 0, first_slab_pos(i1, lane == i1), jnp.where(
            lane == 1, w_top, jnp.where(
                lane == 2, first_slab_pos(i2, lane == i2), jnp.where(lane == 3, e2 * w_top, 0.0))))
    drow_ref[...] = (slot1 - 1.0).T[:N_EXPERTS]
    cnt_ref[...] = prefix[tm - 1:tm].astype(jnp.int32)
    xb_ref[...] = x.astype(BF16)


def _route(x, router_w, router_b):
    t = x.shape[0]
    tm = MOE_TOKEN_TILE
    lanes = 128
    rw = jnp.zeros((D_MODEL, lanes), F32).at[:, :N_EXPERTS].set(router_w)
    rb = jnp.zeros((1, lanes), F32).at[0, :N_EXPERTS].set(router_b)
    tri = jnp.asarray(np.arange(tm)[:, None] >= np.arange(tm)[None, :], BF16)
    row = lambda i: (i, 0)
    const = lambda i: (0, 0)
    return pl.pallas_call(
        _route_kernel,
        grid=(t // tm,),
        in_specs=[
            pl.BlockSpec((tm, D_MODEL), row),
            pl.BlockSpec((D_MODEL, lanes), const),
            pl.BlockSpec((1, lanes), const),
            pl.BlockSpec((tm, tm), const),
        ],
        out_specs=[
            pl.BlockSpec((tm, D_MODEL), row),
            pl.BlockSpec((tm, 3 * lanes), row),
            pl.BlockSpec((tm, lanes), row),
            pl.BlockSpec((N_EXPERTS, tm), row),
            pl.BlockSpec((None, 1, lanes), lambda i: (i, 0, 0)),
        ],
        out_shape=[
            jax.ShapeDtypeStruct((t, D_MODEL), BF16),
            jax.ShapeDtypeStruct((t, 3 * lanes), BF16),
            jax.ShapeDtypeStruct((t, lanes), F32),
            jax.ShapeDtypeStruct((t // tm * N_EXPERTS, tm), F32),
            jax.ShapeDtypeStruct((t // tm, 1, lanes), jnp.int32),
        ],
        compiler_params=_params("parallel"),
        name="route",
    )(x, rw, rb, tri)


def _moe_kernel(cnt_ref, x_ref, xb_ref, tcol_ref, pcol_ref, drow_ref, w1_ref, w3_ref, w2_ref,
                g_ref, b_ref, o_ref, y_ref):
    i = pl.program_id(0)
    e = pl.program_id(1)

    @pl.when(e == 0)
    def _():
        o_ref[...] = jnp.zeros_like(o_ref)

    slot_row = drow_ref[pl.ds(e, 1), :]
    rows = lax.broadcasted_iota(jnp.int32, (MOE_SLAB, 1), 0).astype(F32)

    def expert_ffn(base):
        pick = jnp.where(slot_row == rows + base, 1.0, 0.0).astype(BF16)
        xs = _dot(pick, xb_ref[...]).astype(BF16)
        a = _dot(xs, w1_ref[...])
        hid = a * _sigmoid(a) * _dot(xs, w3_ref[...])
        return _dot(hid.astype(BF16), w2_ref[...]).astype(BF16)

    y_ref[pl.ds(pl.multiple_of(e * MOE_SLAB, 16), MOE_SLAB), :] = expert_ffn(0.0)

    def later_slab(s, carry):
        base = (s * MOE_SLAB).astype(F32)
        pick_e = jnp.where(lax.broadcasted_iota(jnp.int32, (LANES, LANES), 0) == e, 1.0, 0.0).astype(BF16)
        gate_b = _dot(tcol_ref[:, :LANES], pick_e)
        slot_b = (SLOT_RADIX * _dot(tcol_ref[:, LANES:2 * LANES], pick_e)
                  + _dot(tcol_ref[:, 2 * LANES:], pick_e) - 1.0)
        cols = lax.broadcasted_iota(jnp.int32, (1, LANES), 1).astype(F32)
        y = jnp.concatenate([expert_ffn(base), jnp.zeros((MOE_SLAB_K - MOE_SLAB, D_MODEL), BF16)], axis=0)
        put = jnp.concatenate(
            [jnp.where(slot_b == cols + (base + k * LANES), gate_b, 0.0).astype(BF16)
             for k in range(MOE_SLAB_K // LANES)], axis=1)
        o_ref[...] += _dot(put, y)
        return carry

    n_slabs = (cnt_ref[i * N_EXPERTS + e] + MOE_SLAB - 1) // MOE_SLAB
    lax.fori_loop(1, n_slabs, later_slab, 0)

    @pl.when(e == N_EXPERTS - 1)
    def _():
        p = pcol_ref[...]
        pos_a, gate_a, pos_b, gate_b = p[:, 0:1], p[:, 1:2], p[:, 2:3], p[:, 3:4]
        acc = DEEPNORM_ALPHA * x_ref[...] + o_ref[...]
        for k in range(N_EXPERTS * MOE_SLAB // MOE_COMBINE_K):
            cols = lax.broadcasted_iota(jnp.int32, (1, MOE_COMBINE_K), 1).astype(F32) + float(k * MOE_COMBINE_K)
            put = jnp.where(pos_a == cols, gate_a, jnp.where(pos_b == cols, gate_b, 0.0)).astype(BF16)
            acc = acc + _dot(put, y_ref[k * MOE_COMBINE_K:(k + 1) * MOE_COMBINE_K, :])
        o_ref[...] = _layer_norm(acc, g_ref[...], b_ref[...])


def _moe(x, router_w, router_b, w1, w3, w2, g, b):
    t = x.shape[0]
    tm = MOE_TOKEN_TILE
    xb, tcol, pcol, drow, cnt = _route(x, router_w, router_b)
    counts = cnt[:, 0, :N_EXPERTS].reshape(-1)
    row = lambda i, e, c: (i, 0)
    const = lambda i, e, c: (0, 0)
    expert = lambda i, e, c: (e, 0, 0)
    grid_spec = pltpu.PrefetchScalarGridSpec(
        num_scalar_prefetch=1,
        grid=(t // tm, N_EXPERTS),
        in_specs=[
            pl.BlockSpec((tm, D_MODEL), row),
            pl.BlockSpec((tm, D_MODEL), row),
            pl.BlockSpec((tm, 3 * LANES), row),
            pl.BlockSpec((tm, LANES), row),
            pl.BlockSpec((N_EXPERTS, tm), row),
            pl.BlockSpec((None, D_MODEL, D_FF_EXPERT), expert),
            pl.BlockSpec((None, D_MODEL, D_FF_EXPERT), expert),
            pl.BlockSpec((None, D_FF_EXPERT, D_MODEL), expert),
            pl.BlockSpec((1, D_MODEL), const),
            pl.BlockSpec((1, D_MODEL), const),
        ],
        out_specs=pl.BlockSpec((tm, D_MODEL), row),
        scratch_shapes=[pltpu.VMEM((N_EXPERTS * MOE_SLAB, D_MODEL), BF16)],
    )
    return pl.pallas_call(
        _moe_kernel,
        grid_spec=grid_spec,
        out_shape=jax.ShapeDtypeStruct((t, D_MODEL), F32),
        compiler_params=_params("parallel", "arbitrary"),
        name="moe",
    )(counts, x, xb, tcol, pcol, drow, w1, w3, w2, g.reshape(1, -1), b.reshape(1, -1))


def _attn_bias_table():
    qi = np.arange(BLK)[:, None]
    kj = np.arange(3 * BLK)[None, :]
    dist = np.abs(qi - kj + BLK)
    slopes = np.exp2(-8.0 * np.arange(1, N_Q_HEADS + 1) / N_Q_HEADS)
    bias = -(LOG2E * slopes)[:, None, None] * dist[None]
    in_window = (dist <= WINDOW)[None]
    variants = []
    for ok in (kj >= BLK, kj >= 0, kj < 2 * BLK):
        per_head = np.where(in_window & ok[None], bias, NEG_INF)
        per_head = per_head.reshape(N_KV_HEADS, 2, 2, BLK, 3 * BLK)
        variants.append(np.transpose(per_head, (0, 2, 1, 3, 4)).reshape(N_KV_HEADS, 2, 2 * BLK, 3 * BLK))
    return np.stack(variants).astype(np.float32)


def _retention_tables(decay_fwd, decay_bwd):
    pos = jnp.arange(BLK, dtype=F32)
    diff = jnp.arange(BLK)[:, None] - jnp.arange(BLK)[None, :]
    lg_f = jax.nn.log_sigmoid(decay_fwd.astype(F32))
    lg_b = jax.nn.log_sigmoid(decay_bwd.astype(F32))

    def intra(lg, d, mask):
        dec = jnp.where(mask[None], jnp.exp(jnp.maximum(d, 0).astype(F32)[None] * lg[:, None, None]), 0.0)
        return jnp.transpose(dec, (1, 0, 2)).reshape(BLK, N_RET_HEADS * BLK)

    def per_pos(lg, expo, width):
        return jnp.repeat(jnp.exp(expo[:, None] * lg[None, :]), width, axis=1)

    return {
        "intra_f": intra(lg_f, diff, diff >= 0),
        "intra_b": intra(lg_b, -diff, diff < 0),
        "qdec_f": per_pos(lg_f, pos + 1.0, RET_VAL_DIM),
        "qdec_b": per_pos(lg_b, BLK - pos, RET_VAL_DIM),
        "kdec_f": per_pos(lg_f, BLK - 1.0 - pos, RET_KEY_DIM),
        "kdec_b": per_pos(lg_b, pos, RET_KEY_DIM),
        "cdec_f": jnp.repeat(jnp.exp(BLK * lg_f), RET_VAL_DIM)[None, :],
        "cdec_b": jnp.repeat(jnp.exp(BLK * lg_b), RET_VAL_DIM)[None, :],
    }


def _group_avg_matrix():
    head = np.arange(RET_V_WIDTH) // RET_VAL_DIM
    return jnp.asarray(np.where(head[:, None] == head[None, :], 1.0 / RET_VAL_DIM, 0.0), BF16)


def kernel(x, ln_emb_g, ln_emb_b, w_in, b_gate, sink_logits, decay_fwd, decay_bwd,
           w_o_attn, w_o_ret, w_out, ln1_g, ln1_b, ffn_w1, ffn_w3, ffn_w2,
           router_w, router_b, moe_w1, moe_w3, moe_w2, ln2_g, ln2_b):
    batch, seq, _ = x.shape
    nb = seq // BLK
    xt = x.reshape(batch * seq, D_MODEL)
    attn_bias = _attn_bias_table()
    group_avg = _group_avg_matrix()
    for layer in range(DEPTH):
        tabs = _retention_tables(decay_fwd[layer], decay_bwd[layer])
        tabs["attn_bias"] = attn_bias
        tabs["group_avg"] = group_avg
        w = w_in[layer].astype(BF16)
        if layer == 0:
            h, qa, kva, ret, gl, kvf, prev_b = _in_proj(xt, w, tabs, seq, ln=(ln_emb_g, ln_emb_b))
        else:
            h = xt
            qa, kva, ret, gl, kvf, prev_b = _in_proj(xt, w, tabs, seq)
        x1 = _mix_proj(qa, kva, ret, kvf, prev_b, LOG2E * sink_logits[layer].astype(F32), tabs,
                       h, gl, 0.5 * b_gate[layer], w_o_attn[layer].astype(BF16),
                       w_o_ret[layer].astype(BF16), (0.5 * w_out[layer]).astype(BF16),
                       ln1_g[layer], ln1_b[layer], batch, nb)
        i = layer // 2
        if layer % 2 == 0:
            xt = _ffn(x1, ffn_w1[i].astype(BF16), ffn_w3[i].astype(BF16), ffn_w2[i].astype(BF16),
                      ln2_g[layer], ln2_b[layer])
        else:
            xt = _moe(x1, router_w[i], router_b[i], moe_w1[i].astype(BF16), moe_w3[i].astype(BF16),
                      moe_w2[i].astype(BF16), ln2_g[layer], ln2_b[layer])
    return xt.reshape(batch, seq, D_MODEL)
```

```python
import functools

import jax
import jax.numpy as jnp
import numpy as np
from jax import lax
from jax.experimental import pallas as pl
from jax.experimental.pallas import tpu as pltpu

F32 = jnp.float32
BF16 = jnp.bfloat16

D_MODEL = 1024
DEPTH = 2
N_Q_HEADS = 8
N_KV_HEADS = 2
HEAD_DIM = 64
WINDOW = 128
BLK = 128
ATTN_Q_WIDTH = N_Q_HEADS * HEAD_DIM
ATTN_KV_WIDTH = N_KV_HEADS * HEAD_DIM
N_RET_HEADS = 8
RET_KEY_DIM = 32
RET_VAL_DIM = 64
RET_QK_WIDTH = N_RET_HEADS * RET_KEY_DIM
RET_V_WIDTH = N_RET_HEADS * RET_VAL_DIM
RET_WIDTH = 2 * RET_QK_WIDTH + 3 * RET_V_WIDTH
D_FF_DENSE = 2816
N_EXPERTS = 8
D_FF_EXPERT = 1408
IN_WIDTH = ATTN_Q_WIDTH + 2 * ATTN_KV_WIDTH + RET_WIDTH + 2 * D_MODEL
DEEPNORM_ALPHA = (2 * DEPTH) ** 0.25
LN_EPS = 1e-5
GN_EPS = 1e-5
NEG_INF = -1e30
LOG2E = 1.4426950408889634

TOKEN_TILE = 1024
FFN_TOKEN_TILE = 1024
MOE_TOKEN_TILE = 1024
MOE_SLAB = 288
MOE_SLAB_SMALL = 256
MOE_SLAB_K = 384
MOE_COMBINE_K = 256
LANES = 128
SLOT_RADIX = 32.0
MIX_CHUNKS = 4
FF_CHUNK = 256
VMEM_LIMIT_BYTES = 56 * 1024 * 1024


def _params(*sem):
    return pltpu.CompilerParams(dimension_semantics=sem, vmem_limit_bytes=VMEM_LIMIT_BYTES)


def _layer_norm(x, g, b):
    mu = jnp.mean(x, axis=-1, keepdims=True)
    d = x - mu
    var = jnp.mean(d * d, axis=-1, keepdims=True)
    return d * lax.rsqrt(var + LN_EPS) * g + b


def _sigmoid(x):
    return 1.0 / (1.0 + jnp.exp(-x))


def _dot(a, b):
    return jnp.dot(a, b, preferred_element_type=F32)


def _dot_nt(a, b):
    return lax.dot_general(a, b, (((1,), (1,)), ((), ())), preferred_element_type=F32)


def _head_block_mask(rows, cols, row_div, col_div):
    r = lax.broadcasted_iota(jnp.int32, (rows, cols), 0) // row_div
    c = lax.broadcasted_iota(jnp.int32, (rows, cols), 1) // col_div
    return r == c


def _in_proj_kernel(apply_ln, tiles_per_seq, *refs):
    sb_ref = refs[-1]
    tile = pl.num_programs(0) - 1 - pl.program_id(0)

    @pl.when(lax.rem(tile, tiles_per_seq) == tiles_per_seq - 1)
    def _():
        sb_ref[...] = jnp.zeros_like(sb_ref)

    if apply_ln:
        (x_ref, g_ref, b_ref, w_ref, kdf_ref, kdb_ref, cdb_ref,
         h_ref, qa_ref, kva_ref, ret_ref, gl_ref, kvf_ref, pb_ref, _) = refs
        x = _layer_norm(x_ref[...], g_ref[...], b_ref[...])
        h_ref[...] = x
    else:
        (x_ref, w_ref, kdf_ref, kdb_ref, cdb_ref,
         qa_ref, kva_ref, ret_ref, gl_ref, kvf_ref, pb_ref, _) = refs
        x = x_ref[...]
    xb = x.astype(BF16)

    def proj(lo, hi):
        return _dot(xb, w_ref[:, lo:hi])

    c0 = ATTN_Q_WIDTH
    c1 = c0 + 2 * ATTN_KV_WIDTH
    qa_ref[...] = (proj(0, c0) * (HEAD_DIM ** -0.5 * LOG2E)).astype(BF16)
    kva_ref[...] = proj(c0, c1).astype(BF16)
    ret_ref[:, :RET_QK_WIDTH] = proj(c1, c1 + RET_QK_WIDTH).astype(BF16)
    rk = (proj(c1 + RET_QK_WIDTH, c1 + 2 * RET_QK_WIDTH) * (RET_KEY_DIM ** -0.5)).astype(BF16)
    ret_ref[:, RET_QK_WIDTH:2 * RET_QK_WIDTH] = rk
    rest = proj(c1 + 2 * RET_QK_WIDTH, c1 + RET_WIDTH).astype(BF16)
    ret_ref[:, 2 * RET_QK_WIDTH:] = rest
    gl_ref[...] = (0.5 * proj(c1 + RET_WIDTH, IN_WIDTH)).astype(BF16)

    half = N_RET_HEADS // 2
    mask = _head_block_mask(half * RET_KEY_DIM, half * RET_VAL_DIM, RET_KEY_DIM, RET_VAL_DIM)
    def increment(c, kd_ref):
        rows = slice(c * BLK, (c + 1) * BLK)
        kz = (rk[rows].astype(F32) * kd_ref[...]).T.astype(BF16)
        v = rest[rows, :RET_V_WIDTH]
        comps = []
        for g in range(2):
            full = jnp.where(mask, _dot(kz[g * half * RET_KEY_DIM:(g + 1) * half * RET_KEY_DIM],
                                        v[:, g * half * RET_VAL_DIM:(g + 1) * half * RET_VAL_DIM]), 0.0)
            comp = full[0:RET_KEY_DIM]
            for h in range(1, half):
                comp = comp + full[h * RET_KEY_DIM:(h + 1) * RET_KEY_DIM]
            comps.append(comp)
        return jnp.concatenate(comps, axis=1)

    state = sb_ref[...]
    for c in reversed(range(x.shape[0] // BLK)):
        kvf_ref[c] = increment(c, kdf_ref)
        pb_ref[c] = state
        state = state * cdb_ref[...] + increment(c, kdb_ref)
    sb_ref[...] = state


def _in_proj(x, w, tabs, seq, ln=None):
    t = x.shape[0]
    tm = TOKEN_TILE
    tiles = t // tm
    row = lambda i: (tiles - 1 - i, 0)
    const = lambda i: (0, 0)
    in_specs = [pl.BlockSpec((tm, D_MODEL), row)]
    args = [x]
    out_shape = []
    out_specs = []
    if ln is not None:
        in_specs += [pl.BlockSpec((1, D_MODEL), const), pl.BlockSpec((1, D_MODEL), const)]
        args += [ln[0].reshape(1, D_MODEL), ln[1].reshape(1, D_MODEL)]
        out_shape.append(jax.ShapeDtypeStruct((t, D_MODEL), F32))
        out_specs.append(pl.BlockSpec((tm, D_MODEL), row))
    in_specs += [pl.BlockSpec((D_MODEL, IN_WIDTH), const),
                 pl.BlockSpec((BLK, RET_QK_WIDTH), const), pl.BlockSpec((BLK, RET_QK_WIDTH), const),
                 pl.BlockSpec((1, RET_V_WIDTH), const)]
    args += [w, tabs["kdec_f"], tabs["kdec_b"], tabs["cdec_b"]]
    for width in (ATTN_Q_WIDTH, 2 * ATTN_KV_WIDTH, RET_WIDTH, 2 * D_MODEL):
        out_shape.append(jax.ShapeDtypeStruct((t, width), BF16))
        out_specs.append(pl.BlockSpec((tm, width), row))
    for _ in range(2):
        out_shape.append(jax.ShapeDtypeStruct((t // BLK, RET_KEY_DIM, RET_V_WIDTH), F32))
        out_specs.append(pl.BlockSpec((tm // BLK, RET_KEY_DIM, RET_V_WIDTH), lambda i: (tiles - 1 - i, 0, 0)))
    return pl.pallas_call(
        functools.partial(_in_proj_kernel, ln is not None, seq // tm),
        grid=(tiles,),
        in_specs=in_specs,
        out_specs=out_specs,
        out_shape=out_shape,
        scratch_shapes=[pltpu.VMEM((RET_KEY_DIM, RET_V_WIDTH), F32)],
        compiler_params=_params("arbitrary"),
        name="in_proj",
    )(*args)


def _split_bf16(y):
    hi = y.astype(BF16)
    lo = (y - hi.astype(F32)).astype(BF16)
    return hi, lo


def _mixer_body(n, nb, sink_ref, qa_ref, kvp_ref, kvc_ref, kvn_ref, ret_ref, pf_ref, pb_ref,
                bias_ref, df_ref, db_ref, qdf_ref, qdb_ref, gavg_ref, ya_ref, r_ref):
    q = qa_ref[...]
    kv = jnp.concatenate([kvp_ref[...], kvc_ref[...], kvn_ref[...]], axis=0).astype(F32)
    lo_half = lax.broadcasted_iota(jnp.int32, (kv.shape[0], 2 * HEAD_DIM), 1) < HEAD_DIM

    def lane_variants(a):
        swapped = pltpu.roll(a, HEAD_DIM, 1)
        z = jnp.zeros_like(a)
        return (((jnp.where(lo_half, a, z)).astype(BF16), (jnp.where(lo_half, z, swapped)).astype(BF16)),
                ((jnp.where(lo_half, swapped, z)).astype(BF16), (jnp.where(lo_half, z, a)).astype(BF16)))

    k_var = lane_variants(kv[:, :ATTN_KV_WIDTH])
    v_var = lane_variants(kv[:, ATTN_KV_WIDTH:])
    top_rows = lax.broadcasted_iota(jnp.int32, (2 * BLK, 1), 0) < BLK
    out_lo_half = lax.broadcasted_iota(jnp.int32, (2 * BLK, 2 * HEAD_DIM), 1) < HEAD_DIM
    half_heads = N_RET_HEADS // 2
    k_mask = _head_block_mask(N_RET_HEADS * BLK, RET_QK_WIDTH, BLK, RET_KEY_DIM)
    v_mask = _head_block_mask(half_heads * BLK, half_heads * RET_VAL_DIM, BLK, RET_VAL_DIM)
    s_mask = _head_block_mask(RET_QK_WIDTH, RET_V_WIDTH, RET_KEY_DIM, RET_VAL_DIM)
    pair = 2 * HEAD_DIM
    ys = []
    for j in range(MIX_CHUNKS):
        c = n * MIX_CHUNKS + j
        var = jnp.where(c == 0, 0, jnp.where(c == nb - 1, 2, 1))
        tok = slice(j * BLK, (j + 1) * BLK)
        keys = slice(j * BLK, (j + 3) * BLK)

        for g in range(N_KV_HEADS):
            q2 = jnp.concatenate([q[tok, pair * 2 * g:pair * (2 * g + 1)],
                                  q[tok, pair * (2 * g + 1):pair * (2 * g + 2)]], axis=0)
            s_all = _dot_nt(q2, jnp.concatenate([k_var[g][0][keys], k_var[g][1][keys]], axis=0))
            probs, denoms = [], []
            for r in range(2):
                sink = jnp.where(top_rows, sink_ref[4 * g + r], sink_ref[4 * g + 2 + r])
                s = s_all[:, 3 * BLK * r:3 * BLK * (r + 1)] + bias_ref[var, g, r]
                m = jnp.maximum(jnp.max(s, axis=1, keepdims=True), sink)
                e = jnp.exp2(s - m)
                denoms.append(jnp.sum(e, axis=1, keepdims=True) + jnp.exp2(sink - m))
                probs.append(e.astype(BF16))
            pv = _dot(jnp.concatenate(probs, axis=1),
                      jnp.concatenate([v_var[g][0][keys], v_var[g][1][keys]], axis=0))
            out = pv / jnp.where(out_lo_half, denoms[0], denoms[1])
            ya_ref[tok, pair * 2 * g:pair * (2 * g + 1)] = out[:BLK].astype(BF16)
            ya_ref[tok, pair * (2 * g + 1):pair * (2 * g + 2)] = out[BLK:].astype(BF16)

        rq = ret_ref[tok, :RET_QK_WIDTH]
        rk = ret_ref[tok, RET_QK_WIDTH:2 * RET_QK_WIDTH]
        rv = ret_ref[tok, 2 * RET_QK_WIDTH:2 * RET_QK_WIDTH + RET_V_WIDTH]
        k_exp = jnp.where(k_mask, jnp.concatenate([rk] * N_RET_HEADS, axis=0), jnp.zeros((), BF16))
        qk = _dot_nt(rq, k_exp)
        p_fb = jnp.concatenate([(qk * df_ref[...]).astype(BF16), (qk * db_ref[...]).astype(BF16)], axis=0)
        intra = []
        for g in range(2):
            cols = slice(g * half_heads * RET_VAL_DIM, (g + 1) * half_heads * RET_VAL_DIM)
            v_bd = jnp.where(v_mask, jnp.concatenate([rv[:, cols]] * half_heads, axis=0), jnp.zeros((), BF16))
            intra.append(_dot(p_fb[:, g * half_heads * BLK:(g + 1) * half_heads * BLK], v_bd))
        cross = []
        for state_ref, qdec_ref in ((pf_ref, qdf_ref), (pb_ref, qdb_ref)):
            s_bd = jnp.where(s_mask, jnp.concatenate([state_ref[j]] * N_RET_HEADS, axis=0), 0.0)
            cross.append(_dot(rq, s_bd.astype(BF16)) * qdec_ref[...])
        ys.append(jnp.concatenate(intra, axis=1) + jnp.concatenate(cross, axis=0))

    y = jnp.concatenate(ys, axis=0)
    d = y - _dot(y.astype(BF16), gavg_ref[...])
    normed = d * lax.rsqrt(_dot((d * d).astype(BF16), gavg_ref[...]) + GN_EPS)
    for j in range(MIX_CHUNKS):
        tok = slice(j * BLK, (j + 1) * BLK)
        g_f = ret_ref[tok, 2 * RET_QK_WIDTH + RET_V_WIDTH:2 * RET_QK_WIDTH + 2 * RET_V_WIDTH].astype(F32)
        g_b = ret_ref[tok, 2 * RET_QK_WIDTH + 2 * RET_V_WIDTH:].astype(F32)
        r = (g_f * _sigmoid(g_f) * normed[2 * j * BLK:(2 * j + 1) * BLK]
             + g_b * _sigmoid(g_b) * normed[(2 * j + 1) * BLK:(2 * j + 2) * BLK])
        r_ref[tok, :] = r.astype(BF16)


def _mix_proj_kernel(nb, n_blocks, sink_ref, qa_ref, kvp_ref, kvc_ref, kvn_ref, ret_ref, kvf_ref, pb_ref,
                     bias_ref, df_ref, db_ref, qdf_ref, qdb_ref, gavg_ref, cdf_ref,
                     h_ref, gl_ref, bg_ref, woa_ref, wor_ref, wout_ref, g_ref, b_ref,
                     x_ref, ya_scr, r_scr, sf_ref, pf_ref):
    s = pl.program_id(0)

    n = lax.rem(jnp.minimum(s, n_blocks - 1), nb // MIX_CHUNKS)

    @pl.when(s == 0)
    def _():
        ya_scr[...] = jnp.zeros_like(ya_scr)
        r_scr[...] = jnp.zeros_like(r_scr)

    @pl.when(n == 0)
    def _():
        sf_ref[...] = jnp.zeros_like(sf_ref)

    slot = lax.rem(s, 2)
    ya_prev = ya_scr[1 - slot]
    r_prev = r_scr[1 - slot]
    t = jnp.tanh(gl_ref[...].astype(F32) + bg_ref[...])
    pa = _dot(ya_prev, woa_ref[...])
    pr = _dot(r_prev, wor_ref[...])
    merged = pa + pr + t[:, :D_MODEL] * pa + t[:, D_MODEL:] * pr
    mix = _dot(merged.astype(BF16), wout_ref[...])
    x_ref[...] = _layer_norm(DEEPNORM_ALPHA * h_ref[...] + mix, g_ref[...], b_ref[...])

    state = sf_ref[...]
    for j in range(MIX_CHUNKS):
        pf_ref[j] = state
        state = state * cdf_ref[...] + kvf_ref[j]
    sf_ref[...] = state

    _mixer_body(n, nb, sink_ref, qa_ref, kvp_ref, kvc_ref, kvn_ref, ret_ref, pf_ref, pb_ref,
                bias_ref, df_ref, db_ref, qdf_ref, qdb_ref, gavg_ref, ya_scr.at[slot], r_scr.at[slot])


def _mix_proj(qa, kva, ret, kvf, prev_b, sink, tabs, h, gl, b_gate, woa, wor, wout, g, b, batch, nb):
    t = qa.shape[0]
    steps = nb // MIX_CHUNKS
    rows = MIX_CHUNKS * BLK
    n_blocks = batch * steps

    def mixed(s):
        return jnp.minimum(s, n_blocks - 1)

    def projected(s):
        return jnp.maximum(s - 1, 0)

    cur = lambda s: (mixed(s), 0)
    prv = lambda s: (mixed(s) * MIX_CHUNKS - jnp.where(lax.rem(mixed(s), steps) == 0, 0, 1), 0)
    nxt = lambda s: (mixed(s) * MIX_CHUNKS + MIX_CHUNKS - jnp.where(lax.rem(mixed(s), steps) == steps - 1, 1, 0), 0)
    out = lambda s: (projected(s), 0)
    c2 = lambda s: (0, 0)
    state_spec = pl.BlockSpec((MIX_CHUNKS, RET_KEY_DIM, RET_V_WIDTH), lambda s: (mixed(s), 0, 0))
    return pl.pallas_call(
        functools.partial(_mix_proj_kernel, nb, n_blocks),
        grid=(n_blocks + 1,),
        in_specs=[
            pl.BlockSpec(memory_space=pltpu.SMEM),
            pl.BlockSpec((rows, ATTN_Q_WIDTH), cur),
            pl.BlockSpec((BLK, 2 * ATTN_KV_WIDTH), prv),
            pl.BlockSpec((rows, 2 * ATTN_KV_WIDTH), cur),
            pl.BlockSpec((BLK, 2 * ATTN_KV_WIDTH), nxt),
            pl.BlockSpec((rows, RET_WIDTH), cur),
            state_spec,
            state_spec,
            pl.BlockSpec((3, N_KV_HEADS, 2, 2 * BLK, 3 * BLK), lambda s: (0, 0, 0, 0, 0)),
            pl.BlockSpec((BLK, N_RET_HEADS * BLK), c2),
            pl.BlockSpec((BLK, N_RET_HEADS * BLK), c2),
            pl.BlockSpec((BLK, RET_V_WIDTH), c2),
            pl.BlockSpec((BLK, RET_V_WIDTH), c2),
            pl.BlockSpec((RET_V_WIDTH, RET_V_WIDTH), c2),
            pl.BlockSpec((1, RET_V_WIDTH), c2),
            pl.BlockSpec((rows, D_MODEL), out),
            pl.BlockSpec((rows, 2 * D_MODEL), out),
            pl.BlockSpec((1, 2 * D_MODEL), c2),
            pl.BlockSpec((ATTN_Q_WIDTH, D_MODEL), c2),
            pl.BlockSpec((RET_V_WIDTH, D_MODEL), c2),
            pl.BlockSpec((D_MODEL, D_MODEL), c2),
            pl.BlockSpec((1, D_MODEL), c2),
            pl.BlockSpec((1, D_MODEL), c2),
        ],
        out_specs=pl.BlockSpec((rows, D_MODEL), out),
        out_shape=jax.ShapeDtypeStruct((t, D_MODEL), F32),
        scratch_shapes=[pltpu.VMEM((2, rows, ATTN_Q_WIDTH), BF16), pltpu.VMEM((2, rows, RET_V_WIDTH), BF16),
                        pltpu.VMEM((RET_KEY_DIM, RET_V_WIDTH), F32),
                        pltpu.VMEM((MIX_CHUNKS, RET_KEY_DIM, RET_V_WIDTH), F32)],
        compiler_params=_params("arbitrary"),
        name="mix_proj",
    )(sink, qa, kva, kva, kva, ret, kvf, prev_b, tabs["attn_bias"], tabs["intra_f"],
      tabs["intra_b"], tabs["qdec_f"], tabs["qdec_b"], tabs["group_avg"], tabs["cdec_f"],
      h, gl, b_gate.reshape(1, -1), woa, wor, wout, g.reshape(1, -1), b.reshape(1, -1))


def _ffn_kernel(x_ref, w1_ref, w3_ref, w2_ref, g_ref, b_ref, o_ref):
    x = x_ref[...]
    xb = x.astype(BF16)
    acc = None
    for c in range(D_FF_DENSE // FF_CHUNK):
        cols = slice(c * FF_CHUNK, (c + 1) * FF_CHUNK)
        a = _dot(xb, w1_ref[:, cols])
        hid = a * _sigmoid(a) * _dot(xb, w3_ref[:, cols])
        part = _dot(hid.astype(BF16), w2_ref[cols, :])
        acc = part if acc is None else acc + part
    o_ref[...] = _layer_norm(DEEPNORM_ALPHA * x + acc, g_ref[...], b_ref[...])


def _ffn(x, w1, w3, w2, g, b):
    t = x.shape[0]
    tm = FFN_TOKEN_TILE
    row = lambda i: (i, 0)
    const = lambda i: (0, 0)
    return pl.pallas_call(
        _ffn_kernel,
        grid=(t // tm,),
        in_specs=[
            pl.BlockSpec((tm, D_MODEL), row),
            pl.BlockSpec((D_MODEL, D_FF_DENSE), const),
            pl.BlockSpec((D_MODEL, D_FF_DENSE), const),
            pl.BlockSpec((D_FF_DENSE, D_MODEL), const),
            pl.BlockSpec((1, D_MODEL), const),
            pl.BlockSpec((1, D_MODEL), const),
        ],
        out_specs=pl.BlockSpec((tm, D_MODEL), row),
        out_shape=jax.ShapeDtypeStruct((t, D_MODEL), F32),
        compiler_params=_params("parallel"),
        name="ffn",
    )(x, w1, w3, w2, g.reshape(1, -1), b.reshape(1, -1))


def _route_kernel(x_ref, rw_ref, rb_ref, tri_ref, xb_ref, tcol_ref, pcol_ref, drow_ref, cnt_ref):
    x = x_ref[...]
    tm, lanes = x.shape[0], LANES
    lane = lax.broadcasted_iota(jnp.int32, (tm, lanes), 1)
    xh, xl = _split_bf16(x)
    wh, wl = _split_bf16(rw_ref[...])
    hi_part = _dot(xh, jnp.concatenate([wh, wl], axis=1))
    logits = hi_part[:, :lanes] + hi_part[:, lanes:] + _dot(xl, wh) + rb_ref[...]
    logits = jnp.where(lane < N_EXPERTS, logits, NEG_INF)
    m1 = jnp.max(logits, axis=1, keepdims=True)
    i1 = jnp.min(jnp.where(logits == m1, lane, lanes), axis=1, keepdims=True)
    rest = jnp.where(lane == i1, NEG_INF, logits)
    m2 = jnp.max(rest, axis=1, keepdims=True)
    i2 = jnp.min(jnp.where(rest == m2, lane, lanes), axis=1, keepdims=True)
    e2 = jnp.exp(m2 - m1)
    w_top = 1.0 / (1.0 + e2)
    gate =jnp.where(lane == i1, w_top, 0.0) + jnp.where(lane == i2, e2 * w_top, 0.0)
    sel = jnp.where(lane == i1, 1.0, jnp.where(lane == i2, 1.0, 0.0))
    prefix = _dot(tri_ref[...], sel.astype(BF16))
    slot1 = jnp.where(sel > 0.0, prefix, 0.0)
    hi = jnp.floor(slot1 * (1.0 / SLOT_RADIX))
    tcol_ref[:, :lanes] = gate.astype(BF16)
    tcol_ref[:, lanes:2 * lanes] = hi.astype(BF16)
    tcol_ref[:, 2 * lanes:] = (slot1 - SLOT_RADIX * hi).astype(BF16)

    def first_slab_pos(idx, picked):
        rank = jnp.sum(jnp.where(picked, prefix, 0.0), axis=1, keepdims=True) - 1.0
        return jnp.where(rank < MOE_SLAB, idx.astype(F32) * MOE_SLAB + rank, -1.0)

    pcol_ref[...] = jnp.where(
        lane == 0, first_slab_pos(i1, lane == i1), jnp.where(
            lane == 1, w_top, jnp.where(
                lane == 2, first_slab_pos(i2, lane == i2), jnp.where(lane == 3, e2 * w_top, 0.0))))
    drow_ref[...] = (slot1 - 1.0).T[:N_EXPERTS]
    cnt_ref[...] = prefix[tm - 1:tm].astype(jnp.int32)
    xb_ref[...] = x.astype(BF16)


def _route(x, router_w, router_b):
    t = x.shape[0]
    tm = MOE_TOKEN_TILE
    lanes = 128
    rw = jnp.zeros((D_MODEL, lanes), F32).at[:, :N_EXPERTS].set(router_w)
    rb = jnp.zeros((1, lanes), F32).at[0, :N_EXPERTS].set(router_b)
    tri = jnp.asarray(np.arange(tm)[:, None] >= np.arange(tm)[None, :], BF16)
    row = lambda i: (i, 0)
    const = lambda i: (0, 0)
    return pl.pallas_call(
        _route_kernel,
        grid=(t // tm,),
        in_specs=[
            pl.BlockSpec((tm, D_MODEL), row),
            pl.BlockSpec((D_MODEL, lanes), const),
            pl.BlockSpec((1, lanes), const),
            pl.BlockSpec((tm, tm), const),
        ],
        out_specs=[
            pl.BlockSpec((tm, D_MODEL), row),
            pl.BlockSpec((tm, 3 * lanes), row),
            pl.BlockSpec((tm, lanes), row),
            pl.BlockSpec((N_EXPERTS, tm), row),
            pl.BlockSpec((None, 1, lanes), lambda i: (i, 0, 0)),
        ],
        out_shape=[
            jax.ShapeDtypeStruct((t, D_MODEL), BF16),
            jax.ShapeDtypeStruct((t, 3 * lanes), BF16),
            jax.ShapeDtypeStruct((t, lanes), F32),
            jax.ShapeDtypeStruct((t // tm * N_EXPERTS, tm), F32),
            jax.ShapeDtypeStruct((t // tm, 1, lanes), jnp.int32),
        ],
        compiler_params=_params("parallel"),
        name="route",
    )(x, rw, rb, tri)


def _moe_kernel(cnt_ref, x_ref, xb_ref, tcol_ref, pcol_ref, drow_ref, w13_ref, w2_ref,
                g_ref, b_ref, o_ref, y_ref):
    i = pl.program_id(0)
    e = pl.program_id(1)
    count = cnt_ref[i * N_EXPERTS + e]

    @pl.when(e == 0)
    def _():
        o_ref[...] = jnp.zeros_like(o_ref)

    slot_row = drow_ref[pl.ds(e, 1), :]

    def expert_ffn(base, n_rows):
        rows = lax.broadcasted_iota(jnp.int32, (n_rows, 1), 0).astype(F32)
        pick = jnp.where(slot_row == rows + base, 1.0, 0.0).astype(BF16)
        xs = _dot(pick, xb_ref[...]).astype(BF16)
        a = _dot(xs, w13_ref[...])
        gate, up = a[:, :D_FF_EXPERT], a[:, D_FF_EXPERT:]
        hid = gate * _sigmoid(gate) * up
        return _dot(hid.astype(BF16), w2_ref[...]).astype(BF16)

    parked = pl.ds(pl.multiple_of(e * MOE_SLAB, 16), MOE_SLAB)

    @pl.when(count <= MOE_SLAB_SMALL)
    def _():
        y_ref[parked, :] = jnp.concatenate(
            [expert_ffn(0.0, MOE_SLAB_SMALL), jnp.zeros((MOE_SLAB - MOE_SLAB_SMALL, D_MODEL), BF16)], axis=0)

    @pl.when(count > MOE_SLAB_SMALL)
    def _():
        y_ref[parked, :] = expert_ffn(0.0, MOE_SLAB)

    def later_slab(s, carry):
        base = (s * MOE_SLAB).astype(F32)
        pick_e = jnp.where(lax.broadcasted_iota(jnp.int32, (LANES, LANES), 0) == e, 1.0, 0.0).astype(BF16)
        gate_b = _dot(tcol_ref[:, :LANES], pick_e)
        slot_b = (SLOT_RADIX * _dot(tcol_ref[:, LANES:2 * LANES], pick_e)
                  + _dot(tcol_ref[:, 2 * LANES:], pick_e) - 1.0)
        cols = lax.broadcasted_iota(jnp.int32, (1, LANES), 1).astype(F32)
        y = jnp.concatenate([expert_ffn(base, MOE_SLAB),
                             jnp.zeros((MOE_SLAB_K - MOE_SLAB, D_MODEL), BF16)], axis=0)
        put = jnp.concatenate(
            [jnp.where(slot_b == cols + (base + k * LANES), gate_b, 0.0).astype(BF16)
             for k in range(MOE_SLAB_K // LANES)], axis=1)
        o_ref[...] += _dot(put, y)
        return carry

    n_slabs = (count + MOE_SLAB - 1) // MOE_SLAB
    lax.fori_loop(1, n_slabs, later_slab, 0)

    @pl.when(e == N_EXPERTS - 1)
    def _():
        p = pcol_ref[...]
        pos_a, gate_a, pos_b, gate_b = p[:, 0:1], p[:, 1:2], p[:, 2:3], p[:, 3:4]
        acc = DEEPNORM_ALPHA * x_ref[...] + o_ref[...]
        for k in range(N_EXPERTS * MOE_SLAB // MOE_COMBINE_K):
            cols = lax.broadcasted_iota(jnp.int32, (1, MOE_COMBINE_K), 1).astype(F32) + float(k * MOE_COMBINE_K)
            put = jnp.where(pos_a == cols, gate_a, jnp.where(pos_b == cols, gate_b, 0.0)).astype(BF16)
            acc = acc + _dot(put, y_ref[k * MOE_COMBINE_K:(k + 1) * MOE_COMBINE_K, :])
        o_ref[...] = _layer_norm(acc, g_ref[...], b_ref[...])


def _moe(x, router_w, router_b, w13, w2, g, b):
    t = x.shape[0]
    tm = MOE_TOKEN_TILE
    xb, tcol, pcol, drow, cnt = _route(x, router_w, router_b)
    counts = cnt[:, 0, :N_EXPERTS].reshape(-1)
    row = lambda i, e, c: (i, 0)
    const = lambda i, e, c: (0, 0)
    expert = lambda i, e, c: (e, 0, 0)
    grid_spec = pltpu.PrefetchScalarGridSpec(
        num_scalar_prefetch=1,
        grid=(t // tm, N_EXPERTS),
        in_specs=[
            pl.BlockSpec((tm, D_MODEL), row),
            pl.BlockSpec((tm, D_MODEL), row),
            pl.BlockSpec((tm, 3 * LANES), row),
            pl.BlockSpec((tm, LANES), row),
            pl.BlockSpec((N_EXPERTS, tm), row),
            pl.BlockSpec((None, D_MODEL, 2 * D_FF_EXPERT), expert),
            pl.BlockSpec((None, D_FF_EXPERT, D_MODEL), expert),
            pl.BlockSpec((1, D_MODEL), const),
            pl.BlockSpec((1, D_MODEL), const),
        ],
        out_specs=pl.BlockSpec((tm, D_MODEL), row),
        scratch_shapes=[pltpu.VMEM((N_EXPERTS * MOE_SLAB, D_MODEL), BF16)],
    )
    return pl.pallas_call(
        _moe_kernel,
        grid_spec=grid_spec,
        out_shape=jax.ShapeDtypeStruct((t, D_MODEL), F32),
        compiler_params=_params("parallel", "arbitrary"),
        name="moe",
    )(counts, x, xb, tcol, pcol, drow, w13, w2, g.reshape(1, -1), b.reshape(1, -1))


def _attn_bias_table():
    qi = np.arange(BLK)[:, None]
    kj = np.arange(3 * BLK)[None, :]
    dist = np.abs(qi - kj + BLK)
    slopes = np.exp2(-8.0 * np.arange(1, N_Q_HEADS + 1) / N_Q_HEADS)
    bias = -(LOG2E * slopes)[:, None, None] * dist[None]
    in_window = (dist <= WINDOW)[None]
    variants = []
    for ok in (kj >= BLK, kj >= 0, kj < 2 * BLK):
        per_head = np.where(in_window & ok[None], bias, NEG_INF)
        per_head = per_head.reshape(N_KV_HEADS, 2, 2, BLK, 3 * BLK)
        variants.append(np.transpose(per_head, (0, 2, 1, 3, 4)).reshape(N_KV_HEADS, 2, 2 * BLK, 3 * BLK))
    return np.stack(variants).astype(np.float32)


def _retention_tables(decay_fwd, decay_bwd):
    pos = jnp.arange(BLK, dtype=F32)
    diff = jnp.arange(BLK)[:, None] - jnp.arange(BLK)[None, :]
    lg_f = jax.nn.log_sigmoid(decay_fwd.astype(F32))
    lg_b = jax.nn.log_sigmoid(decay_bwd.astype(F32))

    def intra(lg, d, mask):
        dec = jnp.where(mask[None], jnp.exp(jnp.maximum(d, 0).astype(F32)[None] * lg[:, None, None]), 0.0)
        return jnp.transpose(dec, (1, 0, 2)).reshape(BLK, N_RET_HEADS * BLK)

    def per_pos(lg, expo, width):
        return jnp.repeat(jnp.exp(expo[:, None] * lg[None, :]), width, axis=1)

    return {
        "intra_f": intra(lg_f, diff, diff >= 0),
        "intra_b": intra(lg_b, -diff, diff < 0),
        "qdec_f": per_pos(lg_f, pos + 1.0, RET_VAL_DIM),
        "qdec_b": per_pos(lg_b, BLK - pos, RET_VAL_DIM),
        "kdec_f": per_pos(lg_f, BLK - 1.0 - pos, RET_KEY_DIM),
        "kdec_b": per_pos(lg_b, pos, RET_KEY_DIM),
        "cdec_f": jnp.repeat(jnp.exp(BLK * lg_f), RET_VAL_DIM)[None, :],
        "cdec_b": jnp.repeat(jnp.exp(BLK * lg_b), RET_VAL_DIM)[None, :],
    }


def _group_avg_matrix():
    head = np.arange(RET_V_WIDTH) // RET_VAL_DIM
    return jnp.asarray(np.where(head[:, None] == head[None, :], 1.0 / RET_VAL_DIM, 0.0), BF16)


def kernel(x, ln_emb_g, ln_emb_b, w_in, b_gate, sink_logits, decay_fwd, decay_bwd,
           w_o_attn, w_o_ret, w_out, ln1_g, ln1_b, ffn_w1, ffn_w3, ffn_w2,
           router_w, router_b, moe_w1, moe_w3, moe_w2, ln2_g, ln2_b):
    batch, seq, _ = x.shape
    nb = seq // BLK
    xt = x.reshape(batch * seq, D_MODEL)
    attn_bias = _attn_bias_table()
    group_avg = _group_avg_matrix()
    for layer in range(DEPTH):
        tabs = _retention_tables(decay_fwd[layer], decay_bwd[layer])
        tabs["attn_bias"] = attn_bias
        tabs["group_avg"] = group_avg
        w = w_in[layer].astype(BF16)
        if layer == 0:
            h, qa, kva, ret, gl, kvf, prev_b = _in_proj(xt, w, tabs, seq, ln=(ln_emb_g, ln_emb_b))
        else:
            h = xt
            qa, kva, ret, gl, kvf, prev_b = _in_proj(xt, w, tabs, seq)
        x1 = _mix_proj(qa, kva, ret, kvf, prev_b, LOG2E * sink_logits[layer].astype(F32), tabs,
                       h, gl, 0.5 * b_gate[layer], w_o_attn[layer].astype(BF16),
                       w_o_ret[layer].astype(BF16), (0.5 * w_out[layer]).astype(BF16),
                       ln1_g[layer], ln1_b[layer], batch, nb)
        i = layer // 2
        if layer % 2 == 0:
            xt = _ffn(x1, ffn_w1[i].astype(BF16), ffn_w3[i].astype(BF16), ffn_w2[i].astype(BF16),
                      ln2_g[layer], ln2_b[layer])
        else:
            w13 = jnp.concatenate([moe_w1[i], moe_w3[i]], axis=-1).astype(BF16)
            xt = _moe(x1, router_w[i], router_b[i], w13, moe_w2[i].astype(BF16), ln2_g[layer], ln2_b[layer])
    return xt.reshape(batch, seq, D_MODEL)
```

```python
import functools

import jax
import jax.numpy as jnp
import numpy as np
from jax import lax
from jax.experimental import pallas as pl
from jax.experimental.pallas import tpu as pltpu

F32 = jnp.float32
BF16 = jnp.bfloat16

D_MODEL = 1024
DEPTH = 2
N_Q_HEADS = 8
N_KV_HEADS = 2
HEAD_DIM = 64
WINDOW = 128
BLK = 128
ATTN_Q_WIDTH = N_Q_HEADS * HEAD_DIM
ATTN_KV_WIDTH = N_KV_HEADS * HEAD_DIM
N_RET_HEADS = 8
RET_KEY_DIM = 32
RET_VAL_DIM = 64
RET_QK_WIDTH = N_RET_HEADS * RET_KEY_DIM
RET_V_WIDTH = N_RET_HEADS * RET_VAL_DIM
RET_WIDTH = 2 * RET_QK_WIDTH + 3 * RET_V_WIDTH
D_FF_DENSE = 2816
N_EXPERTS = 8
D_FF_EXPERT = 1408
IN_WIDTH = ATTN_Q_WIDTH + 2 * ATTN_KV_WIDTH + RET_WIDTH + 2 * D_MODEL
DEEPNORM_ALPHA = (2 * DEPTH) ** 0.25
LN_EPS = 1e-5
GN_EPS = 1e-5
NEG_INF = -1e30
LOG2E = 1.4426950408889634

TOKEN_TILE = 1024
FFN_TOKEN_TILE = 1024
MOE_TOKEN_TILE = 1024
ROUTE_PREFIX_BLOCK = 256
MOE_SLAB = 288
MOE_SLAB_SMALL = 256
MOE_SLAB_K = 384
MOE_COMBINE_K = 256
LANES = 128
SLOT_RADIX = 32.0
MIX_CHUNKS = 4
FF_CHUNK = 256
VMEM_LIMIT_BYTES = 56 * 1024 * 1024


def _params(*sem):
    return pltpu.CompilerParams(dimension_semantics=sem, vmem_limit_bytes=VMEM_LIMIT_BYTES)


def _layer_norm(x, g, b):
    mu = jnp.mean(x, axis=-1, keepdims=True)
    d = x - mu
    var = jnp.mean(d * d, axis=-1, keepdims=True)
    return d * lax.rsqrt(var + LN_EPS) * g + b


def _sigmoid(x):
    return 1.0 / (1.0 + jnp.exp(-x))


def _dot(a, b):
    return jnp.dot(a, b, preferred_element_type=F32)


def _dot_nt(a, b):
    return lax.dot_general(a, b, (((1,), (1,)), ((), ())), preferred_element_type=F32)


def _head_block_mask(rows, cols, row_div, col_div):
    r = lax.broadcasted_iota(jnp.int32, (rows, cols), 0) // row_div
    c = lax.broadcasted_iota(jnp.int32, (rows, cols), 1) // col_div
    return r == c


def _in_proj_kernel(apply_ln, tiles_per_seq, *refs):
    sb_ref = refs[-1]
    tile = pl.num_programs(0) - 1 - pl.program_id(0)

    @pl.when(lax.rem(tile, tiles_per_seq) == tiles_per_seq - 1)
    def _():
        sb_ref[...] = jnp.zeros_like(sb_ref)

    if apply_ln:
        (x_ref, g_ref, b_ref, w_ref, kdf_ref, kdb_ref, cdb_ref,
         h_ref, qa_ref, kva_ref, ret_ref, gl_ref, kvf_ref, pb_ref, _) = refs
        x = _layer_norm(x_ref[...], g_ref[...], b_ref[...])
        h_ref[...] = x
    else:
        (x_ref, w_ref, kdf_ref, kdb_ref, cdb_ref,
         qa_ref, kva_ref, ret_ref, gl_ref, kvf_ref, pb_ref, _) = refs
        x = x_ref[...]
    xb = x.astype(BF16)

    def proj(lo, hi):
        return _dot(xb, w_ref[:, lo:hi])

    c0 = ATTN_Q_WIDTH
    c1 = c0 + 2 * ATTN_KV_WIDTH
    qa_ref[...] = (proj(0, c0) * (HEAD_DIM ** -0.5 * LOG2E)).astype(BF16)
    kva_ref[...] = proj(c0, c1).astype(BF16)
    ret_ref[:, :RET_QK_WIDTH] = proj(c1, c1 + RET_QK_WIDTH).astype(BF16)
    rk = (proj(c1 + RET_QK_WIDTH, c1 + 2 * RET_QK_WIDTH) * (RET_KEY_DIM ** -0.5)).astype(BF16)
    ret_ref[:, RET_QK_WIDTH:2 * RET_QK_WIDTH] = rk
    rest = proj(c1 + 2 * RET_QK_WIDTH, c1 + RET_WIDTH).astype(BF16)
    ret_ref[:, 2 * RET_QK_WIDTH:] = rest
    gl_ref[...] = (0.5 * proj(c1 + RET_WIDTH, IN_WIDTH)).astype(BF16)

    half = N_RET_HEADS // 2
    mask = _head_block_mask(half * RET_KEY_DIM, half * RET_VAL_DIM, RET_KEY_DIM, RET_VAL_DIM)
    def increment(c, kd_ref):
        rows = slice(c * BLK, (c + 1) * BLK)
        kz = (rk[rows].astype(F32) * kd_ref[...]).T.astype(BF16)
        v = rest[rows, :RET_V_WIDTH]
        comps = []
        for g in range(2):
            full = jnp.where(mask, _dot(kz[g * half * RET_KEY_DIM:(g + 1) * half * RET_KEY_DIM],
                                        v[:, g * half * RET_VAL_DIM:(g + 1) * half * RET_VAL_DIM]), 0.0)
            comp = full[0:RET_KEY_DIM]
            for h in range(1, half):
                comp = comp + full[h * RET_KEY_DIM:(h + 1) * RET_KEY_DIM]
            comps.append(comp)
        return jnp.concatenate(comps, axis=1)

    state = sb_ref[...]
    for c in reversed(range(x.shape[0] // BLK)):
        kvf_ref[c] = increment(c, kdf_ref)
        pb_ref[c] = state
        state = state * cdb_ref[...] + increment(c, kdb_ref)
    sb_ref[...] = state


def _in_proj(x, w, tabs, seq, ln=None):
    t = x.shape[0]
    tm = TOKEN_TILE
    tiles = t // tm
    row = lambda i: (tiles - 1 - i, 0)
    const = lambda i: (0, 0)
    in_specs = [pl.BlockSpec((tm, D_MODEL), row)]
    args = [x]
    out_shape = []
    out_specs = []
    if ln is not None:
        in_specs += [pl.BlockSpec((1, D_MODEL), const), pl.BlockSpec((1, D_MODEL), const)]
        args += [ln[0].reshape(1, D_MODEL), ln[1].reshape(1, D_MODEL)]
        out_shape.append(jax.ShapeDtypeStruct((t, D_MODEL), F32))
        out_specs.append(pl.BlockSpec((tm, D_MODEL), row))
    in_specs += [pl.BlockSpec((D_MODEL, IN_WIDTH), const),
                 pl.BlockSpec((BLK, RET_QK_WIDTH), const), pl.BlockSpec((BLK, RET_QK_WIDTH), const),
                 pl.BlockSpec((1, RET_V_WIDTH), const)]
    args += [w, tabs["kdec_f"], tabs["kdec_b"], tabs["cdec_b"]]
    for width in (ATTN_Q_WIDTH, 2 * ATTN_KV_WIDTH, RET_WIDTH, 2 * D_MODEL):
        out_shape.append(jax.ShapeDtypeStruct((t, width), BF16))
        out_specs.append(pl.BlockSpec((tm, width), row))
    for _ in range(2):
        out_shape.append(jax.ShapeDtypeStruct((t // BLK, RET_KEY_DIM, RET_V_WIDTH), F32))
        out_specs.append(pl.BlockSpec((tm // BLK, RET_KEY_DIM, RET_V_WIDTH), lambda i: (tiles - 1 - i, 0, 0)))
    return pl.pallas_call(
        functools.partial(_in_proj_kernel, ln is not None, seq // tm),
        grid=(tiles,),
        in_specs=in_specs,
        out_specs=out_specs,
        out_shape=out_shape,
        scratch_shapes=[pltpu.VMEM((RET_KEY_DIM, RET_V_WIDTH), F32)],
        compiler_params=_params("arbitrary"),
        name="in_proj",
    )(*args)


def _split_bf16(y):
    hi = y.astype(BF16)
    lo = (y - hi.astype(F32)).astype(BF16)
    return hi, lo


def _mixer_body(n, nb, sink_ref, qa_ref, kvp_ref, kvc_ref, kvn_ref, ret_ref, pf_ref, pb_ref,
                bias_ref, df_ref, db_ref, qdf_ref, qdb_ref, gavg_ref, ya_ref, r_ref):
    q = qa_ref[...]
    kv = jnp.concatenate([kvp_ref[...], kvc_ref[...], kvn_ref[...]], axis=0).astype(F32)
    lo_half = lax.broadcasted_iota(jnp.int32, (kv.shape[0], 2 * HEAD_DIM), 1) < HEAD_DIM

    def lane_variants(a):
        swapped = pltpu.roll(a, HEAD_DIM, 1)
        z = jnp.zeros_like(a)
        return (((jnp.where(lo_half, a, z)).astype(BF16), (jnp.where(lo_half, z, swapped)).astype(BF16)),
                ((jnp.where(lo_half, swapped, z)).astype(BF16), (jnp.where(lo_half, z, a)).astype(BF16)))

    k_var = lane_variants(kv[:, :ATTN_KV_WIDTH])
    v_var = lane_variants(kv[:, ATTN_KV_WIDTH:])
    top_rows = lax.broadcasted_iota(jnp.int32, (2 * BLK, 1), 0) < BLK
    out_lo_half = lax.broadcasted_iota(jnp.int32, (2 * BLK, 2 * HEAD_DIM), 1) < HEAD_DIM
    half_heads = N_RET_HEADS // 2
    k_mask = _head_block_mask(N_RET_HEADS * BLK, RET_QK_WIDTH, BLK, RET_KEY_DIM)
    v_mask = _head_block_mask(half_heads * BLK, half_heads * RET_VAL_DIM, BLK, RET_VAL_DIM)
    s_mask = _head_block_mask(RET_QK_WIDTH, RET_V_WIDTH, RET_KEY_DIM, RET_VAL_DIM)
    pair = 2 * HEAD_DIM
    ys = []
    for j in range(MIX_CHUNKS):
        c = n * MIX_CHUNKS + j
        var = jnp.where(c == 0, 0, jnp.where(c == nb - 1, 2, 1))
        tok = slice(j * BLK, (j + 1) * BLK)
        keys = slice(j * BLK, (j + 3) * BLK)

        for g in range(N_KV_HEADS):
            q2 = jnp.concatenate([q[tok, pair * 2 * g:pair * (2 * g + 1)],
                                  q[tok, pair * (2 * g + 1):pair * (2 * g + 2)]], axis=0)
            s_all = _dot_nt(q2, jnp.concatenate([k_var[g][0][keys], k_var[g][1][keys]], axis=0))
            probs, denoms = [], []
            for r in range(2):
                sink = jnp.where(top_rows, sink_ref[4 * g + r], sink_ref[4 * g + 2 + r])
                s = s_all[:, 3 * BLK * r:3 * BLK * (r + 1)] + bias_ref[var, g, r]
                m = jnp.maximum(jnp.max(s, axis=1, keepdims=True), sink)
                e = jnp.exp2(s - m)
                denoms.append(jnp.sum(e, axis=1, keepdims=True) + jnp.exp2(sink - m))
                probs.append(e.astype(BF16))
            pv = _dot(jnp.concatenate(probs, axis=1),
                      jnp.concatenate([v_var[g][0][keys], v_var[g][1][keys]], axis=0))
            out = pv / jnp.where(out_lo_half, denoms[0], denoms[1])
            ya_ref[tok, pair * 2 * g:pair * (2 * g + 1)] = out[:BLK].astype(BF16)
            ya_ref[tok, pair * (2 * g + 1):pair * (2 * g + 2)] = out[BLK:].astype(BF16)

        rq = ret_ref[tok, :RET_QK_WIDTH]
        rk = ret_ref[tok, RET_QK_WIDTH:2 * RET_QK_WIDTH]
        rv = ret_ref[tok, 2 * RET_QK_WIDTH:2 * RET_QK_WIDTH + RET_V_WIDTH]
        k_exp = jnp.where(k_mask, jnp.concatenate([rk] * N_RET_HEADS, axis=0), jnp.zeros((), BF16))
        qk = _dot_nt(rq, k_exp)
        p_fb = jnp.concatenate([(qk * df_ref[...]).astype(BF16), (qk * db_ref[...]).astype(BF16)], axis=0)
        intra = []
        for g in range(2):
            cols = slice(g * half_heads * RET_VAL_DIM, (g + 1) * half_heads * RET_VAL_DIM)
            v_bd = jnp.where(v_mask, jnp.concatenate([rv[:, cols]] * half_heads, axis=0), jnp.zeros((), BF16))
            intra.append(_dot(p_fb[:, g * half_heads * BLK:(g + 1) * half_heads * BLK], v_bd))
        cross = []
        for state_ref, qdec_ref in ((pf_ref, qdf_ref), (pb_ref, qdb_ref)):
            s_bd = jnp.where(s_mask, jnp.concatenate([state_ref[j]] * N_RET_HEADS, axis=0), 0.0)
            cross.append(_dot(rq, s_bd.astype(BF16)) * qdec_ref[...])
        ys.append(jnp.concatenate(intra, axis=1) + jnp.concatenate(cross, axis=0))

    y = jnp.concatenate(ys, axis=0)
    d = y - _dot(y.astype(BF16), gavg_ref[...])
    normed = d * lax.rsqrt(_dot((d * d).astype(BF16), gavg_ref[...]) + GN_EPS)
    for j in range(MIX_CHUNKS):
        tok = slice(j * BLK, (j + 1) * BLK)
        g_f = ret_ref[tok, 2 * RET_QK_WIDTH + RET_V_WIDTH:2 * RET_QK_WIDTH + 2 * RET_V_WIDTH].astype(F32)
        g_b = ret_ref[tok, 2 * RET_QK_WIDTH + 2 * RET_V_WIDTH:].astype(F32)
        r = (g_f * _sigmoid(g_f) * normed[2 * j * BLK:(2 * j + 1) * BLK]
             + g_b * _sigmoid(g_b) * normed[(2 * j + 1) * BLK:(2 * j + 2) * BLK])
        r_ref[tok, :] = r.astype(BF16)


def _mix_proj_kernel(nb, n_blocks, sink_ref, qa_ref, kvp_ref, kvc_ref, kvn_ref, ret_ref, kvf_ref, pb_ref,
                     bias_ref, df_ref, db_ref, qdf_ref, qdb_ref, gavg_ref, cdf_ref,
                     h_ref, gl_ref, bg_ref, woa_ref, wor_ref, wout_ref, g_ref, b_ref,
                     x_ref, ya_scr, r_scr, sf_ref, pf_ref):
    s = pl.program_id(0)

    n = lax.rem(jnp.minimum(s, n_blocks - 1), nb // MIX_CHUNKS)

    @pl.when(s == 0)
    def _():
        ya_scr[...] = jnp.zeros_like(ya_scr)
        r_scr[...] = jnp.zeros_like(r_scr)

    @pl.when(n == 0)
    def _():
        sf_ref[...] = jnp.zeros_like(sf_ref)

    slot = lax.rem(s, 2)
    ya_prev = ya_scr[1 - slot]
    r_prev = r_scr[1 - slot]
    t = jnp.tanh(gl_ref[...].astype(F32) + bg_ref[...])
    pa = _dot(ya_prev, woa_ref[...])
    pr = _dot(r_prev, wor_ref[...])
    merged = pa + pr + t[:, :D_MODEL] * pa + t[:, D_MODEL:] * pr
    mix = _dot(merged.astype(BF16), wout_ref[...])
    x_ref[...] = _layer_norm(DEEPNORM_ALPHA * h_ref[...] + mix, g_ref[...], b_ref[...])

    state = sf_ref[...]
    for j in range(MIX_CHUNKS):
        pf_ref[j] = state
        state = state * cdf_ref[...] + kvf_ref[j]
    sf_ref[...] = state

    _mixer_body(n, nb, sink_ref, qa_ref, kvp_ref, kvc_ref, kvn_ref, ret_ref, pf_ref, pb_ref,
                bias_ref, df_ref, db_ref, qdf_ref, qdb_ref, gavg_ref, ya_scr.at[slot], r_scr.at[slot])


def _mix_proj(qa, kva, ret, kvf, prev_b, sink, tabs, h, gl, b_gate, woa, wor, wout, g, b, batch, nb):
    t = qa.shape[0]
    steps = nb // MIX_CHUNKS
    rows = MIX_CHUNKS * BLK
    n_blocks = batch * steps

    def mixed(s):
        return jnp.minimum(s, n_blocks - 1)

    def projected(s):
        return jnp.maximum(s - 1, 0)

    cur = lambda s: (mixed(s), 0)
    prv = lambda s: (mixed(s) * MIX_CHUNKS - jnp.where(lax.rem(mixed(s), steps) == 0, 0, 1), 0)
    nxt = lambda s: (mixed(s) * MIX_CHUNKS + MIX_CHUNKS - jnp.where(lax.rem(mixed(s), steps) == steps - 1, 1, 0), 0)
    out = lambda s: (projected(s), 0)
    c2 = lambda s: (0, 0)
    state_spec = pl.BlockSpec((MIX_CHUNKS, RET_KEY_DIM, RET_V_WIDTH), lambda s: (mixed(s), 0, 0))
    return pl.pallas_call(
        functools.partial(_mix_proj_kernel, nb, n_blocks),
        grid=(n_blocks + 1,),
        in_specs=[
            pl.BlockSpec(memory_space=pltpu.SMEM),
            pl.BlockSpec((rows, ATTN_Q_WIDTH), cur),
            pl.BlockSpec((BLK, 2 * ATTN_KV_WIDTH), prv),
            pl.BlockSpec((rows, 2 * ATTN_KV_WIDTH), cur),
            pl.BlockSpec((BLK, 2 * ATTN_KV_WIDTH), nxt),
            pl.BlockSpec((rows, RET_WIDTH), cur),
            state_spec,
            state_spec,
            pl.BlockSpec((3, N_KV_HEADS, 2, 2 * BLK, 3 * BLK), lambda s: (0, 0, 0, 0, 0)),
            pl.BlockSpec((BLK, N_RET_HEADS * BLK), c2),
            pl.BlockSpec((BLK, N_RET_HEADS * BLK), c2),
            pl.BlockSpec((BLK, RET_V_WIDTH), c2),
            pl.BlockSpec((BLK, RET_V_WIDTH), c2),
            pl.BlockSpec((RET_V_WIDTH, RET_V_WIDTH), c2),
            pl.BlockSpec((1, RET_V_WIDTH), c2),
            pl.BlockSpec((rows, D_MODEL), out),
            pl.BlockSpec((rows, 2 * D_MODEL), out),
            pl.BlockSpec((1, 2 * D_MODEL), c2),
            pl.BlockSpec((ATTN_Q_WIDTH, D_MODEL), c2),
            pl.BlockSpec((RET_V_WIDTH, D_MODEL), c2),
            pl.BlockSpec((D_MODEL, D_MODEL), c2),
            pl.BlockSpec((1, D_MODEL), c2),
            pl.BlockSpec((1, D_MODEL), c2),
        ],
        out_specs=pl.BlockSpec((rows, D_MODEL), out),
        out_shape=jax.ShapeDtypeStruct((t, D_MODEL), F32),
        scratch_shapes=[pltpu.VMEM((2, rows, ATTN_Q_WIDTH), BF16), pltpu.VMEM((2, rows, RET_V_WIDTH), BF16),
                        pltpu.VMEM((RET_KEY_DIM, RET_V_WIDTH), F32),
                        pltpu.VMEM((MIX_CHUNKS, RET_KEY_DIM, RET_V_WIDTH), F32)],
        compiler_params=_params("arbitrary"),
        name="mix_proj",
    )(sink, qa, kva, kva, kva, ret, kvf, prev_b, tabs["attn_bias"], tabs["intra_f"],
      tabs["intra_b"], tabs["qdec_f"], tabs["qdec_b"], tabs["group_avg"], tabs["cdec_f"],
      h, gl, b_gate.reshape(1, -1), woa, wor, wout, g.reshape(1, -1), b.reshape(1, -1))


def _ffn_kernel(x_ref, w1_ref, w3_ref, w2_ref, g_ref, b_ref, o_ref):
    x = x_ref[...]
    xb = x.astype(BF16)
    acc = None
    for c in range(D_FF_DENSE // FF_CHUNK):
        cols = slice(c * FF_CHUNK, (c + 1) * FF_CHUNK)
        a = _dot(xb, w1_ref[:, cols])
        hid = a * _sigmoid(a) * _dot(xb, w3_ref[:, cols])
        part = _dot(hid.astype(BF16), w2_ref[cols, :])
        acc = part if acc is None else acc + part
    o_ref[...] = _layer_norm(DEEPNORM_ALPHA * x + acc, g_ref[...], b_ref[...])


def _ffn(x, w1, w3, w2, g, b):
    t = x.shape[0]
    tm = FFN_TOKEN_TILE
    row = lambda i: (i, 0)
    const = lambda i: (0, 0)
    return pl.pallas_call(
        _ffn_kernel,
        grid=(t // tm,),
        in_specs=[
            pl.BlockSpec((tm, D_MODEL), row),
            pl.BlockSpec((D_MODEL, D_FF_DENSE), const),
            pl.BlockSpec((D_MODEL, D_FF_DENSE), const),
            pl.BlockSpec((D_FF_DENSE, D_MODEL), const),
            pl.BlockSpec((1, D_MODEL), const),
            pl.BlockSpec((1, D_MODEL), const),
        ],
        out_specs=pl.BlockSpec((tm, D_MODEL), row),
        out_shape=jax.ShapeDtypeStruct((t, D_MODEL), F32),
        compiler_params=_params("parallel"),
        name="ffn",
    )(x, w1, w3, w2, g.reshape(1, -1), b.reshape(1, -1))


def _route_kernel(x_ref, rw_ref, rb_ref, tri_ref, xb_ref, tcol_ref, pcol_ref, drow_ref, cnt_ref):
    x = x_ref[...]
    tm, lanes = x.shape[0], LANES
    lane = lax.broadcasted_iota(jnp.int32, (tm, lanes), 1)
    xh, xl = _split_bf16(x)
    wh, wl = _split_bf16(rw_ref[...])
    hi_part = _dot(xh, jnp.concatenate([wh, wl], axis=1))
    logits = hi_part[:, :lanes] + hi_part[:, lanes:] + _dot(xl, wh) + rb_ref[...]
    logits = jnp.where(lane < N_EXPERTS, logits, NEG_INF)
    m1 = jnp.max(logits, axis=1, keepdims=True)
    i1 = jnp.min(jnp.where(logits == m1, lane, lanes), axis=1, keepdims=True)
    rest = jnp.where(lane == i1, NEG_INF, logits)
    m2 = jnp.max(rest, axis=1, keepdims=True)
    i2 = jnp.min(jnp.where(rest == m2, lane, lanes), axis=1, keepdims=True)
    e2 = jnp.exp(m2 - m1)
    w_top = 1.0 / (1.0 + e2)
    gate =jnp.where(lane == i1, w_top, 0.0) + jnp.where(lane == i2, e2 * w_top, 0.0)
    sel = jnp.where(lane == i1, 1.0, jnp.where(lane == i2, 1.0, 0.0))
    sel_b = sel.astype(BF16)
    blk = tri_ref.shape[0]
    parts, above = [], jnp.zeros((1, lanes), F32)
    for r0 in range(0, tm, blk):
        local = _dot(tri_ref[...], sel_b[r0:r0 + blk])
        parts.append(local + above)
        above = above + local[blk - 1:blk]
    prefix = jnp.concatenate(parts, axis=0)
    slot1 = jnp.where(sel > 0.0, prefix, 0.0)
    hi = jnp.floor(slot1 * (1.0 / SLOT_RADIX))
    tcol_ref[:, :lanes] = gate.astype(BF16)
    tcol_ref[:, lanes:2 * lanes] = hi.astype(BF16)
    tcol_ref[:, 2 * lanes:] = (slot1 - SLOT_RADIX * hi).astype(BF16)

    def first_slab_pos(idx, picked):
        rank = jnp.sum(jnp.where(picked, prefix, 0.0), axis=1, keepdims=True) - 1.0
        return jnp.where(rank < MOE_SLAB, idx.astype(F32) * MOE_SLAB + rank, -1.0)

    pcol_ref[...] = jnp.where(
        lane == 0, first_slab_pos(i1, lane == i1), jnp.where(
            lane == 1, w_top, jnp.where(
                lane == 2, first_slab_pos(i2, lane == i2), jnp.where(lane == 3, e2 * w_top, 0.0))))
    drow_ref[...] = (slot1 - 1.0).T[:N_EXPERTS]
    cnt_ref[...] = prefix[tm - 1:tm].astype(jnp.int32)
    xb_ref[...] = x.astype(BF16)


def _route(x, router_w, router_b):
    t = x.shape[0]
    tm = MOE_TOKEN_TILE
    lanes = 128
    rw = jnp.zeros((D_MODEL, lanes), F32).at[:, :N_EXPERTS].set(router_w)
    rb = jnp.zeros((1, lanes), F32).at[0, :N_EXPERTS].set(router_b)
    tri = jnp.asarray(np.arange(ROUTE_PREFIX_BLOCK)[:, None] >= np.arange(ROUTE_PREFIX_BLOCK)[None, :], BF16)
    row = lambda i: (i, 0)
    const = lambda i: (0, 0)
    return pl.pallas_call(
        _route_kernel,
        grid=(t // tm,),
        in_specs=[
            pl.BlockSpec((tm, D_MODEL), row),
            pl.BlockSpec((D_MODEL, lanes), const),
            pl.BlockSpec((1, lanes), const),
            pl.BlockSpec((ROUTE_PREFIX_BLOCK, ROUTE_PREFIX_BLOCK), const),
        ],
        out_specs=[
            pl.BlockSpec((tm, D_MODEL), row),
            pl.BlockSpec((tm, 3 * lanes), row),
            pl.BlockSpec((tm, lanes), row),
            pl.BlockSpec((N_EXPERTS, tm), row),
            pl.BlockSpec((None, 1, lanes), lambda i: (i, 0, 0)),
        ],
        out_shape=[
            jax.ShapeDtypeStruct((t, D_MODEL), BF16),
            jax.ShapeDtypeStruct((t, 3 * lanes), BF16),
            jax.ShapeDtypeStruct((t, lanes), F32),
            jax.ShapeDtypeStruct((t // tm * N_EXPERTS, tm), F32),
            jax.ShapeDtypeStruct((t // tm, 1, lanes), jnp.int32),
        ],
        compiler_params=_params("parallel"),
        name="route",
    )(x, rw, rb, tri)


def _moe_kernel(cnt_ref, x_ref, xb_ref, tcol_ref, pcol_ref, drow_ref, w1_ref, w3_ref, w2_ref,
                g_ref, b_ref, o_ref, y_ref):
    i = pl.program_id(0)
    e = pl.program_id(1)
    count = cnt_ref[i * N_EXPERTS + e]

    @pl.when(e == 0)
    def _():
        o_ref[...] = jnp.zeros_like(o_ref)

    slot_row = drow_ref[pl.ds(e, 1), :]

    def expert_ffn(base, n_rows):
        rows = lax.broadcasted_iota(jnp.int32, (n_rows, 1), 0).astype(F32)
        pick = jnp.where(slot_row == rows + base, 1.0, 0.0).astype(BF16)
        xs = _dot(pick, xb_ref[...]).astype(BF16)
        main = D_FF_EXPERT - LANES
        gate = _dot(xs, w1_ref[:, :main])
        up = _dot(xs, w3_ref[:, :main])
        tail = _dot(xs, jnp.concatenate([w1_ref[:, main:], w3_ref[:, main:]], axis=1))
        gate_t, up_t = tail[:, :LANES], tail[:, LANES:]
        hid = jnp.concatenate([gate * _sigmoid(gate) * up, gate_t * _sigmoid(gate_t) * up_t], axis=1)
        return _dot(hid.astype(BF16), w2_ref[...]).astype(BF16)

    parked = pl.ds(pl.multiple_of(e * MOE_SLAB, 16), MOE_SLAB)

    @pl.when(count <= MOE_SLAB_SMALL)
    def _():
        y_ref[parked, :] = jnp.concatenate(
            [expert_ffn(0.0, MOE_SLAB_SMALL), jnp.zeros((MOE_SLAB - MOE_SLAB_SMALL, D_MODEL), BF16)], axis=0)

    @pl.when(count > MOE_SLAB_SMALL)
    def _():
        y_ref[parked, :] = expert_ffn(0.0, MOE_SLAB)

    def later_slab(s, carry):
        base = (s * MOE_SLAB).astype(F32)
        pick_e = jnp.where(lax.broadcasted_iota(jnp.int32, (LANES, LANES), 0) == e, 1.0, 0.0).astype(BF16)
        gate_b = _dot(tcol_ref[:, :LANES], pick_e)
        slot_b = (SLOT_RADIX * _dot(tcol_ref[:, LANES:2 * LANES], pick_e)
                  + _dot(tcol_ref[:, 2 * LANES:], pick_e) - 1.0)
        cols = lax.broadcasted_iota(jnp.int32, (1, LANES), 1).astype(F32)
        y = jnp.concatenate([expert_ffn(base, MOE_SLAB),
                             jnp.zeros((MOE_SLAB_K - MOE_SLAB, D_MODEL), BF16)], axis=0)
        put = jnp.concatenate(
            [jnp.where(slot_b == cols + (base + k * LANES), gate_b, 0.0).astype(BF16)
             for k in range(MOE_SLAB_K // LANES)], axis=1)
        o_ref[...] += _dot(put, y)
        return carry

    n_slabs = (count + MOE_SLAB - 1) // MOE_SLAB
    lax.fori_loop(1, n_slabs, later_slab, 0)

    @pl.when(e == N_EXPERTS - 1)
    def _():
        p = pcol_ref[...]
        pos_a, gate_a, pos_b, gate_b = p[:, 0:1], p[:, 1:2], p[:, 2:3], p[:, 3:4]
        acc = DEEPNORM_ALPHA * x_ref[...] + o_ref[...]
        for k in range(N_EXPERTS * MOE_SLAB // MOE_COMBINE_K):
            cols = lax.broadcasted_iota(jnp.int32, (1, MOE_COMBINE_K), 1).astype(F32) + float(k * MOE_COMBINE_K)
            put = jnp.where(pos_a == cols, gate_a, jnp.where(pos_b == cols, gate_b, 0.0)).astype(BF16)
            acc = acc + _dot(put, y_ref[k * MOE_COMBINE_K:(k + 1) * MOE_COMBINE_K, :])
        o_ref[...] = _layer_norm(acc, g_ref[...], b_ref[...])


def _moe(x, router_w, router_b, w1, w3, w2, g, b):
    t = x.shape[0]
    tm = MOE_TOKEN_TILE
    xb, tcol, pcol, drow, cnt = _route(x, router_w, router_b)
    counts = cnt[:, 0, :N_EXPERTS].reshape(-1)
    row = lambda i, e, c: (i, 0)
    const = lambda i, e, c: (0, 0)
    expert = lambda i, e, c: (e, 0, 0)
    grid_spec = pltpu.PrefetchScalarGridSpec(
        num_scalar_prefetch=1,
        grid=(t // tm, N_EXPERTS),
        in_specs=[
            pl.BlockSpec((tm, D_MODEL), row),
            pl.BlockSpec((tm, D_MODEL), row),
            pl.BlockSpec((tm, 3 * LANES), row),
            pl.BlockSpec((tm, LANES), row),
            pl.BlockSpec((N_EXPERTS, tm), row),
            pl.BlockSpec((None, D_MODEL, D_FF_EXPERT), expert),
            pl.BlockSpec((None, D_MODEL, D_FF_EXPERT), expert),
            pl.BlockSpec((None, D_FF_EXPERT, D_MODEL), expert),
            pl.BlockSpec((1, D_MODEL), const),
            pl.BlockSpec((1, D_MODEL), const),
        ],
        out_specs=pl.BlockSpec((tm, D_MODEL), row),
        scratch_shapes=[pltpu.VMEM((N_EXPERTS * MOE_SLAB, D_MODEL), BF16)],
    )
    return pl.pallas_call(
        _moe_kernel,
        grid_spec=grid_spec,
        out_shape=jax.ShapeDtypeStruct((t, D_MODEL), F32),
        compiler_params=_params("parallel", "arbitrary"),
        name="moe",
    )(counts, x, xb, tcol, pcol, drow, w1, w3, w2, g.reshape(1, -1), b.reshape(1, -1))


def _attn_bias_table():
    qi = np.arange(BLK)[:, None]
    kj = np.arange(3 * BLK)[None, :]
    dist = np.abs(qi - kj + BLK)
    slopes = np.exp2(-8.0 * np.arange(1, N_Q_HEADS + 1) / N_Q_HEADS)
    bias = -(LOG2E * slopes)[:, None, None] * dist[None]
    in_window = (dist <= WINDOW)[None]
    variants = []
    for ok in (kj >= BLK, kj >= 0, kj < 2 * BLK):
        per_head = np.where(in_window & ok[None], bias, NEG_INF)
        per_head = per_head.reshape(N_KV_HEADS, 2, 2, BLK, 3 * BLK)
        variants.append(np.transpose(per_head, (0, 2, 1, 3, 4)).reshape(N_KV_HEADS, 2, 2 * BLK, 3 * BLK))
    return np.stack(variants).astype(np.float32)


def _retention_tables(decay_fwd, decay_bwd):
    pos = jnp.arange(BLK, dtype=F32)
    diff = jnp.arange(BLK)[:, None] - jnp.arange(BLK)[None, :]
    lg_f = jax.nn.log_sigmoid(decay_fwd.astype(F32))
    lg_b = jax.nn.log_sigmoid(decay_bwd.astype(F32))

    def intra(lg, d, mask):
        dec = jnp.where(mask[None], jnp.exp(jnp.maximum(d, 0).astype(F32)[None] * lg[:, None, None]), 0.0)
        return jnp.transpose(dec, (1, 0, 2)).reshape(BLK, N_RET_HEADS * BLK)

    def per_pos(lg, expo, width):
        return jnp.repeat(jnp.exp(expo[:, None] * lg[None, :]), width, axis=1)

    return {
        "intra_f": intra(lg_f, diff, diff >= 0),
        "intra_b": intra(lg_b, -diff, diff < 0),
        "qdec_f": per_pos(lg_f, pos + 1.0, RET_VAL_DIM),
        "qdec_b": per_pos(lg_b, BLK - pos, RET_VAL_DIM),
        "kdec_f": per_pos(lg_f, BLK - 1.0 - pos, RET_KEY_DIM),
        "kdec_b": per_pos(lg_b, pos, RET_KEY_DIM),
        "cdec_f": jnp.repeat(jnp.exp(BLK * lg_f), RET_VAL_DIM)[None, :],
        "cdec_b": jnp.repeat(jnp.exp(BLK * lg_b), RET_VAL_DIM)[None, :],
    }


def _group_avg_matrix():
    head = np.arange(RET_V_WIDTH) // RET_VAL_DIM
    return jnp.asarray(np.where(head[:, None] == head[None, :], 1.0 / RET_VAL_DIM, 0.0), BF16)


def kernel(x, ln_emb_g, ln_emb_b, w_in, b_gate, sink_logits, decay_fwd, decay_bwd,
           w_o_attn, w_o_ret, w_out, ln1_g, ln1_b, ffn_w1, ffn_w3, ffn_w2,
           router_w, router_b, moe_w1, moe_w3, moe_w2, ln2_g, ln2_b):
    batch, seq, _ = x.shape
    nb = seq // BLK
    xt = x.reshape(batch * seq, D_MODEL)
    attn_bias = _attn_bias_table()
    group_avg = _group_avg_matrix()
    for layer in range(DEPTH):
        tabs = _retention_tables(decay_fwd[layer], decay_bwd[layer])
        tabs["attn_bias"] = attn_bias
        tabs["group_avg"] = group_avg
        w = w_in[layer].astype(BF16)
        if layer == 0:
            h, qa, kva, ret, gl, kvf, prev_b = _in_proj(xt, w, tabs, seq, ln=(ln_emb_g, ln_emb_b))
        else:
            h = xt
            qa, kva, ret, gl, kvf, prev_b = _in_proj(xt, w, tabs, seq)
        x1 = _mix_proj(qa, kva, ret, kvf, prev_b, LOG2E * sink_logits[layer].astype(F32), tabs,
                       h, gl, 0.5 * b_gate[layer], w_o_attn[layer].astype(BF16),
                       w_o_ret[layer].astype(BF16), (0.5 * w_out[layer]).astype(BF16),
                       ln1_g[layer], ln1_b[layer], batch, nb)
        i = layer // 2
        if layer % 2 == 0:
            xt = _ffn(x1, ffn_w1[i].astype(BF16), ffn_w3[i].astype(BF16), ffn_w2[i].astype(BF16),
                      ln2_g[layer], ln2_b[layer])
        else:
            xt = _moe(x1, router_w[i], router_b[i], moe_w1[i].astype(BF16), moe_w3[i].astype(BF16),
                      moe_w2[i].astype(BF16), ln2_g[layer], ln2_b[layer])
    return xt.reshape(batch, seq, D_MODEL)
```

```python
import functools

import jax
import jax.numpy as jnp
import numpy as np
from jax import lax
from jax.experimental import pallas as pl
from jax.experimental.pallas import tpu as pltpu

F32 = jnp.float32
BF16 = jnp.bfloat16

D_MODEL = 1024
DEPTH = 2
N_Q_HEADS = 8
N_KV_HEADS = 2
HEAD_DIM = 64
WINDOW = 128
BLK = 128
ATTN_Q_WIDTH = N_Q_HEADS * HEAD_DIM
ATTN_KV_WIDTH = N_KV_HEADS * HEAD_DIM
N_RET_HEADS = 8
RET_KEY_DIM = 32
RET_VAL_DIM = 64
RET_QK_WIDTH = N_RET_HEADS * RET_KEY_DIM
RET_V_WIDTH = N_RET_HEADS * RET_VAL_DIM
RET_WIDTH = 2 * RET_QK_WIDTH + 3 * RET_V_WIDTH
D_FF_DENSE = 2816
N_EXPERTS = 8
D_FF_EXPERT = 1408
IN_WIDTH = ATTN_Q_WIDTH + 2 * ATTN_KV_WIDTH + RET_WIDTH + 2 * D_MODEL
DEEPNORM_ALPHA = (2 * DEPTH) ** 0.25
LN_EPS = 1e-5
GN_EPS = 1e-5
NEG_INF = -1e30
LOG2E = 1.4426950408889634

TOKEN_TILE = 1024
FFN_TOKEN_TILE = 1024
MOE_TOKEN_TILE = 1024
ROUTE_PREFIX_BLOCK = 256
MOE_SLAB = 288
MOE_SLAB_SMALL = 256
MOE_SLAB_K = 384
MOE_COMBINE_K = 256
GN_BLOCK = 256
LANES = 128
SLOT_RADIX = 32.0
MIX_CHUNKS = 4
FF_CHUNK = 256
VMEM_LIMIT_BYTES = 56 * 1024 * 1024


def _params(*sem):
    return pltpu.CompilerParams(dimension_semantics=sem, vmem_limit_bytes=VMEM_LIMIT_BYTES)


def _layer_norm(x, g, b):
    mu = jnp.mean(x, axis=-1, keepdims=True)
    d = x - mu
    var = jnp.mean(d * d, axis=-1, keepdims=True)
    return d * lax.rsqrt(var + LN_EPS) * g + b


def _sigmoid(x):
    return 1.0 / (1.0 + jnp.exp(-x))


def _dot(a, b):
    return jnp.dot(a, b, preferred_element_type=F32)


def _dot_nt(a, b):
    return lax.dot_general(a, b, (((1,), (1,)), ((), ())), preferred_element_type=F32)


def _head_block_mask(rows, cols, row_div, col_div):
    r = lax.broadcasted_iota(jnp.int32, (rows, cols), 0) // row_div
    c = lax.broadcasted_iota(jnp.int32, (rows, cols), 1) // col_div
    return r == c


def _in_proj_kernel(apply_ln, tiles_per_seq, *refs):
    sb_ref = refs[-1]
    tile = pl.num_programs(0) - 1 - pl.program_id(0)

    @pl.when(lax.rem(tile, tiles_per_seq) == tiles_per_seq - 1)
    def _():
        sb_ref[...] = jnp.zeros_like(sb_ref)

    if apply_ln:
        (x_ref, g_ref, b_ref, w_ref, kdf_ref, kdb_ref, cdb_ref,
         h_ref, qa_ref, kva_ref, ret_ref, gl_ref, kvf_ref, pb_ref, _) = refs
        x = _layer_norm(x_ref[...], g_ref[...], b_ref[...])
        h_ref[...] = x
    else:
        (x_ref, w_ref, kdf_ref, kdb_ref, cdb_ref,
         qa_ref, kva_ref, ret_ref, gl_ref, kvf_ref, pb_ref, _) = refs
        x = x_ref[...]
    xb = x.astype(BF16)

    def proj(lo, hi):
        return _dot(xb, w_ref[:, lo:hi])

    c0 = ATTN_Q_WIDTH
    c1 = c0 + 2 * ATTN_KV_WIDTH
    qa_ref[...] = (proj(0, c0) * (HEAD_DIM ** -0.5 * LOG2E)).astype(BF16)
    kva_ref[...] = proj(c0, c1).astype(BF16)
    ret_ref[:, :RET_QK_WIDTH] = proj(c1, c1 + RET_QK_WIDTH).astype(BF16)
    rk = (proj(c1 + RET_QK_WIDTH, c1 + 2 * RET_QK_WIDTH) * (RET_KEY_DIM ** -0.5)).astype(BF16)
    ret_ref[:, RET_QK_WIDTH:2 * RET_QK_WIDTH] = rk
    rest = proj(c1 + 2 * RET_QK_WIDTH, c1 + RET_WIDTH).astype(BF16)
    ret_ref[:, 2 * RET_QK_WIDTH:] = rest
    gl_ref[...] = (0.5 * proj(c1 + RET_WIDTH, IN_WIDTH)).astype(BF16)

    half = N_RET_HEADS // 2
    mask = _head_block_mask(half * RET_KEY_DIM, half * RET_VAL_DIM, RET_KEY_DIM, RET_VAL_DIM)
    def increment(c, kd_ref):
        rows = slice(c * BLK, (c + 1) * BLK)
        kz = (rk[rows].astype(F32) * kd_ref[...]).T.astype(BF16)
        v = rest[rows, :RET_V_WIDTH]
        comps = []
        for g in range(2):
            full = jnp.where(mask, _dot(kz[g * half * RET_KEY_DIM:(g + 1) * half * RET_KEY_DIM],
                                        v[:, g * half * RET_VAL_DIM:(g + 1) * half * RET_VAL_DIM]), 0.0)
            comp = full[0:RET_KEY_DIM]
            for h in range(1, half):
                comp = comp + full[h * RET_KEY_DIM:(h + 1) * RET_KEY_DIM]
            comps.append(comp)
        return jnp.concatenate(comps, axis=1)

    state = sb_ref[...]
    for c in reversed(range(x.shape[0] // BLK)):
        kvf_ref[c] = increment(c, kdf_ref)
        pb_ref[c] = state
        state = state * cdb_ref[...] + increment(c, kdb_ref)
    sb_ref[...] = state


def _in_proj(x, w, tabs, seq, ln=None):
    t = x.shape[0]
    tm = TOKEN_TILE
    tiles = t // tm
    row = lambda i: (tiles - 1 - i, 0)
    const = lambda i: (0, 0)
    in_specs = [pl.BlockSpec((tm, D_MODEL), row)]
    args = [x]
    out_shape = []
    out_specs = []
    if ln is not None:
        in_specs += [pl.BlockSpec((1, D_MODEL), const), pl.BlockSpec((1, D_MODEL), const)]
        args += [ln[0].reshape(1, D_MODEL), ln[1].reshape(1, D_MODEL)]
        out_shape.append(jax.ShapeDtypeStruct((t, D_MODEL), F32))
        out_specs.append(pl.BlockSpec((tm, D_MODEL), row))
    in_specs += [pl.BlockSpec((D_MODEL, IN_WIDTH), const),
                 pl.BlockSpec((BLK, RET_QK_WIDTH), const), pl.BlockSpec((BLK, RET_QK_WIDTH), const),
                 pl.BlockSpec((1, RET_V_WIDTH), const)]
    args += [w, tabs["kdec_f"], tabs["kdec_b"], tabs["cdec_b"]]
    for width in (ATTN_Q_WIDTH, 2 * ATTN_KV_WIDTH, RET_WIDTH, 2 * D_MODEL):
        out_shape.append(jax.ShapeDtypeStruct((t, width), BF16))
        out_specs.append(pl.BlockSpec((tm, width), row))
    for _ in range(2):
        out_shape.append(jax.ShapeDtypeStruct((t // BLK, RET_KEY_DIM, RET_V_WIDTH), F32))
        out_specs.append(pl.BlockSpec((tm // BLK, RET_KEY_DIM, RET_V_WIDTH), lambda i: (tiles - 1 - i, 0, 0)))
    return pl.pallas_call(
        functools.partial(_in_proj_kernel, ln is not None, seq // tm),
        grid=(tiles,),
        in_specs=in_specs,
        out_specs=out_specs,
        out_shape=out_shape,
        scratch_shapes=[pltpu.VMEM((RET_KEY_DIM, RET_V_WIDTH), F32)],
        compiler_params=_params("arbitrary"),
        name="in_proj",
    )(*args)


def _split_bf16(y):
    hi = y.astype(BF16)
    lo = (y - hi.astype(F32)).astype(BF16)
    return hi, lo


def _mixer_body(n, nb, sink_ref, qa_ref, kvp_ref, kvc_ref, kvn_ref, ret_ref, pf_ref, pb_ref,
                bias_ref, df_ref, db_ref, qdf_ref, qdb_ref, gavg_ref, ya_ref, r_ref):
    q = qa_ref[...]
    kv = jnp.concatenate([kvp_ref[...], kvc_ref[...], kvn_ref[...]], axis=0).astype(F32)
    lo_half = lax.broadcasted_iota(jnp.int32, (kv.shape[0], 2 * HEAD_DIM), 1) < HEAD_DIM

    def lane_variants(a):
        swapped = pltpu.roll(a, HEAD_DIM, 1)
        z = jnp.zeros_like(a)
        return (((jnp.where(lo_half, a, z)).astype(BF16), (jnp.where(lo_half, z, swapped)).astype(BF16)),
                ((jnp.where(lo_half, swapped, z)).astype(BF16), (jnp.where(lo_half, z, a)).astype(BF16)))

    k_var = lane_variants(kv[:, :ATTN_KV_WIDTH])
    v_var = lane_variants(kv[:, ATTN_KV_WIDTH:])
    top_rows = lax.broadcasted_iota(jnp.int32, (2 * BLK, 1), 0) < BLK
    out_lo_half = lax.broadcasted_iota(jnp.int32, (2 * BLK, 2 * HEAD_DIM), 1) < HEAD_DIM
    half_heads = N_RET_HEADS // 2
    k_mask = _head_block_mask(N_RET_HEADS * BLK, RET_QK_WIDTH, BLK, RET_KEY_DIM)
    v_mask = _head_block_mask(half_heads * BLK, half_heads * RET_VAL_DIM, BLK, RET_VAL_DIM)
    s_mask = _head_block_mask(RET_QK_WIDTH, RET_V_WIDTH, RET_KEY_DIM, RET_VAL_DIM)
    pair = 2 * HEAD_DIM
    ys = []
    for j in range(MIX_CHUNKS):
        c = n * MIX_CHUNKS + j
        var = jnp.where(c == 0, 0, jnp.where(c == nb - 1, 2, 1))
        tok = slice(j * BLK, (j + 1) * BLK)
        keys = slice(j * BLK, (j + 3) * BLK)

        for g in range(N_KV_HEADS):
            q2 = jnp.concatenate([q[tok, pair * 2 * g:pair * (2 * g + 1)],
                                  q[tok, pair * (2 * g + 1):pair * (2 * g + 2)]], axis=0)
            s_all = _dot_nt(q2, jnp.concatenate([k_var[g][0][keys], k_var[g][1][keys]], axis=0))
            probs, denoms = [], []
            for r in range(2):
                sink = jnp.where(top_rows, sink_ref[4 * g + r], sink_ref[4 * g + 2 + r])
                s = s_all[:, 3 * BLK * r:3 * BLK * (r + 1)] + bias_ref[var, g, r]
                m = jnp.maximum(jnp.max(s, axis=1, keepdims=True), sink)
                e = jnp.exp2(s - m)
                denoms.append(jnp.sum(e, axis=1, keepdims=True) + jnp.exp2(sink - m))
                probs.append(e.astype(BF16))
            pv = _dot(jnp.concatenate(probs, axis=1),
                      jnp.concatenate([v_var[g][0][keys], v_var[g][1][keys]], axis=0))
            out = pv / jnp.where(out_lo_half, denoms[0], denoms[1])
            ya_ref[tok, pair * 2 * g:pair * (2 * g + 1)] = out[:BLK].astype(BF16)
            ya_ref[tok, pair * (2 * g + 1):pair * (2 * g + 2)] = out[BLK:].astype(BF16)

        rq = ret_ref[tok, :RET_QK_WIDTH]
        rk = ret_ref[tok, RET_QK_WIDTH:2 * RET_QK_WIDTH]
        rv = ret_ref[tok, 2 * RET_QK_WIDTH:2 * RET_QK_WIDTH + RET_V_WIDTH]
        k_exp = jnp.where(k_mask, jnp.concatenate([rk] * N_RET_HEADS, axis=0), jnp.zeros((), BF16))
        qk = _dot_nt(rq, k_exp)
        p_fb = jnp.concatenate([(qk * df_ref[...]).astype(BF16), (qk * db_ref[...]).astype(BF16)], axis=0)
        intra = []
        for g in range(2):
            cols = slice(g * half_heads * RET_VAL_DIM, (g + 1) * half_heads * RET_VAL_DIM)
            v_bd = jnp.where(v_mask, jnp.concatenate([rv[:, cols]] * half_heads, axis=0), jnp.zeros((), BF16))
            intra.append(_dot(p_fb[:, g * half_heads * BLK:(g + 1) * half_heads * BLK], v_bd))
        cross = []
        for state_ref, qdec_ref in ((pf_ref, qdf_ref), (pb_ref, qdb_ref)):
            s_bd = jnp.where(s_mask, jnp.concatenate([state_ref[j]] * N_RET_HEADS, axis=0), 0.0)
            cross.append(_dot(rq, s_bd.astype(BF16)) * qdec_ref[...])
        ys.append(jnp.concatenate(intra, axis=1) + jnp.concatenate(cross, axis=0))

    y = jnp.concatenate(ys, axis=0)
    def group_mean(a):
        ab = a.astype(BF16)
        half = gavg_ref.shape[0]
        return jnp.concatenate([_dot(ab[:, k * half:(k + 1) * half], gavg_ref[...])
                                for k in range(RET_V_WIDTH // half)], axis=1)

    d = y - group_mean(y)
    normed = d * lax.rsqrt(group_mean(d * d) + GN_EPS)
    for j in range(MIX_CHUNKS):
        tok = slice(j * BLK, (j + 1) * BLK)
        g_f = ret_ref[tok, 2 * RET_QK_WIDTH + RET_V_WIDTH:2 * RET_QK_WIDTH + 2 * RET_V_WIDTH].astype(F32)
        g_b = ret_ref[tok, 2 * RET_QK_WIDTH + 2 * RET_V_WIDTH:].astype(F32)
        r = (g_f * _sigmoid(g_f) * normed[2 * j * BLK:(2 * j + 1) * BLK]
             + g_b * _sigmoid(g_b) * normed[(2 * j + 1) * BLK:(2 * j + 2) * BLK])
        r_ref[tok, :] = r.astype(BF16)


def _mix_proj_kernel(nb, n_blocks, sink_ref, qa_ref, kvp_ref, kvc_ref, kvn_ref, ret_ref, kvf_ref, pb_ref,
                     bias_ref, df_ref, db_ref, qdf_ref, qdb_ref, gavg_ref, cdf_ref,
                     h_ref, gl_ref, bg_ref, woa_ref, wor_ref, wout_ref, g_ref, b_ref,
                     x_ref, ya_scr, r_scr, sf_ref, pf_ref):
    s = pl.program_id(0)

    n = lax.rem(jnp.minimum(s, n_blocks - 1), nb // MIX_CHUNKS)

    @pl.when(s == 0)
    def _():
        ya_scr[...] = jnp.zeros_like(ya_scr)
        r_scr[...] = jnp.zeros_like(r_scr)

    @pl.when(n == 0)
    def _():
        sf_ref[...] = jnp.zeros_like(sf_ref)

    slot = lax.rem(s, 2)
    ya_prev = ya_scr[1 - slot]
    r_prev = r_scr[1 - slot]
    t = jnp.tanh(gl_ref[...].astype(F32) + bg_ref[...])
    pa = _dot(ya_prev, woa_ref[...])
    pr = _dot(r_prev, wor_ref[...])
    merged = pa + pr + t[:, :D_MODEL] * pa + t[:, D_MODEL:] * pr
    mix = _dot(merged.astype(BF16), wout_ref[...])
    x_ref[...] = _layer_norm(DEEPNORM_ALPHA * h_ref[...] + mix, g_ref[...], b_ref[...])

    state = sf_ref[...]
    for j in range(MIX_CHUNKS):
        pf_ref[j] = state
        state = state * cdf_ref[...] + kvf_ref[j]
    sf_ref[...] = state

    _mixer_body(n, nb, sink_ref, qa_ref, kvp_ref, kvc_ref, kvn_ref, ret_ref, pf_ref, pb_ref,
                bias_ref, df_ref, db_ref, qdf_ref, qdb_ref, gavg_ref, ya_scr.at[slot], r_scr.at[slot])


def _mix_proj(qa, kva, ret, kvf, prev_b, sink, tabs, h, gl, b_gate, woa, wor, wout, g, b, batch, nb):
    t = qa.shape[0]
    steps = nb // MIX_CHUNKS
    rows = MIX_CHUNKS * BLK
    n_blocks = batch * steps

    def mixed(s):
        return jnp.minimum(s, n_blocks - 1)

    def projected(s):
        return jnp.maximum(s - 1, 0)

    cur = lambda s: (mixed(s), 0)
    prv = lambda s: (mixed(s) * MIX_CHUNKS - jnp.where(lax.rem(mixed(s), steps) == 0, 0, 1), 0)
    nxt = lambda s: (mixed(s) * MIX_CHUNKS + MIX_CHUNKS - jnp.where(lax.rem(mixed(s), steps) == steps - 1, 1, 0), 0)
    out = lambda s: (projected(s), 0)
    c2 = lambda s: (0, 0)
    state_spec = pl.BlockSpec((MIX_CHUNKS, RET_KEY_DIM, RET_V_WIDTH), lambda s: (mixed(s), 0, 0))
    return pl.pallas_call(
        functools.partial(_mix_proj_kernel, nb, n_blocks),
        grid=(n_blocks + 1,),
        in_specs=[
            pl.BlockSpec(memory_space=pltpu.SMEM),
            pl.BlockSpec((rows, ATTN_Q_WIDTH), cur),
            pl.BlockSpec((BLK, 2 * ATTN_KV_WIDTH), prv),
            pl.BlockSpec((rows, 2 * ATTN_KV_WIDTH), cur),
            pl.BlockSpec((BLK, 2 * ATTN_KV_WIDTH), nxt),
            pl.BlockSpec((rows, RET_WIDTH), cur),
            state_spec,
            state_spec,
            pl.BlockSpec((3, N_KV_HEADS, 2, 2 * BLK, 3 * BLK), lambda s: (0, 0, 0, 0, 0)),
            pl.BlockSpec((BLK, N_RET_HEADS * BLK), c2),
            pl.BlockSpec((BLK, N_RET_HEADS * BLK), c2),
            pl.BlockSpec((BLK, RET_V_WIDTH), c2),
            pl.BlockSpec((BLK, RET_V_WIDTH), c2),
            pl.BlockSpec((GN_BLOCK, GN_BLOCK), c2),
            pl.BlockSpec((1, RET_V_WIDTH), c2),
            pl.BlockSpec((rows, D_MODEL), out),
            pl.BlockSpec((rows, 2 * D_MODEL), out),
            pl.BlockSpec((1, 2 * D_MODEL), c2),
            pl.BlockSpec((ATTN_Q_WIDTH, D_MODEL), c2),
            pl.BlockSpec((RET_V_WIDTH, D_MODEL), c2),
            pl.BlockSpec((D_MODEL, D_MODEL), c2),
            pl.BlockSpec((1, D_MODEL), c2),
            pl.BlockSpec((1, D_MODEL), c2),
        ],
        out_specs=pl.BlockSpec((rows, D_MODEL), out),
        out_shape=jax.ShapeDtypeStruct((t, D_MODEL), F32),
        scratch_shapes=[pltpu.VMEM((2, rows, ATTN_Q_WIDTH), BF16), pltpu.VMEM((2, rows, RET_V_WIDTH), BF16),
                        pltpu.VMEM((RET_KEY_DIM, RET_V_WIDTH), F32),
                        pltpu.VMEM((MIX_CHUNKS, RET_KEY_DIM, RET_V_WIDTH), F32)],
        compiler_params=_params("arbitrary"),
        name="mix_proj",
    )(sink, qa, kva, kva, kva, ret, kvf, prev_b, tabs["attn_bias"], tabs["intra_f"],
      tabs["intra_b"], tabs["qdec_f"], tabs["qdec_b"], tabs["group_avg"], tabs["cdec_f"],
      h, gl, b_gate.reshape(1, -1), woa, wor, wout, g.reshape(1, -1), b.reshape(1, -1))


def _ffn_kernel(x_ref, w1_ref, w3_ref, w2_ref, g_ref, b_ref, o_ref):
    x = x_ref[...]
    xb = x.astype(BF16)
    hidden = []
    for c in range(D_FF_DENSE // FF_CHUNK):
        cols = slice(c * FF_CHUNK, (c + 1) * FF_CHUNK)
        a = _dot(xb, w1_ref[:, cols])
        hidden.append((a * _sigmoid(a) * _dot(xb, w3_ref[:, cols])).astype(BF16))
    acc = _dot(jnp.concatenate(hidden, axis=1), w2_ref[...])
    o_ref[...] = _layer_norm(DEEPNORM_ALPHA * x + acc, g_ref[...], b_ref[...])


def _ffn(x, w1, w3, w2, g, b):
    t = x.shape[0]
    tm = FFN_TOKEN_TILE
    row = lambda i: (i, 0)
    const = lambda i: (0, 0)
    return pl.pallas_call(
        _ffn_kernel,
        grid=(t // tm,),
        in_specs=[
            pl.BlockSpec((tm, D_MODEL), row),
            pl.BlockSpec((D_MODEL, D_FF_DENSE), const),
            pl.BlockSpec((D_MODEL, D_FF_DENSE), const),
            pl.BlockSpec((D_FF_DENSE, D_MODEL), const),
            pl.BlockSpec((1, D_MODEL), const),
            pl.BlockSpec((1, D_MODEL), const),
        ],
        out_specs=pl.BlockSpec((tm, D_MODEL), row),
        out_shape=jax.ShapeDtypeStruct((t, D_MODEL), F32),
        compiler_params=_params("parallel"),
        name="ffn",
    )(x, w1, w3, w2, g.reshape(1, -1), b.reshape(1, -1))


def _route_kernel(x_ref, rw_ref, rb_ref, tri_ref, xb_ref, tcol_ref, pcol_ref, drow_ref, cnt_ref):
    x = x_ref[...]
    tm, lanes = x.shape[0], LANES
    lane = lax.broadcasted_iota(jnp.int32, (tm, lanes), 1)
    xh, xl = _split_bf16(x)
    wh, wl = _split_bf16(rw_ref[...])
    hi_part = _dot(xh, jnp.concatenate([wh, wl], axis=1))
    logits = hi_part[:, :lanes] + hi_part[:, lanes:] + _dot(xl, wh) + rb_ref[...]
    logits = jnp.where(lane < N_EXPERTS, logits, NEG_INF)
    m1 = jnp.max(logits, axis=1, keepdims=True)
    i1 = jnp.min(jnp.where(logits == m1, lane, lanes), axis=1, keepdims=True)
    rest = jnp.where(lane == i1, NEG_INF, logits)
    m2 = jnp.max(rest, axis=1, keepdims=True)
    i2 = jnp.min(jnp.where(rest == m2, lane, lanes), axis=1, keepdims=True)
    e2 = jnp.exp(m2 - m1)
    w_top = 1.0 / (1.0 + e2)
    gate =jnp.where(lane == i1, w_top, 0.0) + jnp.where(lane == i2, e2 * w_top, 0.0)
    sel = jnp.where(lane == i1, 1.0, jnp.where(lane == i2, 1.0, 0.0))
    sel_b = sel.astype(BF16)
    blk = tri_ref.shape[0]
    parts, above = [], jnp.zeros((1, lanes), F32)
    for r0 in range(0, tm, blk):
        local = _dot(tri_ref[...], sel_b[r0:r0 + blk])
        parts.append(local + above)
        above = above + local[blk - 1:blk]
    prefix = jnp.concatenate(parts, axis=0)
    slot1 = jnp.where(sel > 0.0, prefix, 0.0)
    hi = jnp.floor(slot1 * (1.0 / SLOT_RADIX))
    tcol_ref[:, :lanes] = gate.astype(BF16)
    tcol_ref[:, lanes:2 * lanes] = hi.astype(BF16)
    tcol_ref[:, 2 * lanes:] = (slot1 - SLOT_RADIX * hi).astype(BF16)

    def first_slab_pos(idx, picked):
        rank = jnp.sum(jnp.where(picked, prefix, 0.0), axis=1, keepdims=True) - 1.0
        return jnp.where(rank < MOE_SLAB, idx.astype(F32) * MOE_SLAB + rank, -1.0)

    pcol_ref[...] = jnp.where(
        lane == 0, first_slab_pos(i1, lane == i1), jnp.where(
            lane == 1, w_top, jnp.where(
                lane == 2, first_slab_pos(i2, lane == i2), jnp.where(lane == 3, e2 * w_top, 0.0))))
    drow_ref[...] = (slot1 - 1.0).T[:N_EXPERTS]
    cnt_ref[...] = prefix[tm - 1:tm].astype(jnp.int32)
    xb_ref[...] = x.astype(BF16)


def _route(x, router_w, router_b):
    t = x.shape[0]
    tm = MOE_TOKEN_TILE
    lanes = 128
    rw = jnp.zeros((D_MODEL, lanes), F32).at[:, :N_EXPERTS].set(router_w)
    rb = jnp.zeros((1, lanes), F32).at[0, :N_EXPERTS].set(router_b)
    tri = jnp.asarray(np.arange(ROUTE_PREFIX_BLOCK)[:, None] >= np.arange(ROUTE_PREFIX_BLOCK)[None, :], BF16)
    row = lambda i: (i, 0)
    const = lambda i: (0, 0)
    return pl.pallas_call(
        _route_kernel,
        grid=(t // tm,),
        in_specs=[
            pl.BlockSpec((tm, D_MODEL), row),
            pl.BlockSpec((D_MODEL, lanes), const),
            pl.BlockSpec((1, lanes), const),
            pl.BlockSpec((ROUTE_PREFIX_BLOCK, ROUTE_PREFIX_BLOCK), const),
        ],
        out_specs=[
            pl.BlockSpec((tm, D_MODEL), row),
            pl.BlockSpec((tm, 3 * lanes), row),
            pl.BlockSpec((tm, lanes), row),
            pl.BlockSpec((N_EXPERTS, tm), row),
            pl.BlockSpec((None, 1, lanes), lambda i: (i, 0, 0)),
        ],
        out_shape=[
            jax.ShapeDtypeStruct((t, D_MODEL), BF16),
            jax.ShapeDtypeStruct((t, 3 * lanes), BF16),
            jax.ShapeDtypeStruct((t, lanes), F32),
            jax.ShapeDtypeStruct((t // tm * N_EXPERTS, tm), F32),
            jax.ShapeDtypeStruct((t // tm, 1, lanes), jnp.int32),
        ],
        compiler_params=_params("parallel"),
        name="route",
    )(x, rw, rb, tri)


def _moe_kernel(cnt_ref, x_ref, xb_ref, tcol_ref, pcol_ref, drow_ref, w1_ref, w3_ref, w2_ref,
                g_ref, b_ref, o_ref, y_ref):
    i = pl.program_id(0)
    e = pl.program_id(1)
    count = cnt_ref[i * N_EXPERTS + e]

    @pl.when(e == 0)
    def _():
        o_ref[...] = jnp.zeros_like(o_ref)

    slot_row = drow_ref[pl.ds(e, 1), :]

    def expert_ffn(base, n_rows):
        rows = lax.broadcasted_iota(jnp.int32, (n_rows, 1), 0).astype(F32)
        pick = jnp.where(slot_row == rows + base, 1.0, 0.0).astype(BF16)
        xs = _dot(pick, xb_ref[...]).astype(BF16)
        main = D_FF_EXPERT - LANES
        gate = _dot(xs, w1_ref[:, :main])
        up = _dot(xs, w3_ref[:, :main])
        tail = _dot(xs, jnp.concatenate([w1_ref[:, main:], w3_ref[:, main:]], axis=1))
        gate_t, up_t = tail[:, :LANES], tail[:, LANES:]
        hid = jnp.concatenate([gate * _sigmoid(gate) * up, gate_t * _sigmoid(gate_t) * up_t], axis=1)
        return _dot(hid.astype(BF16), w2_ref[...]).astype(BF16)

    parked = pl.ds(pl.multiple_of(e * MOE_SLAB, 16), MOE_SLAB)

    @pl.when(count <= MOE_SLAB_SMALL)
    def _():
        y_ref[parked, :] = jnp.concatenate(
            [expert_ffn(0.0, MOE_SLAB_SMALL), jnp.zeros((MOE_SLAB - MOE_SLAB_SMALL, D_MODEL), BF16)], axis=0)

    @pl.when(count > MOE_SLAB_SMALL)
    def _():
        y_ref[parked, :] = expert_ffn(0.0, MOE_SLAB)

    def later_slab(s, carry):
        base = (s * MOE_SLAB).astype(F32)
        pick_e = jnp.where(lax.broadcasted_iota(jnp.int32, (LANES, LANES), 0) == e, 1.0, 0.0).astype(BF16)
        gate_b = _dot(tcol_ref[:, :LANES], pick_e)
        slot_b = (SLOT_RADIX * _dot(tcol_ref[:, LANES:2 * LANES], pick_e)
                  + _dot(tcol_ref[:, 2 * LANES:], pick_e) - 1.0)
        cols = lax.broadcasted_iota(jnp.int32, (1, LANES), 1).astype(F32)
        y = jnp.concatenate([expert_ffn(base, MOE_SLAB),
                             jnp.zeros((MOE_SLAB_K - MOE_SLAB, D_MODEL), BF16)], axis=0)
        put = jnp.concatenate(
            [jnp.where(slot_b == cols + (base + k * LANES), gate_b, 0.0).astype(BF16)
             for k in range(MOE_SLAB_K // LANES)], axis=1)
        o_ref[...] += _dot(put, y)
        return carry

    n_slabs = (count + MOE_SLAB - 1) // MOE_SLAB
    lax.fori_loop(1, n_slabs, later_slab, 0)

    @pl.when(e == N_EXPERTS - 1)
    def _():
        p = pcol_ref[...]
        pos_a, gate_a, pos_b, gate_b = p[:, 0:1], p[:, 1:2], p[:, 2:3], p[:, 3:4]
        puts = []
        for k in range(N_EXPERTS * MOE_SLAB // MOE_COMBINE_K):
            cols = lax.broadcasted_iota(jnp.int32, (1, MOE_COMBINE_K), 1).astype(F32) + float(k * MOE_COMBINE_K)
            puts.append(jnp.where(pos_a == cols, gate_a, jnp.where(pos_b == cols, gate_b, 0.0)).astype(BF16))
        acc = DEEPNORM_ALPHA * x_ref[...] + o_ref[...] + _dot(jnp.concatenate(puts, axis=1), y_ref[...])
        o_ref[...] = _layer_norm(acc, g_ref[...], b_ref[...])


def _moe(x, router_w, router_b, w1, w3, w2, g, b):
    t = x.shape[0]
    tm = MOE_TOKEN_TILE
    xb, tcol, pcol, drow, cnt = _route(x, router_w, router_b)
    counts = cnt[:, 0, :N_EXPERTS].reshape(-1)
    row = lambda i, e, c: (i, 0)
    const = lambda i, e, c: (0, 0)
    expert = lambda i, e, c: (e, 0, 0)
    grid_spec = pltpu.PrefetchScalarGridSpec(
        num_scalar_prefetch=1,
        grid=(t // tm, N_EXPERTS),
        in_specs=[
            pl.BlockSpec((tm, D_MODEL), row),
            pl.BlockSpec((tm, D_MODEL), row),
            pl.BlockSpec((tm, 3 * LANES), row),
            pl.BlockSpec((tm, LANES), row),
            pl.BlockSpec((N_EXPERTS, tm), row),
            pl.BlockSpec((None, D_MODEL, D_FF_EXPERT), expert),
            pl.BlockSpec((None, D_MODEL, D_FF_EXPERT), expert),
            pl.BlockSpec((None, D_FF_EXPERT, D_MODEL), expert),
            pl.BlockSpec((1, D_MODEL), const),
            pl.BlockSpec((1, D_MODEL), const),
        ],
        out_specs=pl.BlockSpec((tm, D_MODEL), row),
        scratch_shapes=[pltpu.VMEM((N_EXPERTS * MOE_SLAB, D_MODEL), BF16)],
    )
    return pl.pallas_call(
        _moe_kernel,
        grid_spec=grid_spec,
        out_shape=jax.ShapeDtypeStruct((t, D_MODEL), F32),
        compiler_params=_params("parallel", "arbitrary"),
        name="moe",
    )(counts, x, xb, tcol, pcol, drow, w1, w3, w2, g.reshape(1, -1), b.reshape(1, -1))


def _attn_bias_table():
    qi = np.arange(BLK)[:, None]
    kj = np.arange(3 * BLK)[None, :]
    dist = np.abs(qi - kj + BLK)
    slopes = np.exp2(-8.0 * np.arange(1, N_Q_HEADS + 1) / N_Q_HEADS)
    bias = -(LOG2E * slopes)[:, None, None] * dist[None]
    in_window = (dist <= WINDOW)[None]
    variants = []
    for ok in (kj >= BLK, kj >= 0, kj < 2 * BLK):
        per_head = np.where(in_window & ok[None], bias, NEG_INF)
        per_head = per_head.reshape(N_KV_HEADS, 2, 2, BLK, 3 * BLK)
        variants.append(np.transpose(per_head, (0, 2, 1, 3, 4)).reshape(N_KV_HEADS, 2, 2 * BLK, 3 * BLK))
    return np.stack(variants).astype(np.float32)


def _retention_tables(decay_fwd, decay_bwd):
    pos = jnp.arange(BLK, dtype=F32)
    diff = jnp.arange(BLK)[:, None] - jnp.arange(BLK)[None, :]
    lg_f = jax.nn.log_sigmoid(decay_fwd.astype(F32))
    lg_b = jax.nn.log_sigmoid(decay_bwd.astype(F32))

    def intra(lg, d, mask):
        dec = jnp.where(mask[None], jnp.exp(jnp.maximum(d, 0).astype(F32)[None] * lg[:, None, None]), 0.0)
        return jnp.transpose(dec, (1, 0, 2)).reshape(BLK, N_RET_HEADS * BLK)

    def per_pos(lg, expo, width):
        return jnp.repeat(jnp.exp(expo[:, None] * lg[None, :]), width, axis=1)

    return {
        "intra_f": intra(lg_f, diff, diff >= 0),
        "intra_b": intra(lg_b, -diff, diff < 0),
        "qdec_f": per_pos(lg_f, pos + 1.0, RET_VAL_DIM),
        "qdec_b": per_pos(lg_b, BLK - pos, RET_VAL_DIM),
        "kdec_f": per_pos(lg_f, BLK - 1.0 - pos, RET_KEY_DIM),
        "kdec_b": per_pos(lg_b, pos, RET_KEY_DIM),
        "cdec_f": jnp.repeat(jnp.exp(BLK * lg_f), RET_VAL_DIM)[None, :],
        "cdec_b": jnp.repeat(jnp.exp(BLK * lg_b), RET_VAL_DIM)[None, :],
    }


def _group_avg_matrix():
    head = np.arange(GN_BLOCK) // RET_VAL_DIM
    return jnp.asarray(np.where(head[:, None] == head[None, :], 1.0 / RET_VAL_DIM, 0.0), BF16)


def kernel(x, ln_emb_g, ln_emb_b, w_in, b_gate, sink_logits, decay_fwd, decay_bwd,
           w_o_attn, w_o_ret, w_out, ln1_g, ln1_b, ffn_w1, ffn_w3, ffn_w2,
           router_w, router_b, moe_w1, moe_w3, moe_w2, ln2_g, ln2_b):
    batch, seq, _ = x.shape
    nb = seq // BLK
    xt = x.reshape(batch * seq, D_MODEL)
    attn_bias = _attn_bias_table()
    group_avg = _group_avg_matrix()
    for layer in range(DEPTH):
        tabs = _retention_tables(decay_fwd[layer], decay_bwd[layer])
        tabs["attn_bias"] = attn_bias
        tabs["group_avg"] = group_avg
        w = w_in[layer].astype(BF16)
        if layer == 0:
            h, qa, kva, ret, gl, kvf, prev_b = _in_proj(xt, w, tabs, seq, ln=(ln_emb_g, ln_emb_b))
        else:
            h = xt
            qa, kva, ret, gl, kvf, prev_b = _in_proj(xt, w, tabs, seq)
        x1 = _mix_proj(qa, kva, ret, kvf, prev_b, LOG2E * sink_logits[layer].astype(F32), tabs,
                       h, gl, 0.5 * b_gate[layer], w_o_attn[layer].astype(BF16),
                       w_o_ret[layer].astype(BF16), (0.5 * w_out[layer]).astype(BF16),
                       ln1_g[layer], ln1_b[layer], batch, nb)
        i = layer // 2
        if layer % 2 == 0:
            xt = _ffn(x1, ffn_w1[i].astype(BF16), ffn_w3[i].astype(BF16), ffn_w2[i].astype(BF16),
                      ln2_g[layer], ln2_b[layer])
        else:
            xt = _moe(x1, router_w[i], router_b[i], moe_w1[i].astype(BF16), moe_w3[i].astype(BF16),
                      moe_w2[i].astype(BF16), ln2_g[layer], ln2_b[layer])
    return xt.reshape(batch, seq, D_MODEL)
```

```python
import functools

import jax
import jax.numpy as jnp
import numpy as np
from jax import lax
from jax.experimental import pallas as pl
from jax.experimental.pallas import tpu as pltpu

F32 = jnp.float32
BF16 = jnp.bfloat16

D_MODEL = 1024
DEPTH = 2
N_Q_HEADS = 8
N_KV_HEADS = 2
HEAD_DIM = 64
WINDOW = 128
BLK = 128
ATTN_Q_WIDTH = N_Q_HEADS * HEAD_DIM
ATTN_KV_WIDTH = N_KV_HEADS * HEAD_DIM
N_RET_HEADS = 8
RET_KEY_DIM = 32
RET_VAL_DIM = 64
RET_QK_WIDTH = N_RET_HEADS * RET_KEY_DIM
RET_V_WIDTH = N_RET_HEADS * RET_VAL_DIM
RET_WIDTH = 2 * RET_QK_WIDTH + 3 * RET_V_WIDTH
D_FF_DENSE = 2816
N_EXPERTS = 8
D_FF_EXPERT = 1408
IN_WIDTH = ATTN_Q_WIDTH + 2 * ATTN_KV_WIDTH + RET_WIDTH + 2 * D_MODEL
DEEPNORM_ALPHA = (2 * DEPTH) ** 0.25
LN_EPS = 1e-5
GN_EPS = 1e-5
NEG_INF = -1e30
LOG2E = 1.4426950408889634

TOKEN_TILE = 1024
FFN_TOKEN_TILE = 1024
MOE_TOKEN_TILE = 1024
ROUTE_PREFIX_BLOCK = 256
MOE_SLAB = 288
MOE_SLAB_SMALL = 256
MOE_SLAB_K = 384
MOE_COMBINE_K = 256
GN_BLOCK = 256
LANES = 128
SLOT_RADIX = 32.0
MIX_CHUNKS = 4
FF_CHUNK = 256
VMEM_LIMIT_BYTES = 56 * 1024 * 1024


def _params(*sem):
    return pltpu.CompilerParams(dimension_semantics=sem, vmem_limit_bytes=VMEM_LIMIT_BYTES)


def _layer_norm(x, g, b):
    mu = jnp.mean(x, axis=-1, keepdims=True)
    d = x - mu
    var = jnp.mean(d * d, axis=-1, keepdims=True)
    return d * lax.rsqrt(var + LN_EPS) * g + b


def _sigmoid(x):
    return 1.0 / (1.0 + jnp.exp(-x))


def _dot(a, b):
    return jnp.dot(a, b, preferred_element_type=F32)


def _dot_nt(a, b):
    return lax.dot_general(a, b, (((1,), (1,)), ((), ())), preferred_element_type=F32)


def _head_block_mask(rows, cols, row_div, col_div):
    r = lax.broadcasted_iota(jnp.int32, (rows, cols), 0) // row_div
    c = lax.broadcasted_iota(jnp.int32, (rows, cols), 1) // col_div
    return r == c


def _in_proj_kernel(apply_ln, tiles_per_seq, *refs):
    sb_ref = refs[-1]
    tile = pl.num_programs(0) - 1 - pl.program_id(0)

    @pl.when(lax.rem(tile, tiles_per_seq) == tiles_per_seq - 1)
    def _():
        sb_ref[...] = jnp.zeros_like(sb_ref)

    if apply_ln:
        (x_ref, g_ref, b_ref, w_ref, kdf_ref, kdb_ref, cdb_ref,
         h_ref, qa_ref, kva_ref, ret_ref, gl_ref, kvf_ref, pb_ref, _) = refs
        x = _layer_norm(x_ref[...], g_ref[...], b_ref[...])
        h_ref[...] = x
    else:
        (x_ref, w_ref, kdf_ref, kdb_ref, cdb_ref,
         qa_ref, kva_ref, ret_ref, gl_ref, kvf_ref, pb_ref, _) = refs
        x = x_ref[...]
    xb = x.astype(BF16)

    def proj(lo, hi):
        return _dot(xb, w_ref[:, lo:hi])

    c0 = ATTN_Q_WIDTH
    c1 = c0 + 2 * ATTN_KV_WIDTH
    qa_ref[...] = (proj(0, c0) * (HEAD_DIM ** -0.5 * LOG2E)).astype(BF16)
    kva_ref[...] = proj(c0, c1).astype(BF16)
    ret_ref[:, :RET_QK_WIDTH] = proj(c1, c1 + RET_QK_WIDTH).astype(BF16)
    rk = (proj(c1 + RET_QK_WIDTH, c1 + 2 * RET_QK_WIDTH) * (RET_KEY_DIM ** -0.5)).astype(BF16)
    ret_ref[:, RET_QK_WIDTH:2 * RET_QK_WIDTH] = rk
    rest = proj(c1 + 2 * RET_QK_WIDTH, c1 + RET_WIDTH).astype(BF16)
    ret_ref[:, 2 * RET_QK_WIDTH:] = rest
    gl_ref[...] = (0.5 * proj(c1 + RET_WIDTH, IN_WIDTH)).astype(BF16)

    half = N_RET_HEADS // 2
    mask = _head_block_mask(half * RET_KEY_DIM, half * RET_VAL_DIM, RET_KEY_DIM, RET_VAL_DIM)
    def increment(c, kd_ref):
        rows = slice(c * BLK, (c + 1) * BLK)
        kz = (rk[rows].astype(F32) * kd_ref[...]).T.astype(BF16)
        v = rest[rows, :RET_V_WIDTH]
        comps = []
        for g in range(2):
            full = jnp.where(mask, _dot(kz[g * half * RET_KEY_DIM:(g + 1) * half * RET_KEY_DIM],
                                        v[:, g * half * RET_VAL_DIM:(g + 1) * half * RET_VAL_DIM]), 0.0)
            comp = full[0:RET_KEY_DIM]
            for h in range(1, half):
                comp = comp + full[h * RET_KEY_DIM:(h + 1) * RET_KEY_DIM]
            comps.append(comp)
        return jnp.concatenate(comps, axis=1)

    state = sb_ref[...]
    for c in reversed(range(x.shape[0] // BLK)):
        kvf_ref[c] = increment(c, kdf_ref)
        pb_ref[c] = state
        state = state * cdb_ref[...] + increment(c, kdb_ref)
    sb_ref[...] = state


def _in_proj(x, w, tabs, seq, ln=None):
    t = x.shape[0]
    tm = TOKEN_TILE
    tiles = t // tm
    row = lambda i: (tiles - 1 - i, 0)
    const = lambda i: (0, 0)
    in_specs = [pl.BlockSpec((tm, D_MODEL), row)]
    args = [x]
    out_shape = []
    out_specs = []
    if ln is not None:
        in_specs += [pl.BlockSpec((1, D_MODEL), const), pl.BlockSpec((1, D_MODEL), const)]
        args += [ln[0].reshape(1, D_MODEL), ln[1].reshape(1, D_MODEL)]
        out_shape.append(jax.ShapeDtypeStruct((t, D_MODEL), F32))
        out_specs.append(pl.BlockSpec((tm, D_MODEL), row))
    in_specs += [pl.BlockSpec((D_MODEL, IN_WIDTH), const),
                 pl.BlockSpec((BLK, RET_QK_WIDTH), const), pl.BlockSpec((BLK, RET_QK_WIDTH), const),
                 pl.BlockSpec((1, RET_V_WIDTH), const)]
    args += [w, tabs["kdec_f"], tabs["kdec_b"], tabs["cdec_b"]]
    for width in (ATTN_Q_WIDTH, 2 * ATTN_KV_WIDTH, RET_WIDTH, 2 * D_MODEL):
        out_shape.append(jax.ShapeDtypeStruct((t, width), BF16))
        out_specs.append(pl.BlockSpec((tm, width), row))
    for _ in range(2):
        out_shape.append(jax.ShapeDtypeStruct((t // BLK, RET_KEY_DIM, RET_V_WIDTH), F32))
        out_specs.append(pl.BlockSpec((tm // BLK, RET_KEY_DIM, RET_V_WIDTH), lambda i: (tiles - 1 - i, 0, 0)))
    return pl.pallas_call(
        functools.partial(_in_proj_kernel, ln is not None, seq // tm),
        grid=(tiles,),
        in_specs=in_specs,
        out_specs=out_specs,
        out_shape=out_shape,
        scratch_shapes=[pltpu.VMEM((RET_KEY_DIM, RET_V_WIDTH), F32)],
        compiler_params=_params("arbitrary"),
        name="in_proj",
    )(*args)


def _split_bf16(y):
    hi = y.astype(BF16)
    lo = (y - hi.astype(F32)).astype(BF16)
    return hi, lo


def _mixer_body(n, nb, sink_ref, qa_ref, kvp_ref, kvc_ref, kvn_ref, ret_ref, pf_ref, pb_ref,
                bias_ref, df_ref, db_ref, qdf_ref, qdb_ref, gavg_ref, ya_ref, r_ref):
    q = qa_ref[...]
    kv = jnp.concatenate([kvp_ref[...], kvc_ref[...], kvn_ref[...]], axis=0).astype(F32)
    lo_half = lax.broadcasted_iota(jnp.int32, (kv.shape[0], 2 * HEAD_DIM), 1) < HEAD_DIM

    def lane_variants(a):
        swapped = pltpu.roll(a, HEAD_DIM, 1)
        z = jnp.zeros_like(a)
        return (((jnp.where(lo_half, a, z)).astype(BF16), (jnp.where(lo_half, z, swapped)).astype(BF16)),
                ((jnp.where(lo_half, swapped, z)).astype(BF16), (jnp.where(lo_half, z, a)).astype(BF16)))

    k_var = lane_variants(kv[:, :ATTN_KV_WIDTH])
    v_var = lane_variants(kv[:, ATTN_KV_WIDTH:])
    top_rows = lax.broadcasted_iota(jnp.int32, (2 * BLK, 1), 0) < BLK
    out_lo_half = lax.broadcasted_iota(jnp.int32, (2 * BLK, 2 * HEAD_DIM), 1) < HEAD_DIM
    half_heads = N_RET_HEADS // 2
    k_mask = _head_block_mask(N_RET_HEADS * BLK, RET_QK_WIDTH, BLK, RET_KEY_DIM)
    v_mask = _head_block_mask(half_heads * BLK, half_heads * RET_VAL_DIM, BLK, RET_VAL_DIM)
    s_mask = _head_block_mask(RET_QK_WIDTH, RET_V_WIDTH, RET_KEY_DIM, RET_VAL_DIM)
    pair = 2 * HEAD_DIM

    def group_mean(a):
        ab = a.astype(BF16)
        half = gavg_ref.shape[0]
        return jnp.concatenate([_dot(ab[:, k * half:(k + 1) * half], gavg_ref[...])
                                for k in range(RET_V_WIDTH // half)], axis=1)

    for j in range(MIX_CHUNKS):
        c = n * MIX_CHUNKS + j
        var = jnp.where(c == 0, 0, jnp.where(c == nb - 1, 2, 1))
        tok = slice(j * BLK, (j + 1) * BLK)
        keys = slice(j * BLK, (j + 3) * BLK)

        for g in range(N_KV_HEADS):
            q2 = jnp.concatenate([q[tok, pair * 2 * g:pair * (2 * g + 1)],
                                  q[tok, pair * (2 * g + 1):pair * (2 * g + 2)]], axis=0)
            s_all = _dot_nt(q2, jnp.concatenate([k_var[g][0][keys], k_var[g][1][keys]], axis=0))
            probs, denoms = [], []
            for r in range(2):
                sink = jnp.where(top_rows, sink_ref[4 * g + r], sink_ref[4 * g + 2 + r])
                s = s_all[:, 3 * BLK * r:3 * BLK * (r + 1)] + bias_ref[var, g, r]
                m = jnp.maximum(jnp.max(s, axis=1, keepdims=True), sink)
                e = jnp.exp2(s - m)
                denoms.append(jnp.sum(e, axis=1, keepdims=True) + jnp.exp2(sink - m))
                probs.append(e.astype(BF16))
            pv = _dot(jnp.concatenate(probs, axis=1),
                      jnp.concatenate([v_var[g][0][keys], v_var[g][1][keys]], axis=0))
            out = pv / jnp.where(out_lo_half, denoms[0], denoms[1])
            ya_ref[tok, pair * 2 * g:pair * (2 * g + 1)] = out[:BLK].astype(BF16)
            ya_ref[tok, pair * (2 * g + 1):pair * (2 * g + 2)] = out[BLK:].astype(BF16)

        rq = ret_ref[tok, :RET_QK_WIDTH]
        rk = ret_ref[tok, RET_QK_WIDTH:2 * RET_QK_WIDTH]
        rv = ret_ref[tok, 2 * RET_QK_WIDTH:2 * RET_QK_WIDTH + RET_V_WIDTH]
        k_exp = jnp.where(k_mask, jnp.concatenate([rk] * N_RET_HEADS, axis=0), jnp.zeros((), BF16))
        qk = _dot_nt(rq, k_exp)
        p_fb = jnp.concatenate([(qk * df_ref[...]).astype(BF16), (qk * db_ref[...]).astype(BF16)], axis=0)
        intra = []
        for g in range(2):
            cols = slice(g * half_heads * RET_VAL_DIM, (g + 1) * half_heads * RET_VAL_DIM)
            v_bd = jnp.where(v_mask, jnp.concatenate([rv[:, cols]] * half_heads, axis=0), jnp.zeros((), BF16))
            intra.append(_dot(p_fb[:, g * half_heads * BLK:(g + 1) * half_heads * BLK], v_bd))
        cross = []
        for state_ref, qdec_ref in ((pf_ref, qdf_ref), (pb_ref, qdb_ref)):
            s_bd = jnp.where(s_mask, jnp.concatenate([state_ref[j]] * N_RET_HEADS, axis=0), 0.0)
            cross.append(_dot(rq, s_bd.astype(BF16)) * qdec_ref[...])
        y = jnp.concatenate(intra, axis=1) + jnp.concatenate(cross, axis=0)

        d = y - group_mean(y)
        normed = d * lax.rsqrt(group_mean(d * d) + GN_EPS)
        g_f = ret_ref[tok, 2 * RET_QK_WIDTH + RET_V_WIDTH:2 * RET_QK_WIDTH + 2 * RET_V_WIDTH].astype(F32)
        g_b = ret_ref[tok, 2 * RET_QK_WIDTH + 2 * RET_V_WIDTH:].astype(F32)
        r = g_f * _sigmoid(g_f) * normed[:BLK] + g_b * _sigmoid(g_b) * normed[BLK:]
        r_ref[tok, :] = r.astype(BF16)


def _mix_proj_kernel(nb, n_blocks, sink_ref, qa_ref, kvp_ref, kvc_ref, kvn_ref, ret_ref, kvf_ref, pb_ref,
                     bias_ref, df_ref, db_ref, qdf_ref, qdb_ref, gavg_ref, cdf_ref,
                     h_ref, gl_ref, bg_ref, woa_ref, wor_ref, wout_ref, g_ref, b_ref,
                     x_ref, ya_scr, r_scr, sf_ref, pf_ref):
    s = pl.program_id(0)

    n = lax.rem(jnp.minimum(s, n_blocks - 1), nb // MIX_CHUNKS)

    @pl.when(s == 0)
    def _():
        ya_scr[...] = jnp.zeros_like(ya_scr)
        r_scr[...] = jnp.zeros_like(r_scr)

    @pl.when(n == 0)
    def _():
        sf_ref[...] = jnp.zeros_like(sf_ref)

    slot = lax.rem(s, 2)
    ya_prev = ya_scr[1 - slot]
    r_prev = r_scr[1 - slot]
    t = jnp.tanh(gl_ref[...].astype(F32) + bg_ref[...])
    pa = _dot(ya_prev, woa_ref[...])
    pr = _dot(r_prev, wor_ref[...])
    merged = pa + pr + t[:, :D_MODEL] * pa + t[:, D_MODEL:] * pr
    mix = _dot(merged.astype(BF16), wout_ref[...])
    x_ref[...] = _layer_norm(DEEPNORM_ALPHA * h_ref[...] + mix, g_ref[...], b_ref[...])

    state = sf_ref[...]
    for j in range(MIX_CHUNKS):
        pf_ref[j] = state
        state = state * cdf_ref[...] + kvf_ref[j]
    sf_ref[...] = state

    _mixer_body(n, nb, sink_ref, qa_ref, kvp_ref, kvc_ref, kvn_ref, ret_ref, pf_ref, pb_ref,
                bias_ref, df_ref, db_ref, qdf_ref, qdb_ref, gavg_ref, ya_scr.at[slot], r_scr.at[slot])


def _mix_proj(qa, kva, ret, kvf, prev_b, sink, tabs, h, gl, b_gate, woa, wor, wout, g, b, batch, nb):
    t = qa.shape[0]
    steps = nb // MIX_CHUNKS
    rows = MIX_CHUNKS * BLK
    n_blocks = batch * steps

    def mixed(s):
        return jnp.minimum(s, n_blocks - 1)

    def projected(s):
        return jnp.maximum(s - 1, 0)

    cur = lambda s: (mixed(s), 0)
    prv = lambda s: (mixed(s) * MIX_CHUNKS - jnp.where(lax.rem(mixed(s), steps) == 0, 0, 1), 0)
    nxt = lambda s: (mixed(s) * MIX_CHUNKS + MIX_CHUNKS - jnp.where(lax.rem(mixed(s), steps) == steps - 1, 1, 0), 0)
    out = lambda s: (projected(s), 0)
    c2 = lambda s: (0, 0)
    state_spec = pl.BlockSpec((MIX_CHUNKS, RET_KEY_DIM, RET_V_WIDTH), lambda s: (mixed(s), 0, 0))
    return pl.pallas_call(
        functools.partial(_mix_proj_kernel, nb, n_blocks),
        grid=(n_blocks + 1,),
        in_specs=[
            pl.BlockSpec(memory_space=pltpu.SMEM),
            pl.BlockSpec((rows, ATTN_Q_WIDTH), cur),
            pl.BlockSpec((BLK, 2 * ATTN_KV_WIDTH), prv),
            pl.BlockSpec((rows, 2 * ATTN_KV_WIDTH), cur),
            pl.BlockSpec((BLK, 2 * ATTN_KV_WIDTH), nxt),
            pl.BlockSpec((rows, RET_WIDTH), cur),
            state_spec,
            state_spec,
            pl.BlockSpec((3, N_KV_HEADS, 2, 2 * BLK, 3 * BLK), lambda s: (0, 0, 0, 0, 0)),
            pl.BlockSpec((BLK, N_RET_HEADS * BLK), c2),
            pl.BlockSpec((BLK, N_RET_HEADS * BLK), c2),
            pl.BlockSpec((BLK, RET_V_WIDTH), c2),
            pl.BlockSpec((BLK, RET_V_WIDTH), c2),
            pl.BlockSpec((GN_BLOCK, GN_BLOCK), c2),
            pl.BlockSpec((1, RET_V_WIDTH), c2),
            pl.BlockSpec((rows, D_MODEL), out),
            pl.BlockSpec((rows, 2 * D_MODEL), out),
            pl.BlockSpec((1, 2 * D_MODEL), c2),
            pl.BlockSpec((ATTN_Q_WIDTH, D_MODEL), c2),
            pl.BlockSpec((RET_V_WIDTH, D_MODEL), c2),
            pl.BlockSpec((D_MODEL, D_MODEL), c2),
            pl.BlockSpec((1, D_MODEL), c2),
            pl.BlockSpec((1, D_MODEL), c2),
        ],
        out_specs=pl.BlockSpec((rows, D_MODEL), out),
        out_shape=jax.ShapeDtypeStruct((t, D_MODEL), F32),
        scratch_shapes=[pltpu.VMEM((2, rows, ATTN_Q_WIDTH), BF16), pltpu.VMEM((2, rows, RET_V_WIDTH), BF16),
                        pltpu.VMEM((RET_KEY_DIM, RET_V_WIDTH), F32),
                        pltpu.VMEM((MIX_CHUNKS, RET_KEY_DIM, RET_V_WIDTH), F32)],
        compiler_params=_params("arbitrary"),
        name="mix_proj",
    )(sink, qa, kva, kva, kva, ret, kvf, prev_b, tabs["attn_bias"], tabs["intra_f"],
      tabs["intra_b"], tabs["qdec_f"], tabs["qdec_b"], tabs["group_avg"], tabs["cdec_f"],
      h, gl, b_gate.reshape(1, -1), woa, wor, wout, g.reshape(1, -1), b.reshape(1, -1))


def _ffn_kernel(x_ref, w1_ref, w3_ref, w2_ref, g_ref, b_ref, o_ref):
    x = x_ref[...]
    xb = x.astype(BF16)
    hidden = []
    for c in range(D_FF_DENSE // FF_CHUNK):
        cols = slice(c * FF_CHUNK, (c + 1) * FF_CHUNK)
        a = _dot(xb, w1_ref[:, cols])
        hidden.append((a * _sigmoid(a) * _dot(xb, w3_ref[:, cols])).astype(BF16))
    acc = _dot(jnp.concatenate(hidden, axis=1), w2_ref[...])
    o_ref[...] = _layer_norm(DEEPNORM_ALPHA * x + acc, g_ref[...], b_ref[...])


def _ffn(x, w1, w3, w2, g, b):
    t = x.shape[0]
    tm = FFN_TOKEN_TILE
    row = lambda i: (i, 0)
    const = lambda i: (0, 0)
    return pl.pallas_call(
        _ffn_kernel,
        grid=(t // tm,),
        in_specs=[
            pl.BlockSpec((tm, D_MODEL), row),
            pl.BlockSpec((D_MODEL, D_FF_DENSE), const),
            pl.BlockSpec((D_MODEL, D_FF_DENSE), const),
            pl.BlockSpec((D_FF_DENSE, D_MODEL), const),
            pl.BlockSpec((1, D_MODEL), const),
            pl.BlockSpec((1, D_MODEL), const),
        ],
        out_specs=pl.BlockSpec((tm, D_MODEL), row),
        out_shape=jax.ShapeDtypeStruct((t, D_MODEL), F32),
        compiler_params=_params("parallel"),
        name="ffn",
    )(x, w1, w3, w2, g.reshape(1, -1), b.reshape(1, -1))


def _route_kernel(x_ref, rw_ref, rb_ref, tri_ref, xb_ref, tcol_ref, pcol_ref, drow_ref, cnt_ref):
    x = x_ref[...]
    tm, lanes = x.shape[0], LANES
    lane = lax.broadcasted_iota(jnp.int32, (tm, lanes), 1)
    xh, xl = _split_bf16(x)
    wh, wl = _split_bf16(rw_ref[...])
    hi_part = _dot(xh, jnp.concatenate([wh, wl], axis=1))
    logits = hi_part[:, :lanes] + hi_part[:, lanes:] + _dot(xl, wh) + rb_ref[...]
    logits = jnp.where(lane < N_EXPERTS, logits, NEG_INF)
    m1 = jnp.max(logits, axis=1, keepdims=True)
    i1 = jnp.min(jnp.where(logits == m1, lane, lanes), axis=1, keepdims=True)
    rest = jnp.where(lane == i1, NEG_INF, logits)
    m2 = jnp.max(rest, axis=1, keepdims=True)
    i2 = jnp.min(jnp.where(rest == m2, lane, lanes), axis=1, keepdims=True)
    e2 = jnp.exp(m2 - m1)
    w_top = 1.0 / (1.0 + e2)
    gate =jnp.where(lane == i1, w_top, 0.0) + jnp.where(lane == i2, e2 * w_top, 0.0)
    sel = jnp.where(lane == i1, 1.0, jnp.where(lane == i2, 1.0, 0.0))
    sel_b = sel.astype(BF16)
    blk = tri_ref.shape[0]
    parts, above = [], jnp.zeros((1, lanes), F32)
    for r0 in range(0, tm, blk):
        local = _dot(tri_ref[...], sel_b[r0:r0 + blk])
        parts.append(local + above)
        above = above + local[blk - 1:blk]
    prefix = jnp.concatenate(parts, axis=0)
    slot1 = jnp.where(sel > 0.0, prefix, 0.0)
    hi = jnp.floor(slot1 * (1.0 / SLOT_RADIX))
    tcol_ref[:, :lanes] = gate.astype(BF16)
    tcol_ref[:, lanes:2 * lanes] = hi.astype(BF16)
    tcol_ref[:, 2 * lanes:] = (slot1 - SLOT_RADIX * hi).astype(BF16)

    def first_slab_pos(idx, picked):
        rank = jnp.sum(jnp.where(picked, prefix, 0.0), axis=1, keepdims=True) - 1.0
        return jnp.where(rank < MOE_SLAB, idx.astype(F32) * MOE_SLAB + rank, -1.0)

    pcol_ref[...] = jnp.where(
        lane == 0, first_slab_pos(i1, lane == i1), jnp.where(
            lane == 1, w_top, jnp.where(
                lane == 2, first_slab_pos(i2, lane == i2), jnp.where(lane == 3, e2 * w_top, 0.0))))
    drow_ref[...] = (slot1 - 1.0).T[:N_EXPERTS]
    cnt_ref[...] = prefix[tm - 1:tm].astype(jnp.int32)
    xb_ref[...] = x.astype(BF16)


def _route(x, router_w, router_b):
    t = x.shape[0]
    tm = MOE_TOKEN_TILE
    lanes = 128
    rw = jnp.zeros((D_MODEL, lanes), F32).at[:, :N_EXPERTS].set(router_w)
    rb = jnp.zeros((1, lanes), F32).at[0, :N_EXPERTS].set(router_b)
    tri = jnp.asarray(np.arange(ROUTE_PREFIX_BLOCK)[:, None] >= np.arange(ROUTE_PREFIX_BLOCK)[None, :], BF16)
    row = lambda i: (i, 0)
    const = lambda i: (0, 0)
    return pl.pallas_call(
        _route_kernel,
        grid=(t // tm,),
        in_specs=[
            pl.BlockSpec((tm, D_MODEL), row),
            pl.BlockSpec((D_MODEL, lanes), const),
            pl.BlockSpec((1, lanes), const),
            pl.BlockSpec((ROUTE_PREFIX_BLOCK, ROUTE_PREFIX_BLOCK), const),
        ],
        out_specs=[
            pl.BlockSpec((tm, D_MODEL), row),
            pl.BlockSpec((tm, 3 * lanes), row),
            pl.BlockSpec((tm, lanes), row),
            pl.BlockSpec((N_EXPERTS, tm), row),
            pl.BlockSpec((None, 1, lanes), lambda i: (i, 0, 0)),
        ],
        out_shape=[
            jax.ShapeDtypeStruct((t, D_MODEL), BF16),
            jax.ShapeDtypeStruct((t, 3 * lanes), BF16),
            jax.ShapeDtypeStruct((t, lanes), F32),
            jax.ShapeDtypeStruct((t // tm * N_EXPERTS, tm), F32),
            jax.ShapeDtypeStruct((t // tm, 1, lanes), jnp.int32),
        ],
        compiler_params=_params("parallel"),
        name="route",
    )(x, rw, rb, tri)


def _moe_kernel(cnt_ref, x_ref, xb_ref, tcol_ref, pcol_ref, drow_ref, w1_ref, w3_ref, w2_ref,
                g_ref, b_ref, o_ref, y_ref):
    i = pl.program_id(0)
    e = pl.program_id(1)
    count = cnt_ref[i * N_EXPERTS + e]

    @pl.when(e == 0)
    def _():
        o_ref[...] = jnp.zeros_like(o_ref)

    slot_row = drow_ref[pl.ds(e, 1), :]

    def expert_ffn(base, n_rows):
        rows = lax.broadcasted_iota(jnp.int32, (n_rows, 1), 0).astype(F32)
        pick = jnp.where(slot_row == rows + base, 1.0, 0.0).astype(BF16)
        xs = _dot(pick, xb_ref[...]).astype(BF16)
        main = D_FF_EXPERT - LANES
        gate = _dot(xs, w1_ref[:, :main])
        up = _dot(xs, w3_ref[:, :main])
        tail = _dot(xs, jnp.concatenate([w1_ref[:, main:], w3_ref[:, main:]], axis=1))
        gate_t, up_t = tail[:, :LANES], tail[:, LANES:]
        hid = jnp.concatenate([gate * _sigmoid(gate) * up, gate_t * _sigmoid(gate_t) * up_t], axis=1)
        return _dot(hid.astype(BF16), w2_ref[...]).astype(BF16)

    parked = pl.ds(pl.multiple_of(e * MOE_SLAB, 16), MOE_SLAB)

    @pl.when(count <= MOE_SLAB_SMALL)
    def _():
        y_ref[parked, :] = jnp.concatenate(
            [expert_ffn(0.0, MOE_SLAB_SMALL), jnp.zeros((MOE_SLAB - MOE_SLAB_SMALL, D_MODEL), BF16)], axis=0)

    @pl.when(count > MOE_SLAB_SMALL)
    def _():
        y_ref[parked, :] = expert_ffn(0.0, MOE_SLAB)

    def later_slab(s, carry):
        base = (s * MOE_SLAB).astype(F32)
        pick_e = jnp.where(lax.broadcasted_iota(jnp.int32, (LANES, LANES), 0) == e, 1.0, 0.0).astype(BF16)
        gate_b = _dot(tcol_ref[:, :LANES], pick_e)
        slot_b = (SLOT_RADIX * _dot(tcol_ref[:, LANES:2 * LANES], pick_e)
                  + _dot(tcol_ref[:, 2 * LANES:], pick_e) - 1.0)
        cols = lax.broadcasted_iota(jnp.int32, (1, LANES), 1).astype(F32)
        y = jnp.concatenate([expert_ffn(base, MOE_SLAB),
                             jnp.zeros((MOE_SLAB_K - MOE_SLAB, D_MODEL), BF16)], axis=0)
        put = jnp.concatenate(
            [jnp.where(slot_b == cols + (base + k * LANES), gate_b, 0.0).astype(BF16)
             for k in range(MOE_SLAB_K // LANES)], axis=1)
        o_ref[...] += _dot(put, y)
        return carry

    n_slabs = (count + MOE_SLAB - 1) // MOE_SLAB
    lax.fori_loop(1, n_slabs, later_slab, 0)

    @pl.when(e == N_EXPERTS - 1)
    def _():
        p = pcol_ref[...]
        pos_a, gate_a, pos_b, gate_b = p[:, 0:1], p[:, 1:2], p[:, 2:3], p[:, 3:4]
        puts = []
        for k in range(N_EXPERTS * MOE_SLAB // MOE_COMBINE_K):
            cols = lax.broadcasted_iota(jnp.int32, (1, MOE_COMBINE_K), 1).astype(F32) + float(k * MOE_COMBINE_K)
            puts.append(jnp.where(pos_a == cols, gate_a, jnp.where(pos_b == cols, gate_b, 0.0)).astype(BF16))
        acc = DEEPNORM_ALPHA * x_ref[...] + o_ref[...] + _dot(jnp.concatenate(puts, axis=1), y_ref[...])
        o_ref[...] = _layer_norm(acc, g_ref[...], b_ref[...])


def _moe(x, router_w, router_b, w1, w3, w2, g, b):
    t = x.shape[0]
    tm = MOE_TOKEN_TILE
    xb, tcol, pcol, drow, cnt = _route(x, router_w, router_b)
    counts = cnt[:, 0, :N_EXPERTS].reshape(-1)
    row = lambda i, e, c: (i, 0)
    const = lambda i, e, c: (0, 0)
    expert = lambda i, e, c: (e, 0, 0)
    grid_spec = pltpu.PrefetchScalarGridSpec(
        num_scalar_prefetch=1,
        grid=(t // tm, N_EXPERTS),
        in_specs=[
            pl.BlockSpec((tm, D_MODEL), row),
            pl.BlockSpec((tm, D_MODEL), row),
            pl.BlockSpec((tm, 3 * LANES), row),
            pl.BlockSpec((tm, LANES), row),
            pl.BlockSpec((N_EXPERTS, tm), row),
            pl.BlockSpec((None, D_MODEL, D_FF_EXPERT), expert),
            pl.BlockSpec((None, D_MODEL, D_FF_EXPERT), expert),
            pl.BlockSpec((None, D_FF_EXPERT, D_MODEL), expert),
            pl.BlockSpec((1, D_MODEL), const),
            pl.BlockSpec((1, D_MODEL), const),
        ],
        out_specs=pl.BlockSpec((tm, D_MODEL), row),
        scratch_shapes=[pltpu.VMEM((N_EXPERTS * MOE_SLAB, D_MODEL), BF16)],
    )
    return pl.pallas_call(
        _moe_kernel,
        grid_spec=grid_spec,
        out_shape=jax.ShapeDtypeStruct((t, D_MODEL), F32),
        compiler_params=_params("parallel", "arbitrary"),
        name="moe",
    )(counts, x, xb, tcol, pcol, drow, w1, w3, w2, g.reshape(1, -1), b.reshape(1, -1))


def _attn_bias_table():
    qi = np.arange(BLK)[:, None]
    kj = np.arange(3 * BLK)[None, :]
    dist = np.abs(qi - kj + BLK)
    slopes = np.exp2(-8.0 * np.arange(1, N_Q_HEADS + 1) / N_Q_HEADS)
    bias = -(LOG2E * slopes)[:, None, None] * dist[None]
    in_window = (dist <= WINDOW)[None]
    variants = []
    for ok in (kj >= BLK, kj >= 0, kj < 2 * BLK):
        per_head = np.where(in_window & ok[None], bias, NEG_INF)
        per_head = per_head.reshape(N_KV_HEADS, 2, 2, BLK, 3 * BLK)
        variants.append(np.transpose(per_head, (0, 2, 1, 3, 4)).reshape(N_KV_HEADS, 2, 2 * BLK, 3 * BLK))
    return np.stack(variants).astype(np.float32)


def _retention_tables(decay_fwd, decay_bwd):
    pos = jnp.arange(BLK, dtype=F32)
    diff = jnp.arange(BLK)[:, None] - jnp.arange(BLK)[None, :]
    lg_f = jax.nn.log_sigmoid(decay_fwd.astype(F32))
    lg_b = jax.nn.log_sigmoid(decay_bwd.astype(F32))

    def intra(lg, d, mask):
        dec = jnp.where(mask[None], jnp.exp(jnp.maximum(d, 0).astype(F32)[None] * lg[:, None, None]), 0.0)
        return jnp.transpose(dec, (1, 0, 2)).reshape(BLK, N_RET_HEADS * BLK)

    def per_pos(lg, expo, width):
        return jnp.repeat(jnp.exp(expo[:, None] * lg[None, :]), width, axis=1)

    return {
        "intra_f": intra(lg_f, diff, diff >= 0),
        "intra_b": intra(lg_b, -diff, diff < 0),
        "qdec_f": per_pos(lg_f, pos + 1.0, RET_VAL_DIM),
        "qdec_b": per_pos(lg_b, BLK - pos, RET_VAL_DIM),
        "kdec_f": per_pos(lg_f, BLK - 1.0 - pos, RET_KEY_DIM),
        "kdec_b": per_pos(lg_b, pos, RET_KEY_DIM),
        "cdec_f": jnp.repeat(jnp.exp(BLK * lg_f), RET_VAL_DIM)[None, :],
        "cdec_b": jnp.repeat(jnp.exp(BLK * lg_b), RET_VAL_DIM)[None, :],
    }


def _group_avg_matrix():
    head = np.arange(GN_BLOCK) // RET_VAL_DIM
    return jnp.asarray(np.where(head[:, None] == head[None, :], 1.0 / RET_VAL_DIM, 0.0), BF16)


def kernel(x, ln_emb_g, ln_emb_b, w_in, b_gate, sink_logits, decay_fwd, decay_bwd,
           w_o_attn, w_o_ret, w_out, ln1_g, ln1_b, ffn_w1, ffn_w3, ffn_w2,
           router_w, router_b, moe_w1, moe_w3, moe_w2, ln2_g, ln2_b):
    batch, seq, _ = x.shape
    nb = seq // BLK
    xt = x.reshape(batch * seq, D_MODEL)
    attn_bias = _attn_bias_table()
    group_avg = _group_avg_matrix()
    for layer in range(DEPTH):
        tabs = _retention_tables(decay_fwd[layer], decay_bwd[layer])
        tabs["attn_bias"] = attn_bias
        tabs["group_avg"] = group_avg
        w = w_in[layer].astype(BF16)
        if layer == 0:
            h, qa, kva, ret, gl, kvf, prev_b = _in_proj(xt, w, tabs, seq, ln=(ln_emb_g, ln_emb_b))
        else:
            h = xt
            qa, kva, ret, gl, kvf, prev_b = _in_proj(xt, w, tabs, seq)
        x1 = _mix_proj(qa, kva, ret, kvf, prev_b, LOG2E * sink_logits[layer].astype(F32), tabs,
                       h, gl, 0.5 * b_gate[layer], w_o_attn[layer].astype(BF16),
                       w_o_ret[layer].astype(BF16), (0.5 * w_out[layer]).astype(BF16),
                       ln1_g[layer], ln1_b[layer], batch, nb)
        i = layer // 2
        if layer % 2 == 0:
            xt = _ffn(x1, ffn_w1[i].astype(BF16), ffn_w3[i].astype(BF16), ffn_w2[i].astype(BF16),
                      ln2_g[layer], ln2_b[layer])
        else:
            xt = _moe(x1, router_w[i], router_b[i], moe_w1[i].astype(BF16), moe_w3[i].astype(BF16),
                      moe_w2[i].astype(BF16), ln2_g[layer], ln2_b[layer])
    return xt.reshape(batch, seq, D_MODEL)
```

```python
import functools

import jax
import jax.numpy as jnp
import numpy as np
from jax import lax
from jax.experimental import pallas as pl
from jax.experimental.pallas import tpu as pltpu

F32 = jnp.float32
BF16 = jnp.bfloat16

D_MODEL = 1024
DEPTH = 2
N_Q_HEADS = 8
N_KV_HEADS = 2
HEAD_DIM = 64
WINDOW = 128
BLK = 128
ATTN_Q_WIDTH = N_Q_HEADS * HEAD_DIM
ATTN_KV_WIDTH = N_KV_HEADS * HEAD_DIM
N_RET_HEADS = 8
RET_KEY_DIM = 32
RET_VAL_DIM = 64
RET_QK_WIDTH = N_RET_HEADS * RET_KEY_DIM
RET_V_WIDTH = N_RET_HEADS * RET_VAL_DIM
RET_WIDTH = 2 * RET_QK_WIDTH + 3 * RET_V_WIDTH
D_FF_DENSE = 2816
N_EXPERTS = 8
D_FF_EXPERT = 1408
IN_WIDTH = ATTN_Q_WIDTH + 2 * ATTN_KV_WIDTH + RET_WIDTH + 2 * D_MODEL
DEEPNORM_ALPHA = (2 * DEPTH) ** 0.25
LN_EPS = 1e-5
GN_EPS = 1e-5
NEG_INF = -1e30
LOG2E = 1.4426950408889634

TOKEN_TILE = 1024
FFN_TOKEN_TILE = 1024
MOE_TOKEN_TILE = 1024
ROUTE_PREFIX_BLOCK = 256
MOE_SLAB = 288
MOE_FIRST_SLAB_SIZES = (256, 272, MOE_SLAB)
MOE_SLAB_K = 384
MOE_COMBINE_K = 256
GN_BLOCK = 256
LANES = 128
SLOT_RADIX = 32.0
MIX_CHUNKS = 4
FF_CHUNK = 256
VMEM_LIMIT_BYTES = 56 * 1024 * 1024


def _params(*sem):
    return pltpu.CompilerParams(dimension_semantics=sem, vmem_limit_bytes=VMEM_LIMIT_BYTES)


def _layer_norm(x, g, b):
    mu = jnp.mean(x, axis=-1, keepdims=True)
    d = x - mu
    var = jnp.mean(d * d, axis=-1, keepdims=True)
    return d * lax.rsqrt(var + LN_EPS) * g + b


def _sigmoid(x):
    return 1.0 / (1.0 + jnp.exp(-x))


def _dot(a, b):
    return jnp.dot(a, b, preferred_element_type=F32)


def _dot_nt(a, b):
    return lax.dot_general(a, b, (((1,), (1,)), ((), ())), preferred_element_type=F32)


def _head_block_mask(rows, cols, row_div, col_div):
    r = lax.broadcasted_iota(jnp.int32, (rows, cols), 0) // row_div
    c = lax.broadcasted_iota(jnp.int32, (rows, cols), 1) // col_div
    return r == c


def _in_proj_kernel(apply_ln, tiles_per_seq, *refs):
    sb_ref = refs[-1]
    tile = pl.num_programs(0) - 1 - pl.program_id(0)

    @pl.when(lax.rem(tile, tiles_per_seq) == tiles_per_seq - 1)
    def _():
        sb_ref[...] = jnp.zeros_like(sb_ref)

    if apply_ln:
        (x_ref, g_ref, b_ref, w_ref, kdf_ref, kdb_ref, cdb_ref,
         h_ref, qa_ref, kva_ref, ret_ref, gl_ref, kvf_ref, pb_ref, _) = refs
        x = _layer_norm(x_ref[...], g_ref[...], b_ref[...])
        h_ref[...] = x
    else:
        (x_ref, w_ref, kdf_ref, kdb_ref, cdb_ref,
         qa_ref, kva_ref, ret_ref, gl_ref, kvf_ref, pb_ref, _) = refs
        x = x_ref[...]
    xb = x.astype(BF16)

    def proj(lo, hi):
        return _dot(xb, w_ref[:, lo:hi])

    c0 = ATTN_Q_WIDTH
    c1 = c0 + 2 * ATTN_KV_WIDTH
    qa_ref[...] = (proj(0, c0) * (HEAD_DIM ** -0.5 * LOG2E)).astype(BF16)
    kva_ref[...] = proj(c0, c1).astype(BF16)
    ret_ref[:, :RET_QK_WIDTH] = proj(c1, c1 + RET_QK_WIDTH).astype(BF16)
    rk = (proj(c1 + RET_QK_WIDTH, c1 + 2 * RET_QK_WIDTH) * (RET_KEY_DIM ** -0.5)).astype(BF16)
    ret_ref[:, RET_QK_WIDTH:2 * RET_QK_WIDTH] = rk
    rest = proj(c1 + 2 * RET_QK_WIDTH, c1 + RET_WIDTH).astype(BF16)
    ret_ref[:, 2 * RET_QK_WIDTH:] = rest
    gl_ref[...] = (0.5 * proj(c1 + RET_WIDTH, IN_WIDTH)).astype(BF16)

    half = N_RET_HEADS // 2
    mask = _head_block_mask(half * RET_KEY_DIM, half * RET_VAL_DIM, RET_KEY_DIM, RET_VAL_DIM)
    def increment(c, kd_ref):
        rows = slice(c * BLK, (c + 1) * BLK)
        kz = (rk[rows].astype(F32) * kd_ref[...]).T.astype(BF16)
        v = rest[rows, :RET_V_WIDTH]
        comps = []
        for g in range(2):
            full = jnp.where(mask, _dot(kz[g * half * RET_KEY_DIM:(g + 1) * half * RET_KEY_DIM],
                                        v[:, g * half * RET_VAL_DIM:(g + 1) * half * RET_VAL_DIM]), 0.0)
            comp = full[0:RET_KEY_DIM]
            for h in range(1, half):
                comp = comp + full[h * RET_KEY_DIM:(h + 1) * RET_KEY_DIM]
            comps.append(comp)
        return jnp.concatenate(comps, axis=1)

    state = sb_ref[...]
    for c in reversed(range(x.shape[0] // BLK)):
        kvf_ref[c] = increment(c, kdf_ref)
        pb_ref[c] = state
        state = state * cdb_ref[...] + increment(c, kdb_ref)
    sb_ref[...] = state


def _in_proj(x, w, tabs, seq, ln=None):
    t = x.shape[0]
    tm = TOKEN_TILE
    tiles = t // tm
    row = lambda i: (tiles - 1 - i, 0)
    const = lambda i: (0, 0)
    in_specs = [pl.BlockSpec((tm, D_MODEL), row)]
    args = [x]
    out_shape = []
    out_specs = []
    if ln is not None:
        in_specs += [pl.BlockSpec((1, D_MODEL), const), pl.BlockSpec((1, D_MODEL), const)]
        args += [ln[0].reshape(1, D_MODEL), ln[1].reshape(1, D_MODEL)]
        out_shape.append(jax.ShapeDtypeStruct((t, D_MODEL), F32))
        out_specs.append(pl.BlockSpec((tm, D_MODEL), row))
    in_specs += [pl.BlockSpec((D_MODEL, IN_WIDTH), const),
                 pl.BlockSpec((BLK, RET_QK_WIDTH), const), pl.BlockSpec((BLK, RET_QK_WIDTH), const),
                 pl.BlockSpec((1, RET_V_WIDTH), const)]
    args += [w, tabs["kdec_f"], tabs["kdec_b"], tabs["cdec_b"]]
    for width in (ATTN_Q_WIDTH, 2 * ATTN_KV_WIDTH, RET_WIDTH, 2 * D_MODEL):
        out_shape.append(jax.ShapeDtypeStruct((t, width), BF16))
        out_specs.append(pl.BlockSpec((tm, width), row))
    for _ in range(2):
        out_shape.append(jax.ShapeDtypeStruct((t // BLK, RET_KEY_DIM, RET_V_WIDTH), F32))
        out_specs.append(pl.BlockSpec((tm // BLK, RET_KEY_DIM, RET_V_WIDTH), lambda i: (tiles - 1 - i, 0, 0)))
    return pl.pallas_call(
        functools.partial(_in_proj_kernel, ln is not None, seq // tm),
        grid=(tiles,),
        in_specs=in_specs,
        out_specs=out_specs,
        out_shape=out_shape,
        scratch_shapes=[pltpu.VMEM((RET_KEY_DIM, RET_V_WIDTH), F32)],
        compiler_params=_params("arbitrary"),
        name="in_proj",
    )(*args)


def _split_bf16(y):
    hi = y.astype(BF16)
    lo = (y - hi.astype(F32)).astype(BF16)
    return hi, lo


def _mixer_body(n, nb, sink_ref, qa_ref, kvp_ref, kvc_ref, kvn_ref, ret_ref, pf_ref, pb_ref,
                bias_ref, df_ref, db_ref, qdf_ref, qdb_ref, gavg_ref, ya_ref, r_ref):
    q = qa_ref[...]
    kv = jnp.concatenate([kvp_ref[...], kvc_ref[...], kvn_ref[...]], axis=0).astype(F32)
    lo_half = lax.broadcasted_iota(jnp.int32, (kv.shape[0], 2 * HEAD_DIM), 1) < HEAD_DIM

    def lane_variants(a):
        swapped = pltpu.roll(a, HEAD_DIM, 1)
        z = jnp.zeros_like(a)
        return (((jnp.where(lo_half, a, z)).astype(BF16), (jnp.where(lo_half, z, swapped)).astype(BF16)),
                ((jnp.where(lo_half, swapped, z)).astype(BF16), (jnp.where(lo_half, z, a)).astype(BF16)))

    k_var = lane_variants(kv[:, :ATTN_KV_WIDTH])
    v_var = lane_variants(kv[:, ATTN_KV_WIDTH:])
    top_rows = lax.broadcasted_iota(jnp.int32, (2 * BLK, 1), 0) < BLK
    out_lo_half = lax.broadcasted_iota(jnp.int32, (2 * BLK, 2 * HEAD_DIM), 1) < HEAD_DIM
    half_heads = N_RET_HEADS // 2
    k_mask = _head_block_mask(N_RET_HEADS * BLK, RET_QK_WIDTH, BLK, RET_KEY_DIM)
    v_mask = _head_block_mask(half_heads * BLK, half_heads * RET_VAL_DIM, BLK, RET_VAL_DIM)
    s_mask = _head_block_mask(RET_QK_WIDTH, RET_V_WIDTH, RET_KEY_DIM, RET_VAL_DIM)
    pair = 2 * HEAD_DIM
    ys = []
    for j in range(MIX_CHUNKS):
        c = n * MIX_CHUNKS + j
        var = jnp.where(c == 0, 0, jnp.where(c == nb - 1, 2, 1))
        tok = slice(j * BLK, (j + 1) * BLK)
        keys = slice(j * BLK, (j + 3) * BLK)

        for g in range(N_KV_HEADS):
            q2 = jnp.concatenate([q[tok, pair * 2 * g:pair * (2 * g + 1)],
                                  q[tok, pair * (2 * g + 1):pair * (2 * g + 2)]], axis=0)
            s_all = _dot_nt(q2, jnp.concatenate([k_var[g][0][keys], k_var[g][1][keys]], axis=0))
            probs, denoms = [], []
            for r in range(2):
                sink = jnp.where(top_rows, sink_ref[4 * g + r], sink_ref[4 * g + 2 + r])
                s = s_all[:, 3 * BLK * r:3 * BLK * (r + 1)] + bias_ref[var, g, r]
                m = jnp.maximum(jnp.max(s, axis=1, keepdims=True), sink)
                e = jnp.exp2(s - m)
                denoms.append(jnp.sum(e, axis=1, keepdims=True) + jnp.exp2(sink - m))
                probs.append(e.astype(BF16))
            pv = _dot(jnp.concatenate(probs, axis=1),
                      jnp.concatenate([v_var[g][0][keys], v_var[g][1][keys]], axis=0))
            out = pv / jnp.where(out_lo_half, denoms[0], denoms[1])
            ya_ref[tok, pair * 2 * g:pair * (2 * g + 1)] = out[:BLK].astype(BF16)
            ya_ref[tok, pair * (2 * g + 1):pair * (2 * g + 2)] = out[BLK:].astype(BF16)

        rq = ret_ref[tok, :RET_QK_WIDTH]
        rk = ret_ref[tok, RET_QK_WIDTH:2 * RET_QK_WIDTH]
        rv = ret_ref[tok, 2 * RET_QK_WIDTH:2 * RET_QK_WIDTH + RET_V_WIDTH]
        k_exp = jnp.where(k_mask, jnp.concatenate([rk] * N_RET_HEADS, axis=0), jnp.zeros((), BF16))
        qk = _dot_nt(rq, k_exp)
        p_fb = jnp.concatenate([(qk * df_ref[...]).astype(BF16), (qk * db_ref[...]).astype(BF16)], axis=0)
        intra = []
        for g in range(2):
            cols = slice(g * half_heads * RET_VAL_DIM, (g + 1) * half_heads * RET_VAL_DIM)
            v_bd = jnp.where(v_mask, jnp.concatenate([rv[:, cols]] * half_heads, axis=0), jnp.zeros((), BF16))
            intra.append(_dot(p_fb[:, g * half_heads * BLK:(g + 1) * half_heads * BLK], v_bd))
        cross = []
        for state_ref, qdec_ref in ((pf_ref, qdf_ref), (pb_ref, qdb_ref)):
            s_bd = jnp.where(s_mask, jnp.concatenate([state_ref[j]] * N_RET_HEADS, axis=0), 0.0)
            cross.append(_dot(rq, s_bd.astype(BF16)) * qdec_ref[...])
        ys.append(jnp.concatenate(intra, axis=1) + jnp.concatenate(cross, axis=0))

    y = jnp.concatenate(ys, axis=0)
    def group_mean(a):
        ab = a.astype(BF16)
        half = gavg_ref.shape[0]
        return jnp.concatenate([_dot(ab[:, k * half:(k + 1) * half], gavg_ref[...])
                                for k in range(RET_V_WIDTH // half)], axis=1)

    d = y - group_mean(y)
    normed = d * lax.rsqrt(group_mean(d * d) + GN_EPS)
    for j in range(MIX_CHUNKS):
        tok = slice(j * BLK, (j + 1) * BLK)
        g_f = ret_ref[tok, 2 * RET_QK_WIDTH + RET_V_WIDTH:2 * RET_QK_WIDTH + 2 * RET_V_WIDTH].astype(F32)
        g_b = ret_ref[tok, 2 * RET_QK_WIDTH + 2 * RET_V_WIDTH:].astype(F32)
        r = (g_f * _sigmoid(g_f) * normed[2 * j * BLK:(2 * j + 1) * BLK]
             + g_b * _sigmoid(g_b) * normed[(2 * j + 1) * BLK:(2 * j + 2) * BLK])
        r_ref[tok, :] = r.astype(BF16)


def _mix_proj_kernel(nb, n_blocks, sink_ref, qa_ref, kvp_ref, kvc_ref, kvn_ref, ret_ref, kvf_ref, pb_ref,
                     bias_ref, df_ref, db_ref, qdf_ref, qdb_ref, gavg_ref, cdf_ref,
                     h_ref, gl_ref, bg_ref, woa_ref, wor_ref, wout_ref, g_ref, b_ref,
                     x_ref, ya_scr, r_scr, sf_ref, pf_ref):
    s = pl.program_id(0)

    n = lax.rem(jnp.minimum(s, n_blocks - 1), nb // MIX_CHUNKS)

    @pl.when(s == 0)
    def _():
        ya_scr[...] = jnp.zeros_like(ya_scr)
        r_scr[...] = jnp.zeros_like(r_scr)

    @pl.when(n == 0)
    def _():
        sf_ref[...] = jnp.zeros_like(sf_ref)

    slot = lax.rem(s, 2)
    ya_prev = ya_scr[1 - slot]
    r_prev = r_scr[1 - slot]
    t = jnp.tanh(gl_ref[...].astype(F32) + bg_ref[...])
    pa = _dot(ya_prev, woa_ref[...])
    pr = _dot(r_prev, wor_ref[...])
    merged = pa + pr + t[:, :D_MODEL] * pa + t[:, D_MODEL:] * pr
    mix = _dot(merged.astype(BF16), wout_ref[...])
    x_ref[...] = _layer_norm(DEEPNORM_ALPHA * h_ref[...] + mix, g_ref[...], b_ref[...])

    state = sf_ref[...]
    for j in range(MIX_CHUNKS):
        pf_ref[j] = state
        state = state * cdf_ref[...] + kvf_ref[j]
    sf_ref[...] = state

    _mixer_body(n, nb, sink_ref, qa_ref, kvp_ref, kvc_ref, kvn_ref, ret_ref, pf_ref, pb_ref,
                bias_ref, df_ref, db_ref, qdf_ref, qdb_ref, gavg_ref, ya_scr.at[slot], r_scr.at[slot])


def _mix_proj(qa, kva, ret, kvf, prev_b, sink, tabs, h, gl, b_gate, woa, wor, wout, g, b, batch, nb):
    t = qa.shape[0]
    steps = nb // MIX_CHUNKS
    rows = MIX_CHUNKS * BLK
    n_blocks = batch * steps

    def mixed(s):
        return jnp.minimum(s, n_blocks - 1)

    def projected(s):
        return jnp.maximum(s - 1, 0)

    cur = lambda s: (mixed(s), 0)
    prv = lambda s: (mixed(s) * MIX_CHUNKS - jnp.where(lax.rem(mixed(s), steps) == 0, 0, 1), 0)
    nxt = lambda s: (mixed(s) * MIX_CHUNKS + MIX_CHUNKS - jnp.where(lax.rem(mixed(s), steps) == steps - 1, 1, 0), 0)
    out = lambda s: (projected(s), 0)
    c2 = lambda s: (0, 0)
    state_spec = pl.BlockSpec((MIX_CHUNKS, RET_KEY_DIM, RET_V_WIDTH), lambda s: (mixed(s), 0, 0))
    return pl.pallas_call(
        functools.partial(_mix_proj_kernel, nb, n_blocks),
        grid=(n_blocks + 1,),
        in_specs=[
            pl.BlockSpec(memory_space=pltpu.SMEM),
            pl.BlockSpec((rows, ATTN_Q_WIDTH), cur),
            pl.BlockSpec((BLK, 2 * ATTN_KV_WIDTH), prv),
            pl.BlockSpec((rows, 2 * ATTN_KV_WIDTH), cur),
            pl.BlockSpec((BLK, 2 * ATTN_KV_WIDTH), nxt),
            pl.BlockSpec((rows, RET_WIDTH), cur),
            state_spec,
            state_spec,
            pl.BlockSpec((3, N_KV_HEADS, 2, 2 * BLK, 3 * BLK), lambda s: (0, 0, 0, 0, 0)),
            pl.BlockSpec((BLK, N_RET_HEADS * BLK), c2),
            pl.BlockSpec((BLK, N_RET_HEADS * BLK), c2),
            pl.BlockSpec((BLK, RET_V_WIDTH), c2),
            pl.BlockSpec((BLK, RET_V_WIDTH), c2),
            pl.BlockSpec((GN_BLOCK, GN_BLOCK), c2),
            pl.BlockSpec((1, RET_V_WIDTH), c2),
            pl.BlockSpec((rows, D_MODEL), out),
            pl.BlockSpec((rows, 2 * D_MODEL), out),
            pl.BlockSpec((1, 2 * D_MODEL), c2),
            pl.BlockSpec((ATTN_Q_WIDTH, D_MODEL), c2),
            pl.BlockSpec((RET_V_WIDTH, D_MODEL), c2),
            pl.BlockSpec((D_MODEL, D_MODEL), c2),
            pl.BlockSpec((1, D_MODEL), c2),
            pl.BlockSpec((1, D_MODEL), c2),
        ],
        out_specs=pl.BlockSpec((rows, D_MODEL), out),
        out_shape=jax.ShapeDtypeStruct((t, D_MODEL), F32),
        scratch_shapes=[pltpu.VMEM((2, rows, ATTN_Q_WIDTH), BF16), pltpu.VMEM((2, rows, RET_V_WIDTH), BF16),
                        pltpu.VMEM((RET_KEY_DIM, RET_V_WIDTH), F32),
                        pltpu.VMEM((MIX_CHUNKS, RET_KEY_DIM, RET_V_WIDTH), F32)],
        compiler_params=_params("arbitrary"),
        name="mix_proj",
    )(sink, qa, kva, kva, kva, ret, kvf, prev_b, tabs["attn_bias"], tabs["intra_f"],
      tabs["intra_b"], tabs["qdec_f"], tabs["qdec_b"], tabs["group_avg"], tabs["cdec_f"],
      h, gl, b_gate.reshape(1, -1), woa, wor, wout, g.reshape(1, -1), b.reshape(1, -1))


def _ffn_kernel(x_ref, w1_ref, w3_ref, w2_ref, g_ref, b_ref, o_ref):
    x = x_ref[...]
    xb = x.astype(BF16)
    hidden = []
    for c in range(D_FF_DENSE // FF_CHUNK):
        cols = slice(c * FF_CHUNK, (c + 1) * FF_CHUNK)
        a = _dot(xb, w1_ref[:, cols])
        hidden.append((a * _sigmoid(a) * _dot(xb, w3_ref[:, cols])).astype(BF16))
    acc = _dot(jnp.concatenate(hidden, axis=1), w2_ref[...])
    o_ref[...] = _layer_norm(DEEPNORM_ALPHA * x + acc, g_ref[...], b_ref[...])


def _ffn(x, w1, w3, w2, g, b):
    t = x.shape[0]
    tm = FFN_TOKEN_TILE
    row = lambda i: (i, 0)
    const = lambda i: (0, 0)
    return pl.pallas_call(
        _ffn_kernel,
        grid=(t // tm,),
        in_specs=[
            pl.BlockSpec((tm, D_MODEL), row),
            pl.BlockSpec((D_MODEL, D_FF_DENSE), const),
            pl.BlockSpec((D_MODEL, D_FF_DENSE), const),
            pl.BlockSpec((D_FF_DENSE, D_MODEL), const),
            pl.BlockSpec((1, D_MODEL), const),
            pl.BlockSpec((1, D_MODEL), const),
        ],
        out_specs=pl.BlockSpec((tm, D_MODEL), row),
        out_shape=jax.ShapeDtypeStruct((t, D_MODEL), F32),
        compiler_params=_params("parallel"),
        name="ffn",
    )(x, w1, w3, w2, g.reshape(1, -1), b.reshape(1, -1))


def _route_kernel(x_ref, rw_ref, rb_ref, tri_ref, xb_ref, tcol_ref, pcol_ref, drow_ref, cnt_ref):
    x = x_ref[...]
    tm, lanes = x.shape[0], LANES
    lane = lax.broadcasted_iota(jnp.int32, (tm, lanes), 1)
    xh, xl = _split_bf16(x)
    wh, wl = _split_bf16(rw_ref[...])
    hi_part = _dot(xh, jnp.concatenate([wh, wl], axis=1))
    logits = hi_part[:, :lanes] + hi_part[:, lanes:] + _dot(xl, wh) + rb_ref[...]
    logits = jnp.where(lane < N_EXPERTS, logits, NEG_INF)
    m1 = jnp.max(logits, axis=1, keepdims=True)
    i1 = jnp.min(jnp.where(logits == m1, lane, lanes), axis=1, keepdims=True)
    rest = jnp.where(lane == i1, NEG_INF, logits)
    m2 = jnp.max(rest, axis=1, keepdims=True)
    i2 = jnp.min(jnp.where(rest == m2, lane, lanes), axis=1, keepdims=True)
    e2 = jnp.exp(m2 - m1)
    w_top = 1.0 / (1.0 + e2)
    gate =jnp.where(lane == i1, w_top, 0.0) + jnp.where(lane == i2, e2 * w_top, 0.0)
    sel = jnp.where(lane == i1, 1.0, jnp.where(lane == i2, 1.0, 0.0))
    sel_b = sel.astype(BF16)
    blk = tri_ref.shape[0]
    parts, above = [], jnp.zeros((1, lanes), F32)
    for r0 in range(0, tm, blk):
        local = _dot(tri_ref[...], sel_b[r0:r0 + blk])
        parts.append(local + above)
        above = above + local[blk - 1:blk]
    prefix = jnp.concatenate(parts, axis=0)
    slot1 = jnp.where(sel > 0.0, prefix, 0.0)
    hi = jnp.floor(slot1 * (1.0 / SLOT_RADIX))
    tcol_ref[:, :lanes] = gate.astype(BF16)
    tcol_ref[:, lanes:2 * lanes] = hi.astype(BF16)
    tcol_ref[:, 2 * lanes:] = (slot1 - SLOT_RADIX * hi).astype(BF16)

    def first_slab_pos(idx, picked):
        rank = jnp.sum(jnp.where(picked, prefix, 0.0), axis=1, keepdims=True) - 1.0
        return jnp.where(rank < MOE_SLAB, idx.astype(F32) * MOE_SLAB + rank, -1.0)

    pcol_ref[...] = jnp.where(
        lane == 0, first_slab_pos(i1, lane == i1), jnp.where(
            lane == 1, w_top, jnp.where(
                lane == 2, first_slab_pos(i2, lane == i2), jnp.where(lane == 3, e2 * w_top, 0.0))))
    drow_ref[...] = (slot1 - 1.0).T[:N_EXPERTS]
    cnt_ref[...] = prefix[tm - 1:tm].astype(jnp.int32)
    xb_ref[...] = x.astype(BF16)


def _route(x, router_w, router_b):
    t = x.shape[0]
    tm = MOE_TOKEN_TILE
    lanes = 128
    rw = jnp.zeros((D_MODEL, lanes), F32).at[:, :N_EXPERTS].set(router_w)
    rb = jnp.zeros((1, lanes), F32).at[0, :N_EXPERTS].set(router_b)
    tri = jnp.asarray(np.arange(ROUTE_PREFIX_BLOCK)[:, None] >= np.arange(ROUTE_PREFIX_BLOCK)[None, :], BF16)
    row = lambda i: (i, 0)
    const = lambda i: (0, 0)
    return pl.pallas_call(
        _route_kernel,
        grid=(t // tm,),
        in_specs=[
            pl.BlockSpec((tm, D_MODEL), row),
            pl.BlockSpec((D_MODEL, lanes), const),
            pl.BlockSpec((1, lanes), const),
            pl.BlockSpec((ROUTE_PREFIX_BLOCK, ROUTE_PREFIX_BLOCK), const),
        ],
        out_specs=[
            pl.BlockSpec((tm, D_MODEL), row),
            pl.BlockSpec((tm, 3 * lanes), row),
            pl.BlockSpec((tm, lanes), row),
            pl.BlockSpec((N_EXPERTS, tm), row),
            pl.BlockSpec((None, 1, lanes), lambda i: (i, 0, 0)),
        ],
        out_shape=[
            jax.ShapeDtypeStruct((t, D_MODEL), BF16),
            jax.ShapeDtypeStruct((t, 3 * lanes), BF16),
            jax.ShapeDtypeStruct((t, lanes), F32),
            jax.ShapeDtypeStruct((t // tm * N_EXPERTS, tm), F32),
            jax.ShapeDtypeStruct((t // tm, 1, lanes), jnp.int32),
        ],
        compiler_params=_params("parallel"),
        name="route",
    )(x, rw, rb, tri)


def _moe_kernel(cnt_ref, x_ref, xb_ref, tcol_ref, pcol_ref, drow_ref, w1_ref, w3_ref, w2_ref,
                g_ref, b_ref, o_ref, y_ref):
    i = pl.program_id(0)
    e = pl.program_id(1)
    count = cnt_ref[i * N_EXPERTS + e]

    @pl.when(e == 0)
    def _():
        o_ref[...] = jnp.zeros_like(o_ref)

    slot_row = drow_ref[pl.ds(e, 1), :]

    def expert_ffn(base, n_rows):
        rows = lax.broadcasted_iota(jnp.int32, (n_rows, 1), 0).astype(F32)
        pick = jnp.where(slot_row == rows + base, 1.0, 0.0).astype(BF16)
        xs = _dot(pick, xb_ref[...]).astype(BF16)
        main = D_FF_EXPERT - LANES
        gate = _dot(xs, w1_ref[:, :main])
        up = _dot(xs, w3_ref[:, :main])
        tail = _dot(xs, jnp.concatenate([w1_ref[:, main:], w3_ref[:, main:]], axis=1))
        gate_t, up_t = tail[:, :LANES], tail[:, LANES:]
        hid = jnp.concatenate([gate * _sigmoid(gate) * up, gate_t * _sigmoid(gate_t) * up_t], axis=1)
        return _dot(hid.astype(BF16), w2_ref[...]).astype(BF16)

    parked = pl.ds(pl.multiple_of(e * MOE_SLAB, 16), MOE_SLAB)
    lower = 0
    for size in MOE_FIRST_SLAB_SIZES:
        fits = (count <= size) if size < MOE_SLAB else True

        @pl.when(jnp.logical_and(count > lower, fits) if lower else fits)
        def _(size=size):
            y = expert_ffn(0.0, size)
            if size < MOE_SLAB:
                y = jnp.concatenate([y, jnp.zeros((MOE_SLAB - size, D_MODEL), BF16)], axis=0)
            y_ref[parked, :] = y

        lower = size

    def later_slab(s, carry):
        base = (s * MOE_SLAB).astype(F32)
        pick_e = jnp.where(lax.broadcasted_iota(jnp.int32, (LANES, LANES), 0) == e, 1.0, 0.0).astype(BF16)
        gate_b = _dot(tcol_ref[:, :LANES], pick_e)
        slot_b = (SLOT_RADIX * _dot(tcol_ref[:, LANES:2 * LANES], pick_e)
                  + _dot(tcol_ref[:, 2 * LANES:], pick_e) - 1.0)
        cols = lax.broadcasted_iota(jnp.int32, (1, LANES), 1).astype(F32)
        y = jnp.concatenate([expert_ffn(base, MOE_SLAB),
                             jnp.zeros((MOE_SLAB_K - MOE_SLAB, D_MODEL), BF16)], axis=0)
        put = jnp.concatenate(
            [jnp.where(slot_b == cols + (base + k * LANES), gate_b, 0.0).astype(BF16)
             for k in range(MOE_SLAB_K // LANES)], axis=1)
        o_ref[...] += _dot(put, y)
        return carry

    n_slabs = (count + MOE_SLAB - 1) // MOE_SLAB
    lax.fori_loop(1, n_slabs, later_slab, 0)

    @pl.when(e == N_EXPERTS - 1)
    def _():
        p = pcol_ref[...]
        pos_a, gate_a, pos_b, gate_b = p[:, 0:1], p[:, 1:2], p[:, 2:3], p[:, 3:4]
        puts = []
        for k in range(N_EXPERTS * MOE_SLAB // MOE_COMBINE_K):
            cols = lax.broadcasted_iota(jnp.int32, (1, MOE_COMBINE_K), 1).astype(F32) + float(k * MOE_COMBINE_K)
            puts.append(jnp.where(pos_a == cols, gate_a, jnp.where(pos_b == cols, gate_b, 0.0)).astype(BF16))
        acc = DEEPNORM_ALPHA * x_ref[...] + o_ref[...] + _dot(jnp.concatenate(puts, axis=1), y_ref[...])
        o_ref[...] = _layer_norm(acc, g_ref[...], b_ref[...])


def _moe(x, router_w, router_b, w1, w3, w2, g, b):
    t = x.shape[0]
    tm = MOE_TOKEN_TILE
    xb, tcol, pcol, drow, cnt = _route(x, router_w, router_b)
    counts = cnt[:, 0, :N_EXPERTS].reshape(-1)
    row = lambda i, e, c: (i, 0)
    const = lambda i, e, c: (0, 0)
    expert = lambda i, e, c: (e, 0, 0)
    grid_spec = pltpu.PrefetchScalarGridSpec(
        num_scalar_prefetch=1,
        grid=(t // tm, N_EXPERTS),
        in_specs=[
            pl.BlockSpec((tm, D_MODEL), row),
            pl.BlockSpec((tm, D_MODEL), row),
            pl.BlockSpec((tm, 3 * LANES), row),
            pl.BlockSpec((tm, LANES), row),
            pl.BlockSpec((N_EXPERTS, tm), row),
            pl.BlockSpec((None, D_MODEL, D_FF_EXPERT), expert),
            pl.BlockSpec((None, D_MODEL, D_FF_EXPERT), expert),
            pl.BlockSpec((None, D_FF_EXPERT, D_MODEL), expert),
            pl.BlockSpec((1, D_MODEL), const),
            pl.BlockSpec((1, D_MODEL), const),
        ],
        out_specs=pl.BlockSpec((tm, D_MODEL), row),
        scratch_shapes=[pltpu.VMEM((N_EXPERTS * MOE_SLAB, D_MODEL), BF16)],
    )
    return pl.pallas_call(
        _moe_kernel,
        grid_spec=grid_spec,
        out_shape=jax.ShapeDtypeStruct((t, D_MODEL), F32),
        compiler_params=_params("parallel", "arbitrary"),
        name="moe",
    )(counts, x, xb, tcol, pcol, drow, w1, w3, w2, g.reshape(1, -1), b.reshape(1, -1))


def _attn_bias_table():
    qi = np.arange(BLK)[:, None]
    kj = np.arange(3 * BLK)[None, :]
    dist = np.abs(qi - kj + BLK)
    slopes = np.exp2(-8.0 * np.arange(1, N_Q_HEADS + 1) / N_Q_HEADS)
    bias = -(LOG2E * slopes)[:, None, None] * dist[None]
    in_window = (dist <= WINDOW)[None]
    variants = []
    for ok in (kj >= BLK, kj >= 0, kj < 2 * BLK):
        per_head = np.where(in_window & ok[None], bias, NEG_INF)
        per_head = per_head.reshape(N_KV_HEADS, 2, 2, BLK, 3 * BLK)
        variants.append(np.transpose(per_head, (0, 2, 1, 3, 4)).reshape(N_KV_HEADS, 2, 2 * BLK, 3 * BLK))
    return np.stack(variants).astype(np.float32)


def _retention_tables(decay_fwd, decay_bwd):
    pos = jnp.arange(BLK, dtype=F32)
    diff = jnp.arange(BLK)[:, None] - jnp.arange(BLK)[None, :]
    lg_f = jax.nn.log_sigmoid(decay_fwd.astype(F32))
    lg_b = jax.nn.log_sigmoid(decay_bwd.astype(F32))

    def intra(lg, d, mask):
        dec = jnp.where(mask[None], jnp.exp(jnp.maximum(d, 0).astype(F32)[None] * lg[:, None, None]), 0.0)
        return jnp.transpose(dec, (1, 0, 2)).reshape(BLK, N_RET_HEADS * BLK)

    def per_pos(lg, expo, width):
        return jnp.repeat(jnp.exp(expo[:, None] * lg[None, :]), width, axis=1)

    return {
        "intra_f": intra(lg_f, diff, diff >= 0),
        "intra_b": intra(lg_b, -diff, diff < 0),
        "qdec_f": per_pos(lg_f, pos + 1.0, RET_VAL_DIM),
        "qdec_b": per_pos(lg_b, BLK - pos, RET_VAL_DIM),
        "kdec_f": per_pos(lg_f, BLK - 1.0 - pos, RET_KEY_DIM),
        "kdec_b": per_pos(lg_b, pos, RET_KEY_DIM),
        "cdec_f": jnp.repeat(jnp.exp(BLK * lg_f), RET_VAL_DIM)[None, :],
        "cdec_b": jnp.repeat(jnp.exp(BLK * lg_b), RET_VAL_DIM)[None, :],
    }


def _group_avg_matrix():
    head = np.arange(GN_BLOCK) // RET_VAL_DIM
    return jnp.asarray(np.where(head[:, None] == head[None, :], 1.0 / RET_VAL_DIM, 0.0), BF16)


def kernel(x, ln_emb_g, ln_emb_b, w_in, b_gate, sink_logits, decay_fwd, decay_bwd,
           w_o_attn, w_o_ret, w_out, ln1_g, ln1_b, ffn_w1, ffn_w3, ffn_w2,
           router_w, router_b, moe_w1, moe_w3, moe_w2, ln2_g, ln2_b):
    batch, seq, _ = x.shape
    nb = seq // BLK
    xt = x.reshape(batch * seq, D_MODEL)
    attn_bias = _attn_bias_table()
    group_avg = _group_avg_matrix()
    for layer in range(DEPTH):
        tabs = _retention_tables(decay_fwd[layer], decay_bwd[layer])
        tabs["attn_bias"] = attn_bias
        tabs["group_avg"] = group_avg
        w = w_in[layer].astype(BF16)
        if layer == 0:
            h, qa, kva, ret, gl, kvf, prev_b = _in_proj(xt, w, tabs, seq, ln=(ln_emb_g, ln_emb_b))
        else:
            h = xt
            qa, kva, ret, gl, kvf, prev_b = _in_proj(xt, w, tabs, seq)
        x1 = _mix_proj(qa, kva, ret, kvf, prev_b, LOG2E * sink_logits[layer].astype(F32), tabs,
                       h, gl, 0.5 * b_gate[layer], w_o_attn[layer].astype(BF16),
                       w_o_ret[layer].astype(BF16), (0.5 * w_out[layer]).astype(BF16),
                       ln1_g[layer], ln1_b[layer], batch, nb)
        i = layer // 2
        if layer % 2 == 0:
            xt = _ffn(x1, ffn_w1[i].astype(BF16), ffn_w3[i].astype(BF16), ffn_w2[i].astype(BF16),
                      ln2_g[layer], ln2_b[layer])
        else:
            xt = _moe(x1, router_w[i], router_b[i], moe_w1[i].astype(BF16), moe_w3[i].astype(BF16),
                      moe_w2[i].astype(BF16), ln2_g[layer], ln2_b[layer])
    return xt.reshape(batch, seq, D_MODEL)
```

```python
import functools

import jax
import jax.numpy as jnp
import numpy as np
from jax import lax
from jax.experimental import pallas as pl
from jax.experimental.pallas import tpu as pltpu

F32 = jnp.float32
BF16 = jnp.bfloat16

D_MODEL = 1024
DEPTH = 2
N_Q_HEADS = 8
N_KV_HEADS = 2
HEAD_DIM = 64
WINDOW = 128
BLK = 128
ATTN_Q_WIDTH = N_Q_HEADS * HEAD_DIM
ATTN_KV_WIDTH = N_KV_HEADS * HEAD_DIM
N_RET_HEADS = 8
RET_KEY_DIM = 32
RET_VAL_DIM = 64
RET_QK_WIDTH = N_RET_HEADS * RET_KEY_DIM
RET_V_WIDTH = N_RET_HEADS * RET_VAL_DIM
RET_WIDTH = 2 * RET_QK_WIDTH + 3 * RET_V_WIDTH
D_FF_DENSE = 2816
N_EXPERTS = 8
D_FF_EXPERT = 1408
IN_WIDTH = ATTN_Q_WIDTH + 2 * ATTN_KV_WIDTH + RET_WIDTH + 2 * D_MODEL
DEEPNORM_ALPHA = (2 * DEPTH) ** 0.25
LN_EPS = 1e-5
GN_EPS = 1e-5
NEG_INF = -1e30
LOG2E = 1.4426950408889634

TOKEN_TILE = 1024
FFN_TOKEN_TILE = 1024
MOE_TOKEN_TILE = 1024
ROUTE_PREFIX_BLOCK = 256
MOE_SLAB = 288
MOE_FIRST_SLAB_SIZES = (256, 272, MOE_SLAB)
MOE_SLAB_K = 384
MOE_COMBINE_K = 256
GN_BLOCK = 256
LANES = 128
SLOT_RADIX = 32.0
MIX_CHUNKS = 4
FF_CHUNK = 256
VMEM_LIMIT_BYTES = 56 * 1024 * 1024


def _params(*sem):
    return pltpu.CompilerParams(dimension_semantics=sem, vmem_limit_bytes=VMEM_LIMIT_BYTES)


def _layer_norm(x, g, b):
    mu = jnp.mean(x, axis=-1, keepdims=True)
    d = x - mu
    var = jnp.mean(d * d, axis=-1, keepdims=True)
    return d * lax.rsqrt(var + LN_EPS) * g + b


def _sigmoid(x):
    return 1.0 / (1.0 + jnp.exp(-x))


def _dot(a, b):
    return jnp.dot(a, b, preferred_element_type=F32)


def _dot_nt(a, b):
    return lax.dot_general(a, b, (((1,), (1,)), ((), ())), preferred_element_type=F32)


def _head_block_mask(rows, cols, row_div, col_div):
    r = lax.broadcasted_iota(jnp.int32, (rows, cols), 0) // row_div
    c = lax.broadcasted_iota(jnp.int32, (rows, cols), 1) // col_div
    return r == c


def _in_proj_kernel(apply_ln, tiles_per_seq, *refs):
    sb_ref = refs[-1]
    tile = pl.num_programs(0) - 1 - pl.program_id(0)

    @pl.when(lax.rem(tile, tiles_per_seq) == tiles_per_seq - 1)
    def _():
        sb_ref[...] = jnp.zeros_like(sb_ref)

    if apply_ln:
        (x_ref, g_ref, b_ref, w_ref, kdf_ref, kdb_ref, cdb_ref,
         h_ref, qa_ref, kva_ref, ret_ref, gl_ref, kvf_ref, pb_ref, _) = refs
        x = _layer_norm(x_ref[...], g_ref[...], b_ref[...])
        h_ref[...] = x
    else:
        (x_ref, w_ref, kdf_ref, kdb_ref, cdb_ref,
         qa_ref, kva_ref, ret_ref, gl_ref, kvf_ref, pb_ref, _) = refs
        x = x_ref[...]
    xb = x.astype(BF16)

    def proj(lo, hi):
        return _dot(xb, w_ref[:, lo:hi])

    c0 = ATTN_Q_WIDTH
    c1 = c0 + 2 * ATTN_KV_WIDTH
    qa_ref[...] = (proj(0, c0) * (HEAD_DIM ** -0.5 * LOG2E)).astype(BF16)
    kva_ref[...] = proj(c0, c1).astype(BF16)
    ret_ref[:, :RET_QK_WIDTH] = proj(c1, c1 + RET_QK_WIDTH).astype(BF16)
    rk = (proj(c1 + RET_QK_WIDTH, c1 + 2 * RET_QK_WIDTH) * (RET_KEY_DIM ** -0.5)).astype(BF16)
    ret_ref[:, RET_QK_WIDTH:2 * RET_QK_WIDTH] = rk
    rest = proj(c1 + 2 * RET_QK_WIDTH, c1 + RET_WIDTH).astype(BF16)
    ret_ref[:, 2 * RET_QK_WIDTH:] = rest
    gl_ref[...] = (0.5 * proj(c1 + RET_WIDTH, IN_WIDTH)).astype(BF16)

    half = N_RET_HEADS // 2
    mask = _head_block_mask(half * RET_KEY_DIM, half * RET_VAL_DIM, RET_KEY_DIM, RET_VAL_DIM)
    def increment(c, kd_ref):
        rows = slice(c * BLK, (c + 1) * BLK)
        kz = (rk[rows].astype(F32) * kd_ref[...]).T.astype(BF16)
        v = rest[rows, :RET_V_WIDTH]
        comps = []
        for g in range(2):
            full = jnp.where(mask, _dot(kz[g * half * RET_KEY_DIM:(g + 1) * half * RET_KEY_DIM],
                                        v[:, g * half * RET_VAL_DIM:(g + 1) * half * RET_VAL_DIM]), 0.0)
            comp = full[0:RET_KEY_DIM]
            for h in range(1, half):
                comp = comp + full[h * RET_KEY_DIM:(h + 1) * RET_KEY_DIM]
            comps.append(comp)
        return jnp.concatenate(comps, axis=1)

    state = sb_ref[...]
    for c in reversed(range(x.shape[0] // BLK)):
        kvf_ref[c] = increment(c, kdf_ref)
        pb_ref[c] = state
        state = state * cdb_ref[...] + increment(c, kdb_ref)
    sb_ref[...] = state


def _in_proj(x, w, tabs, seq, ln=None):
    t = x.shape[0]
    tm = TOKEN_TILE
    tiles = t // tm
    row = lambda i: (tiles - 1 - i, 0)
    const = lambda i: (0, 0)
    in_specs = [pl.BlockSpec((tm, D_MODEL), row)]
    args = [x]
    out_shape = []
    out_specs = []
    if ln is not None:
        in_specs += [pl.BlockSpec((1, D_MODEL), const), pl.BlockSpec((1, D_MODEL), const)]
        args += [ln[0].reshape(1, D_MODEL), ln[1].reshape(1, D_MODEL)]
        out_shape.append(jax.ShapeDtypeStruct((t, D_MODEL), F32))
        out_specs.append(pl.BlockSpec((tm, D_MODEL), row))
    in_specs += [pl.BlockSpec((D_MODEL, IN_WIDTH), const),
                 pl.BlockSpec((BLK, RET_QK_WIDTH), const), pl.BlockSpec((BLK, RET_QK_WIDTH), const),
                 pl.BlockSpec((1, RET_V_WIDTH), const)]
    args += [w, tabs["kdec_f"], tabs["kdec_b"], tabs["cdec_b"]]
    for width in (ATTN_Q_WIDTH, 2 * ATTN_KV_WIDTH, RET_WIDTH, 2 * D_MODEL):
        out_shape.append(jax.ShapeDtypeStruct((t, width), BF16))
        out_specs.append(pl.BlockSpec((tm, width), row))
    for _ in range(2):
        out_shape.append(jax.ShapeDtypeStruct((t // BLK, RET_KEY_DIM, RET_V_WIDTH), F32))
        out_specs.append(pl.BlockSpec((tm // BLK, RET_KEY_DIM, RET_V_WIDTH), lambda i: (tiles - 1 - i, 0, 0)))
    return pl.pallas_call(
        functools.partial(_in_proj_kernel, ln is not None, seq // tm),
        grid=(tiles,),
        in_specs=in_specs,
        out_specs=out_specs,
        out_shape=out_shape,
        scratch_shapes=[pltpu.VMEM((RET_KEY_DIM, RET_V_WIDTH), F32)],
        compiler_params=_params("arbitrary"),
        name="in_proj",
    )(*args)


def _split_bf16(y):
    hi = y.astype(BF16)
    lo = (y - hi.astype(F32)).astype(BF16)
    return hi, lo


def _mixer_body(n, nb, sink_ref, qa_ref, kvp_ref, kvc_ref, kvn_ref, ret_ref, pf_ref, pb_ref,
                bias_ref, df_ref, db_ref, qdf_ref, qdb_ref, gavg_ref, ya_ref, r_ref):
    q = qa_ref[...]
    kv = jnp.concatenate([kvp_ref[...], kvc_ref[...], kvn_ref[...]], axis=0).astype(F32)
    lo_half = lax.broadcasted_iota(jnp.int32, (kv.shape[0], 2 * HEAD_DIM), 1) < HEAD_DIM

    def lane_variants(a):
        swapped = pltpu.roll(a, HEAD_DIM, 1)
        z = jnp.zeros_like(a)
        return (((jnp.where(lo_half, a, z)).astype(BF16), (jnp.where(lo_half, z, swapped)).astype(BF16)),
                ((jnp.where(lo_half, swapped, z)).astype(BF16), (jnp.where(lo_half, z, a)).astype(BF16)))

    k_var = lane_variants(kv[:, :ATTN_KV_WIDTH])
    v_var = lane_variants(kv[:, ATTN_KV_WIDTH:])
    top_rows = lax.broadcasted_iota(jnp.int32, (2 * BLK, 1), 0) < BLK
    out_lo_half = lax.broadcasted_iota(jnp.int32, (2 * BLK, 2 * HEAD_DIM), 1) < HEAD_DIM
    half_heads = N_RET_HEADS // 2
    k_mask = _head_block_mask(N_RET_HEADS * BLK, RET_QK_WIDTH, BLK, RET_KEY_DIM)
    v_mask = _head_block_mask(half_heads * BLK, half_heads * RET_VAL_DIM, BLK, RET_VAL_DIM)
    s_mask = _head_block_mask(RET_QK_WIDTH, RET_V_WIDTH, RET_KEY_DIM, RET_VAL_DIM)
    pair = 2 * HEAD_DIM
    ys = []
    for j in range(MIX_CHUNKS):
        c = n * MIX_CHUNKS + j
        var = jnp.where(c == 0, 0, jnp.where(c == nb - 1, 2, 1))
        tok = slice(j * BLK, (j + 1) * BLK)
        keys = slice(j * BLK, (j + 3) * BLK)

        for g in range(N_KV_HEADS):
            q2 = jnp.concatenate([q[tok, pair * 2 * g:pair * (2 * g + 1)],
                                  q[tok, pair * (2 * g + 1):pair * (2 * g + 2)]], axis=0)
            s_all = _dot_nt(q2, jnp.concatenate([k_var[g][0][keys], k_var[g][1][keys]], axis=0))
            probs, denoms = [], []
            for r in range(2):
                sink = jnp.where(top_rows, sink_ref[4 * g + r], sink_ref[4 * g + 2 + r])
                s = s_all[:, 3 * BLK * r:3 * BLK * (r + 1)] + bias_ref[var, g, r]
                m = jnp.maximum(jnp.max(s, axis=1, keepdims=True), sink)
                e = jnp.exp2(s - m)
                denoms.append(jnp.sum(e, axis=1, keepdims=True) + jnp.exp2(sink - m))
                probs.append(e.astype(BF16))
            pv = _dot(jnp.concatenate(probs, axis=1),
                      jnp.concatenate([v_var[g][0][keys], v_var[g][1][keys]], axis=0))
            out = pv / jnp.where(out_lo_half, denoms[0], denoms[1])
            ya_ref[tok, pair * 2 * g:pair * (2 * g + 1)] = out[:BLK].astype(BF16)
            ya_ref[tok, pair * (2 * g + 1):pair * (2 * g + 2)] = out[BLK:].astype(BF16)

        rq = ret_ref[tok, :RET_QK_WIDTH]
        rk = ret_ref[tok, RET_QK_WIDTH:2 * RET_QK_WIDTH]
        rv = ret_ref[tok, 2 * RET_QK_WIDTH:2 * RET_QK_WIDTH + RET_V_WIDTH]
        k_exp = jnp.where(k_mask, jnp.concatenate([rk] * N_RET_HEADS, axis=0), jnp.zeros((), BF16))
        qk = _dot_nt(rq, k_exp)
        p_fb = jnp.concatenate([(qk * df_ref[...]).astype(BF16), (qk * db_ref[...]).astype(BF16)], axis=0)
        intra = []
        for g in range(2):
            cols = slice(g * half_heads * RET_VAL_DIM, (g + 1) * half_heads * RET_VAL_DIM)
            v_bd = jnp.where(v_mask, jnp.concatenate([rv[:, cols]] * half_heads, axis=0), jnp.zeros((), BF16))
            intra.append(_dot(p_fb[:, g * half_heads * BLK:(g + 1) * half_heads * BLK], v_bd))
        cross = []
        for state_ref, qdec_ref in ((pf_ref, qdf_ref), (pb_ref, qdb_ref)):
            s_bd = jnp.where(s_mask, jnp.concatenate([state_ref[j]] * N_RET_HEADS, axis=0), 0.0)
            cross.append(_dot(rq, s_bd.astype(BF16)) * qdec_ref[...])
        ys.append(jnp.concatenate(intra, axis=1) + jnp.concatenate(cross, axis=0))

    y = jnp.concatenate(ys, axis=0)
    def group_mean(a):
        ab = a.astype(BF16)
        half = gavg_ref.shape[0]
        return jnp.concatenate([_dot(ab[:, k * half:(k + 1) * half], gavg_ref[...])
                                for k in range(RET_V_WIDTH // half)], axis=1)

    d = y - group_mean(y)
    normed = d * lax.rsqrt(group_mean(d * d) + GN_EPS)
    for j in range(MIX_CHUNKS):
        tok = slice(j * BLK, (j + 1) * BLK)
        g_f = ret_ref[tok, 2 * RET_QK_WIDTH + RET_V_WIDTH:2 * RET_QK_WIDTH + 2 * RET_V_WIDTH].astype(F32)
        g_b = ret_ref[tok, 2 * RET_QK_WIDTH + 2 * RET_V_WIDTH:].astype(F32)
        r = (g_f * _sigmoid(g_f) * normed[2 * j * BLK:(2 * j + 1) * BLK]
             + g_b * _sigmoid(g_b) * normed[(2 * j + 1) * BLK:(2 * j + 2) * BLK])
        r_ref[tok, :] = r.astype(BF16)


def _mix_proj_kernel(nb, n_blocks, sink_ref, qa_ref, kvp_ref, kvc_ref, kvn_ref, ret_ref, kvf_ref, pb_ref,
                     bias_ref, df_ref, db_ref, qdf_ref, qdb_ref, gavg_ref, cdf_ref,
                     h_ref, gl_ref, bg_ref, woa_ref, wor_ref, wout_ref, g_ref, b_ref,
                     x_ref, ya_scr, r_scr, sf_ref, pf_ref):
    s = pl.program_id(0)

    n = lax.rem(jnp.minimum(s, n_blocks - 1), nb // MIX_CHUNKS)

    @pl.when(s == 0)
    def _():
        ya_scr[...] = jnp.zeros_like(ya_scr)
        r_scr[...] = jnp.zeros_like(r_scr)

    @pl.when(n == 0)
    def _():
        sf_ref[...] = jnp.zeros_like(sf_ref)

    slot = lax.rem(s, 2)
    ya_prev = ya_scr[1 - slot]
    r_prev = r_scr[1 - slot]
    halves = []
    for c0 in range(0, D_MODEL, D_MODEL // 2):
        cols = slice(c0, c0 + D_MODEL // 2)
        gcols = slice(D_MODEL + c0, D_MODEL + c0 + D_MODEL // 2)
        pa = _dot(ya_prev, woa_ref[:, cols])
        pr = _dot(r_prev, wor_ref[:, cols])
        t_a = jnp.tanh(gl_ref[:, cols].astype(F32) + bg_ref[:, cols])
        t_r = jnp.tanh(gl_ref[:, gcols].astype(F32) + bg_ref[:, gcols])
        halves.append((pa + pr + t_a * pa + t_r * pr).astype(BF16))
    mix = _dot(jnp.concatenate(halves, axis=1), wout_ref[...])
    x_ref[...] = _layer_norm(DEEPNORM_ALPHA * h_ref[...] + mix, g_ref[...], b_ref[...])

    state = sf_ref[...]
    for j in range(MIX_CHUNKS):
        pf_ref[j] = state
        state = state * cdf_ref[...] + kvf_ref[j]
    sf_ref[...] = state

    _mixer_body(n, nb, sink_ref, qa_ref, kvp_ref, kvc_ref, kvn_ref, ret_ref, pf_ref, pb_ref,
                bias_ref, df_ref, db_ref, qdf_ref, qdb_ref, gavg_ref, ya_scr.at[slot], r_scr.at[slot])


def _mix_proj(qa, kva, ret, kvf, prev_b, sink, tabs, h, gl, b_gate, woa, wor, wout, g, b, batch, nb):
    t = qa.shape[0]
    steps = nb // MIX_CHUNKS
    rows = MIX_CHUNKS * BLK
    n_blocks = batch * steps

    def mixed(s):
        return jnp.minimum(s, n_blocks - 1)

    def projected(s):
        return jnp.maximum(s - 1, 0)

    cur = lambda s: (mixed(s), 0)
    prv = lambda s: (mixed(s) * MIX_CHUNKS - jnp.where(lax.rem(mixed(s), steps) == 0, 0, 1), 0)
    nxt = lambda s: (mixed(s) * MIX_CHUNKS + MIX_CHUNKS - jnp.where(lax.rem(mixed(s), steps) == steps - 1, 1, 0), 0)
    out = lambda s: (projected(s), 0)
    c2 = lambda s: (0, 0)
    state_spec = pl.BlockSpec((MIX_CHUNKS, RET_KEY_DIM, RET_V_WIDTH), lambda s: (mixed(s), 0, 0))
    return pl.pallas_call(
        functools.partial(_mix_proj_kernel, nb, n_blocks),
        grid=(n_blocks + 1,),
        in_specs=[
            pl.BlockSpec(memory_space=pltpu.SMEM),
            pl.BlockSpec((rows, ATTN_Q_WIDTH), cur),
            pl.BlockSpec((BLK, 2 * ATTN_KV_WIDTH), prv),
            pl.BlockSpec((rows, 2 * ATTN_KV_WIDTH), cur),
            pl.BlockSpec((BLK, 2 * ATTN_KV_WIDTH), nxt),
            pl.BlockSpec((rows, RET_WIDTH), cur),
            state_spec,
            state_spec,
            pl.BlockSpec((3, N_KV_HEADS, 2, 2 * BLK, 3 * BLK), lambda s: (0, 0, 0, 0, 0)),
            pl.BlockSpec((BLK, N_RET_HEADS * BLK), c2),
            pl.BlockSpec((BLK, N_RET_HEADS * BLK), c2),
            pl.BlockSpec((BLK, RET_V_WIDTH), c2),
            pl.BlockSpec((BLK, RET_V_WIDTH), c2),
            pl.BlockSpec((GN_BLOCK, GN_BLOCK), c2),
            pl.BlockSpec((1, RET_V_WIDTH), c2),
            pl.BlockSpec((rows, D_MODEL), out),
            pl.BlockSpec((rows, 2 * D_MODEL), out),
            pl.BlockSpec((1, 2 * D_MODEL), c2),
            pl.BlockSpec((ATTN_Q_WIDTH, D_MODEL), c2),
            pl.BlockSpec((RET_V_WIDTH, D_MODEL), c2),
            pl.BlockSpec((D_MODEL, D_MODEL), c2),
            pl.BlockSpec((1, D_MODEL), c2),
            pl.BlockSpec((1, D_MODEL), c2),
        ],
        out_specs=pl.BlockSpec((rows, D_MODEL), out),
        out_shape=jax.ShapeDtypeStruct((t, D_MODEL), F32),
        scratch_shapes=[pltpu.VMEM((2, rows, ATTN_Q_WIDTH), BF16), pltpu.VMEM((2, rows, RET_V_WIDTH), BF16),
                        pltpu.VMEM((RET_KEY_DIM, RET_V_WIDTH), F32),
                        pltpu.VMEM((MIX_CHUNKS, RET_KEY_DIM, RET_V_WIDTH), F32)],
        compiler_params=_params("arbitrary"),
        name="mix_proj",
    )(sink, qa, kva, kva, kva, ret, kvf, prev_b, tabs["attn_bias"], tabs["intra_f"],
      tabs["intra_b"], tabs["qdec_f"], tabs["qdec_b"], tabs["group_avg"], tabs["cdec_f"],
      h, gl, b_gate.reshape(1, -1), woa, wor, wout, g.reshape(1, -1), b.reshape(1, -1))


def _ffn_kernel(x_ref, w1_ref, w3_ref, w2_ref, g_ref, b_ref, o_ref):
    x = x_ref[...]
    xb = x.astype(BF16)
    hidden = []
    for c in range(D_FF_DENSE // FF_CHUNK):
        cols = slice(c * FF_CHUNK, (c + 1) * FF_CHUNK)
        a = _dot(xb, w1_ref[:, cols])
        hidden.append((a * _sigmoid(a) * _dot(xb, w3_ref[:, cols])).astype(BF16))
    acc = _dot(jnp.concatenate(hidden, axis=1), w2_ref[...])
    o_ref[...] = _layer_norm(DEEPNORM_ALPHA * x + acc, g_ref[...], b_ref[...])


def _ffn(x, w1, w3, w2, g, b):
    t = x.shape[0]
    tm = FFN_TOKEN_TILE
    row = lambda i: (i, 0)
    const = lambda i: (0, 0)
    return pl.pallas_call(
        _ffn_kernel,
        grid=(t // tm,),
        in_specs=[
            pl.BlockSpec((tm, D_MODEL), row),
            pl.BlockSpec((D_MODEL, D_FF_DENSE), const),
            pl.BlockSpec((D_MODEL, D_FF_DENSE), const),
            pl.BlockSpec((D_FF_DENSE, D_MODEL), const),
            pl.BlockSpec((1, D_MODEL), const),
            pl.BlockSpec((1, D_MODEL), const),
        ],
        out_specs=pl.BlockSpec((tm, D_MODEL), row),
        out_shape=jax.ShapeDtypeStruct((t, D_MODEL), F32),
        compiler_params=_params("parallel"),
        name="ffn",
    )(x, w1, w3, w2, g.reshape(1, -1), b.reshape(1, -1))


def _route_kernel(x_ref, rw_ref, rb_ref, tri_ref, xb_ref, tcol_ref, pcol_ref, drow_ref, cnt_ref):
    x = x_ref[...]
    tm, lanes = x.shape[0], LANES
    lane = lax.broadcasted_iota(jnp.int32, (tm, lanes), 1)
    xh, xl = _split_bf16(x)
    wh, wl = _split_bf16(rw_ref[...])
    hi_part = _dot(xh, jnp.concatenate([wh, wl], axis=1))
    logits = hi_part[:, :lanes] + hi_part[:, lanes:] + _dot(xl, wh) + rb_ref[...]
    logits = jnp.where(lane < N_EXPERTS, logits, NEG_INF)
    m1 = jnp.max(logits, axis=1, keepdims=True)
    i1 = jnp.min(jnp.where(logits == m1, lane, lanes), axis=1, keepdims=True)
    rest = jnp.where(lane == i1, NEG_INF, logits)
    m2 = jnp.max(rest, axis=1, keepdims=True)
    i2 = jnp.min(jnp.where(rest == m2, lane, lanes), axis=1, keepdims=True)
    e2 = jnp.exp(m2 - m1)
    w_top = 1.0 / (1.0 + e2)
    gate =jnp.where(lane == i1, w_top, 0.0) + jnp.where(lane == i2, e2 * w_top, 0.0)
    sel = jnp.where(lane == i1, 1.0, jnp.where(lane == i2, 1.0, 0.0))
    sel_b = sel.astype(BF16)
    blk = tri_ref.shape[0]
    parts, above = [], jnp.zeros((1, lanes), F32)
    for r0 in range(0, tm, blk):
        local = _dot(tri_ref[...], sel_b[r0:r0 + blk])
        parts.append(local + above)
        above = above + local[blk - 1:blk]
    prefix = jnp.concatenate(parts, axis=0)
    slot1 = jnp.where(sel > 0.0, prefix, 0.0)
    hi = jnp.floor(slot1 * (1.0 / SLOT_RADIX))
    tcol_ref[:, :lanes] = gate.astype(BF16)
    tcol_ref[:, lanes:2 * lanes] = hi.astype(BF16)
    tcol_ref[:, 2 * lanes:] = (slot1 - SLOT_RADIX * hi).astype(BF16)

    def first_slab_pos(idx, picked):
        rank = jnp.sum(jnp.where(picked, prefix, 0.0), axis=1, keepdims=True) - 1.0
        return jnp.where(rank < MOE_SLAB, idx.astype(F32) * MOE_SLAB + rank, -1.0)

    pcol_ref[...] = jnp.where(
        lane == 0, first_slab_pos(i1, lane == i1), jnp.where(
            lane == 1, w_top, jnp.where(
                lane == 2, first_slab_pos(i2, lane == i2), jnp.where(lane == 3, e2 * w_top, 0.0))))
    drow_ref[...] = (slot1 - 1.0).T[:N_EXPERTS]
    cnt_ref[...] = prefix[tm - 1:tm].astype(jnp.int32)
    xb_ref[...] = x.astype(BF16)


def _route(x, router_w, router_b):
    t = x.shape[0]
    tm = MOE_TOKEN_TILE
    lanes = 128
    rw = jnp.zeros((D_MODEL, lanes), F32).at[:, :N_EXPERTS].set(router_w)
    rb = jnp.zeros((1, lanes), F32).at[0, :N_EXPERTS].set(router_b)
    tri = jnp.asarray(np.arange(ROUTE_PREFIX_BLOCK)[:, None] >= np.arange(ROUTE_PREFIX_BLOCK)[None, :], BF16)
    row = lambda i: (i, 0)
    const = lambda i: (0, 0)
    return pl.pallas_call(
        _route_kernel,
        grid=(t // tm,),
        in_specs=[
            pl.BlockSpec((tm, D_MODEL), row),
            pl.BlockSpec((D_MODEL, lanes), const),
            pl.BlockSpec((1, lanes), const),
            pl.BlockSpec((ROUTE_PREFIX_BLOCK, ROUTE_PREFIX_BLOCK), const),
        ],
        out_specs=[
            pl.BlockSpec((tm, D_MODEL), row),
            pl.BlockSpec((tm, 3 * lanes), row),
            pl.BlockSpec((tm, lanes), row),
            pl.BlockSpec((N_EXPERTS, tm), row),
            pl.BlockSpec((None, 1, lanes), lambda i: (i, 0, 0)),
        ],
        out_shape=[
            jax.ShapeDtypeStruct((t, D_MODEL), BF16),
            jax.ShapeDtypeStruct((t, 3 * lanes), BF16),
            jax.ShapeDtypeStruct((t, lanes), F32),
            jax.ShapeDtypeStruct((t // tm * N_EXPERTS, tm), F32),
            jax.ShapeDtypeStruct((t // tm, 1, lanes), jnp.int32),
        ],
        compiler_params=_params("parallel"),
        name="route",
    )(x, rw, rb, tri)


def _moe_kernel(cnt_ref, x_ref, xb_ref, tcol_ref, pcol_ref, drow_ref, w1_ref, w3_ref, w2_ref,
                g_ref, b_ref, o_ref, y_ref):
    i = pl.program_id(0)
    e = pl.program_id(1)
    count = cnt_ref[i * N_EXPERTS + e]

    @pl.when(e == 0)
    def _():
        o_ref[...] = jnp.zeros_like(o_ref)

    slot_row = drow_ref[pl.ds(e, 1), :]

    def expert_ffn(base, n_rows):
        rows = lax.broadcasted_iota(jnp.int32, (n_rows, 1), 0).astype(F32)
        pick = jnp.where(slot_row == rows + base, 1.0, 0.0).astype(BF16)
        xs = _dot(pick, xb_ref[...]).astype(BF16)
        main = D_FF_EXPERT - LANES
        gate = _dot(xs, w1_ref[:, :main])
        up = _dot(xs, w3_ref[:, :main])
        tail = _dot(xs, jnp.concatenate([w1_ref[:, main:], w3_ref[:, main:]], axis=1))
        gate_t, up_t = tail[:, :LANES], tail[:, LANES:]
        hid = jnp.concatenate([gate * _sigmoid(gate) * up, gate_t * _sigmoid(gate_t) * up_t], axis=1)
        return _dot(hid.astype(BF16), w2_ref[...]).astype(BF16)

    parked = pl.ds(pl.multiple_of(e * MOE_SLAB, 16), MOE_SLAB)
    lower = 0
    for size in MOE_FIRST_SLAB_SIZES:
        fits = (count <= size) if size < MOE_SLAB else True

        @pl.when(jnp.logical_and(count > lower, fits) if lower else fits)
        def _(size=size):
            y = expert_ffn(0.0, size)
            if size < MOE_SLAB:
                y = jnp.concatenate([y, jnp.zeros((MOE_SLAB - size, D_MODEL), BF16)], axis=0)
            y_ref[parked, :] = y

        lower = size

    def later_slab(s, carry):
        base = (s * MOE_SLAB).astype(F32)
        pick_e = jnp.where(lax.broadcasted_iota(jnp.int32, (LANES, LANES), 0) == e, 1.0, 0.0).astype(BF16)
        gate_b = _dot(tcol_ref[:, :LANES], pick_e)
        slot_b = (SLOT_RADIX * _dot(tcol_ref[:, LANES:2 * LANES], pick_e)
                  + _dot(tcol_ref[:, 2 * LANES:], pick_e) - 1.0)
        cols = lax.broadcasted_iota(jnp.int32, (1, LANES), 1).astype(F32)
        y = jnp.concatenate([expert_ffn(base, MOE_SLAB),
                             jnp.zeros((MOE_SLAB_K - MOE_SLAB, D_MODEL), BF16)], axis=0)
        put = jnp.concatenate(
            [jnp.where(slot_b == cols + (base + k * LANES), gate_b, 0.0).astype(BF16)
             for k in range(MOE_SLAB_K // LANES)], axis=1)
        o_ref[...] += _dot(put, y)
        return carry

    n_slabs = (count + MOE_SLAB - 1) // MOE_SLAB
    lax.fori_loop(1, n_slabs, later_slab, 0)

    @pl.when(e == N_EXPERTS - 1)
    def _():
        p = pcol_ref[...]
        pos_a, gate_a, pos_b, gate_b = p[:, 0:1], p[:, 1:2], p[:, 2:3], p[:, 3:4]
        puts = []
        for k in range(N_EXPERTS * MOE_SLAB // MOE_COMBINE_K):
            cols = lax.broadcasted_iota(jnp.int32, (1, MOE_COMBINE_K), 1).astype(F32) + float(k * MOE_COMBINE_K)
            puts.append(jnp.where(pos_a == cols, gate_a, jnp.where(pos_b == cols, gate_b, 0.0)).astype(BF16))
        acc = DEEPNORM_ALPHA * x_ref[...] + o_ref[...] + _dot(jnp.concatenate(puts, axis=1), y_ref[...])
        o_ref[...] = _layer_norm(acc, g_ref[...], b_ref[...])


def _moe(x, router_w, router_b, w1, w3, w2, g, b):
    t = x.shape[0]
    tm = MOE_TOKEN_TILE
    xb, tcol, pcol, drow, cnt = _route(x, router_w, router_b)
    counts = cnt[:, 0, :N_EXPERTS].reshape(-1)
    row = lambda i, e, c: (i, 0)
    const = lambda i, e, c: (0, 0)
    expert = lambda i, e, c: (e, 0, 0)
    grid_spec = pltpu.PrefetchScalarGridSpec(
        num_scalar_prefetch=1,
        grid=(t // tm, N_EXPERTS),
        in_specs=[
            pl.BlockSpec((tm, D_MODEL), row),
            pl.BlockSpec((tm, D_MODEL), row),
            pl.BlockSpec((tm, 3 * LANES), row),
            pl.BlockSpec((tm, LANES), row),
            pl.BlockSpec((N_EXPERTS, tm), row),
            pl.BlockSpec((None, D_MODEL, D_FF_EXPERT), expert),
            pl.BlockSpec((None, D_MODEL, D_FF_EXPERT), expert),
            pl.BlockSpec((None, D_FF_EXPERT, D_MODEL), expert),
            pl.BlockSpec((1, D_MODEL), const),
            pl.BlockSpec((1, D_MODEL), const),
        ],
        out_specs=pl.BlockSpec((tm, D_MODEL), row),
        scratch_shapes=[pltpu.VMEM((N_EXPERTS * MOE_SLAB, D_MODEL), BF16)],
    )
    return pl.pallas_call(
        _moe_kernel,
        grid_spec=grid_spec,
        out_shape=jax.ShapeDtypeStruct((t, D_MODEL), F32),
        compiler_params=_params("parallel", "arbitrary"),
        name="moe",
    )(counts, x, xb, tcol, pcol, drow, w1, w3, w2, g.reshape(1, -1), b.reshape(1, -1))


def _attn_bias_table():
    qi = np.arange(BLK)[:, None]
    kj = np.arange(3 * BLK)[None, :]
    dist = np.abs(qi - kj + BLK)
    slopes = np.exp2(-8.0 * np.arange(1, N_Q_HEADS + 1) / N_Q_HEADS)
    bias = -(LOG2E * slopes)[:, None, None] * dist[None]
    in_window = (dist <= WINDOW)[None]
    variants = []
    for ok in (kj >= BLK, kj >= 0, kj < 2 * BLK):
        per_head = np.where(in_window & ok[None], bias, NEG_INF)
        per_head = per_head.reshape(N_KV_HEADS, 2, 2, BLK, 3 * BLK)
        variants.append(np.transpose(per_head, (0, 2, 1, 3, 4)).reshape(N_KV_HEADS, 2, 2 * BLK, 3 * BLK))
    return np.stack(variants).astype(np.float32)


def _retention_tables(decay_fwd, decay_bwd):
    pos = jnp.arange(BLK, dtype=F32)
    diff = jnp.arange(BLK)[:, None] - jnp.arange(BLK)[None, :]
    lg_f = jax.nn.log_sigmoid(decay_fwd.astype(F32))
    lg_b = jax.nn.log_sigmoid(decay_bwd.astype(F32))

    def intra(lg, d, mask):
        dec = jnp.where(mask[None], jnp.exp(jnp.maximum(d, 0).astype(F32)[None] * lg[:, None, None]), 0.0)
        return jnp.transpose(dec, (1, 0, 2)).reshape(BLK, N_RET_HEADS * BLK)

    def per_pos(lg, expo, width):
        return jnp.repeat(jnp.exp(expo[:, None] * lg[None, :]), width, axis=1)

    return {
        "intra_f": intra(lg_f, diff, diff >= 0),
        "intra_b": intra(lg_b, -diff, diff < 0),
        "qdec_f": per_pos(lg_f, pos + 1.0, RET_VAL_DIM),
        "qdec_b": per_pos(lg_b, BLK - pos, RET_VAL_DIM),
        "kdec_f": per_pos(lg_f, BLK - 1.0 - pos, RET_KEY_DIM),
        "kdec_b": per_pos(lg_b, pos, RET_KEY_DIM),
        "cdec_f": jnp.repeat(jnp.exp(BLK * lg_f), RET_VAL_DIM)[None, :],
        "cdec_b": jnp.repeat(jnp.exp(BLK * lg_b), RET_VAL_DIM)[None, :],
    }


def _group_avg_matrix():
    head = np.arange(GN_BLOCK) // RET_VAL_DIM
    return jnp.asarray(np.where(head[:, None] == head[None, :], 1.0 / RET_VAL_DIM, 0.0), BF16)


def kernel(x, ln_emb_g, ln_emb_b, w_in, b_gate, sink_logits, decay_fwd, decay_bwd,
           w_o_attn, w_o_ret, w_out, ln1_g, ln1_b, ffn_w1, ffn_w3, ffn_w2,
           router_w, router_b, moe_w1, moe_w3, moe_w2, ln2_g, ln2_b):
    batch, seq, _ = x.shape
    nb = seq // BLK
    xt = x.reshape(batch * seq, D_MODEL)
    attn_bias = _attn_bias_table()
    group_avg = _group_avg_matrix()
    for layer in range(DEPTH):
        tabs = _retention_tables(decay_fwd[layer], decay_bwd[layer])
        tabs["attn_bias"] = attn_bias
        tabs["group_avg"] = group_avg
        w = w_in[layer].astype(BF16)
        if layer == 0:
            h, qa, kva, ret, gl, kvf, prev_b = _in_proj(xt, w, tabs, seq, ln=(ln_emb_g, ln_emb_b))
        else:
            h = xt
            qa, kva, ret, gl, kvf, prev_b = _in_proj(xt, w, tabs, seq)
        x1 = _mix_proj(qa, kva, ret, kvf, prev_b, LOG2E * sink_logits[layer].astype(F32), tabs,
                       h, gl, 0.5 * b_gate[layer], w_o_attn[layer].astype(BF16),
                       w_o_ret[layer].astype(BF16), (0.5 * w_out[layer]).astype(BF16),
                       ln1_g[layer], ln1_b[layer], batch, nb)
        i = layer // 2
        if layer % 2 == 0:
            xt = _ffn(x1, ffn_w1[i].astype(BF16), ffn_w3[i].astype(BF16), ffn_w2[i].astype(BF16),
                      ln2_g[layer], ln2_b[layer])
        else:
            xt = _moe(x1, router_w[i], router_b[i], moe_w1[i].astype(BF16), moe_w3[i].astype(BF16),
                      moe_w2[i].astype(BF16), ln2_g[layer], ln2_b[layer])
    return xt.reshape(batch, seq, D_MODEL)
```
